```python
import jax, jax.numpy as jnp
from jax import lax
import numpy as np


D_MODEL = 2048
BATCH = 4
SEQ = 2048
DEPTH = 2

HEAD_DIM = 128
ROPE_THETA = 10000.0
NORM_EPS = 1e-6
Q_BLOCK = 128
NEG = -1e30

NSA_HEADS = D_MODEL // (2 * HEAD_DIM)
NSA_KV_GROUPS = 2
CMP_LEN = 32
CMP_STRIDE = 16
CMP_HIDDEN = 256
SLC_LEN = 64
SLC_TOP = 8
WIN = 512
FOX_HEADS = D_MODEL // (2 * HEAD_DIM)
MLA_HEADS = D_MODEL // HEAD_DIM
MLA_Q_RANK = 512
MLA_KV_RANK = 512
MLA_NOPE = 128
MLA_ROPE = 64
MLA_V = 128
D_FF = 5632

NSA_Q = NSA_HEADS * HEAD_DIM
NSA_KV = NSA_KV_GROUPS * HEAD_DIM
FOX_W = FOX_HEADS * HEAD_DIM
HYB_SPLITS = (NSA_Q, NSA_KV, NSA_KV, NSA_KV, NSA_KV, NSA_KV, NSA_KV, 3 * NSA_HEADS, FOX_W, FOX_W, FOX_W, FOX_HEADS)
HYB_IN = sum(HYB_SPLITS)
HYB_OUT = NSA_Q + FOX_W
MLA_SPLITS = (MLA_Q_RANK, MLA_KV_RANK, MLA_ROPE)
MLA_IN = sum(MLA_SPLITS)
N_EVEN = (DEPTH + 1) // 2
N_ODD = DEPTH // 2

kernel_name = 'hybrid_nsa_fox_mla_macaron'


def split_cols(a, sizes):
    out, start = [], 0
    for s in sizes:
        out.append(a[..., start:start + s])
        start += s
    return out


def rms_norm(x, g):
    x32 = x.astype(jnp.float32)
    y = x32 * lax.rsqrt(jnp.mean(x32 * x32, axis=-1, keepdims=True) + NORM_EPS)
    return (y * g.astype(jnp.float32)).astype(x.dtype)


def rope(x, pos):
    d = x.shape[-1]
    inv = 1.0 / (ROPE_THETA ** (np.arange(0, d, 2, dtype=np.float32) / d))
    ang = pos.astype(jnp.float32)[:, None] * jnp.asarray(inv, jnp.float32)[None, :]
    cos = jnp.cos(ang)[None, :, None, :]
    sin = jnp.sin(ang)[None, :, None, :]
    x32 = x.astype(jnp.float32)
    x1, x2 = x32[..., : d // 2], x32[..., d // 2:]
    return jnp.concatenate([x1 * cos - x2 * sin, x2 * cos + x1 * sin], axis=-1).astype(x.dtype)


def swiglu(x, w_in, w_out):
    h = x @ w_in
    return (jax.nn.silu(h[..., :D_FF]) * h[..., D_FF:]) @ w_out


def to_chunks(a):
    b, t = a.shape[:2]
    return jnp.moveaxis(a.reshape(b, t // Q_BLOCK, Q_BLOCK, *a.shape[2:]), 1, 0)


def from_chunks(a):
    a = jnp.moveaxis(a, 0, 1)
    return a.reshape(a.shape[0], -1, *a.shape[3:])


def blocked_causal_attention(q, k, v, scale, cum_log_f=None):
    t_len = q.shape[1]
    kpos = jnp.arange(t_len)
    xs = [to_chunks(q), jnp.arange(t_len).reshape(-1, Q_BLOCK)]
    ck = None
    if cum_log_f is not None:
        xs.append(to_chunks(cum_log_f))
        ck = jnp.moveaxis(cum_log_f, 1, 2)[:, :, None, :]

    def block(args):
        qc, qpos = args[0], args[1]
        s = jnp.einsum('bqhd,bkhd->bhqk', qc, k).astype(jnp.float32) * scale
        if ck is not None:
            s = s + (jnp.moveaxis(args[2], 1, 2)[..., None] - ck)
        s = jnp.where(kpos[None, :] <= qpos[:, None], s, NEG)
        p = jax.nn.softmax(s, axis=-1)
        return jnp.einsum('bhqk,bkhd->bqhd', p.astype(v.dtype), v)

    return from_chunks(lax.map(block, tuple(xs)))


def cmp_to_slc_matrix(n_cmp, n_slc):
    c0 = np.arange(n_cmp) * CMP_STRIDE
    c1 = c0 + CMP_LEN
    s0 = np.arange(n_slc) * SLC_LEN
    s1 = s0 + SLC_LEN
    ov = np.clip(np.minimum(c1[:, None], s1[None, :]) - np.maximum(c0[:, None], s0[None, :]), 0, None)
    return (ov / CMP_LEN).astype(np.float32)


def nsa_attention(q, k_cmp, v_cmp, k_slc, v_slc, k_win, v_win, gates, pos, cmp_pe, cmp_w1, cmp_w2):
    b, t_len, h, d = q.shape
    g = k_cmp.shape[2]
    hg = h // g
    scale = d ** -0.5
    q_plain = q.reshape(b, t_len, g, hg, d)
    q_rot = rope(q, pos).reshape(b, t_len, g, hg, d)
    k_slc = rope(k_slc, pos)
    k_win = rope(k_win, pos)

    n_cmp = (t_len - CMP_LEN) // CMP_STRIDE + 1
    blk = np.arange(n_cmp)[:, None] * CMP_STRIDE + np.arange(CMP_LEN)[None, :]

    def compress(x, pe, w1, w2):
        xb = x[:, blk] + pe[None, None, :, None, :]
        xb = jnp.moveaxis(xb, 3, 2).reshape(b, n_cmp, g, CMP_LEN * d)
        return jax.nn.gelu(xb @ w1) @ w2

    kc = compress(k_cmp, cmp_pe[0], cmp_w1[0], cmp_w2[0])
    vc = compress(v_cmp, cmp_pe[1], cmp_w1[1], cmp_w2[1])
    cmp_end = jnp.asarray(np.arange(n_cmp) * CMP_STRIDE + CMP_LEN - 1)
    cmask = cmp_end[None, :] <= pos[:, None]
    s = jnp.einsum('btghd,bngd->bghtn', q_plain, kc).astype(jnp.float32) * scale
    p_cmp = jax.nn.softmax(jnp.where(cmask, s, NEG), axis=-1) * cmask
    o_cmp = jnp.einsum('bghtn,bngd->btghd', p_cmp.astype(vc.dtype), vc)

    n_slc = t_len // SLC_LEN
    top = min(SLC_TOP, n_slc)
    imp = jnp.einsum('bghtn,nj->btgj', p_cmp, jnp.asarray(cmp_to_slc_matrix(n_cmp, n_slc)))
    j = jnp.arange(n_slc)[None, None, :]
    cur = (pos // SLC_LEN)[:, None, None]
    forced = (j == 0) | (j == cur) | (j == cur - 1)
    imp = jnp.where(j > cur, -jnp.inf, jnp.where(forced, jnp.inf, imp))
    _, sel = lax.top_k(imp, top)
    kblk = jnp.moveaxis(k_slc.reshape(b, n_slc, SLC_LEN, g, d), 3, 1)
    vblk = jnp.moveaxis(v_slc.reshape(b, n_slc, SLC_LEN, g, d), 3, 1)
    bi = jnp.arange(b)[:, None, None, None]
    gi = jnp.arange(g)[None, None, :, None]
    offs = jnp.arange(SLC_LEN)

    def slc_block(args):
        qc, sc, qpos = args
        kg = kblk[bi, gi, sc]
        vg = vblk[bi, gi, sc]
        kpos = sc[..., None] * SLC_LEN + offs
        m = (kpos <= qpos[None, :, None, None, None])[:, :, :, None]
        s = jnp.einsum('bqghd,bqgkld->bqghkl', qc, kg).astype(jnp.float32) * scale
        s = jnp.where(m, s, NEG)
        bq = qc.shape[1]
        p = jax.nn.softmax(s.reshape(b, bq, g, hg, top * SLC_LEN), axis=-1)
        p = p.reshape(b, bq, g, hg, top, SLC_LEN)
        return jnp.einsum('bqghkl,bqgkld->bqghd', p.astype(vg.dtype), vg)

    o_slc = from_chunks(lax.map(slc_block, (to_chunks(q_rot), to_chunks(sel), pos.reshape(-1, Q_BLOCK))))

    nc = t_len // Q_BLOCK
    band = np.arange(nc)[:, None] * Q_BLOCK + np.arange(WIN + Q_BLOCK)[None, :]
    pad = ((0, 0), (WIN, 0), (0, 0), (0, 0))
    kb = jnp.pad(k_win, pad)[:, band]
    vb = jnp.pad(v_win, pad)[:, band]
    qq = np.arange(Q_BLOCK)[:, None]
    kk = np.arange(WIN + Q_BLOCK)[None, :]
    wmask = ((kk > qq) & (kk <= qq + WIN))[None] & ((np.arange(nc)[:, None, None] * Q_BLOCK + kk[None]) >= WIN)
    qw = q_rot.reshape(b, nc, Q_BLOCK, g, hg, d)
    s = jnp.einsum('bcqghd,bckgd->bcghqk', qw, kb).astype(jnp.float32) * scale
    s = jnp.where(jnp.asarray(wmask)[None, :, None, None], s, NEG)
    p = jax.nn.softmax(s, axis=-1)
    o_win = jnp.einsum('bcghqk,bckgd->bcqghd', p.astype(vb.dtype), vb).reshape(b, t_len, g, hg, d)

    gt = gates.reshape(b, t_len, g, hg, 3)
    return o_cmp * gt[..., 0:1] + o_slc * gt[..., 1:2] + o_win * gt[..., 2:3]


def hybrid_mixer(h, w_in, w_out, cmp_pe, cmp_w1, cmp_w2, f_bias):
    b, t_len, _ = h.shape
    pos = jnp.arange(t_len)
    (q_n, k_c, v_c, k_s, v_s, k_w, v_w, g_logit,
     q_f, k_f, v_f, f_logit) = split_cols(h @ w_in, HYB_SPLITS)
    heads = lambda a, n: a.reshape(b, t_len, n, HEAD_DIM)
    gates = jax.nn.sigmoid(g_logit.astype(jnp.float32)).reshape(b, t_len, NSA_HEADS, 3).astype(h.dtype)
    o_nsa = nsa_attention(heads(q_n, NSA_HEADS),
                          heads(k_c, NSA_KV_GROUPS), heads(v_c, NSA_KV_GROUPS),
                          heads(k_s, NSA_KV_GROUPS), heads(v_s, NSA_KV_GROUPS),
                          heads(k_w, NSA_KV_GROUPS), heads(v_w, NSA_KV_GROUPS),
                          gates, pos, cmp_pe, cmp_w1, cmp_w2)
    log_f = jax.nn.log_sigmoid(f_logit.astype(jnp.float32) + f_bias.astype(jnp.float32))
    cum = jnp.cumsum(log_f, axis=1)
    o_fox = blocked_causal_attention(heads(q_f, FOX_HEADS), heads(k_f, FOX_HEADS),
                                     heads(v_f, FOX_HEADS), HEAD_DIM ** -0.5, cum)
    o = jnp.concatenate([o_nsa.reshape(b, t_len, NSA_Q), o_fox.reshape(b, t_len, FOX_W)], axis=-1)
    return o @ w_out


def mla_mixer(h, w_in, q_norm, kv_norm, w_uq, w_ukv, w_out):
    b, t_len, _ = h.shape
    pos = jnp.arange(t_len)
    c_q, c_kv, k_r = split_cols(h @ w_in, MLA_SPLITS)
    q = (rms_norm(c_q, q_norm) @ w_uq).reshape(b, t_len, MLA_HEADS, MLA_NOPE + MLA_ROPE)
    q = jnp.concatenate([q[..., :MLA_NOPE], rope(q[..., MLA_NOPE:], pos)], axis=-1)
    kv = (rms_norm(c_kv, kv_norm) @ w_ukv).reshape(b, t_len, MLA_HEADS, MLA_NOPE + MLA_V)
    k_r = rope(k_r[:, :, None, :], pos)
    k = jnp.concatenate([kv[..., :MLA_NOPE], jnp.broadcast_to(k_r, (b, t_len, MLA_HEADS, MLA_ROPE))], axis=-1)
    o = blocked_causal_attention(q, k, kv[..., MLA_NOPE:], (MLA_NOPE + MLA_ROPE) ** -0.5)
    return o.reshape(b, t_len, MLA_HEADS * MLA_V) @ w_out


def setup_inputs(seed: int = 0) -> dict:
    key = jax.random.key(seed)
    ks = jax.random.split(key, 21)
    f32 = jnp.float32

    def w(k, shape, fan_in):
        return jax.random.normal(k, shape, f32) * (fan_in ** -0.5)

    def gain(k, shape):
        return 1.0 + 0.02 * jax.random.normal(k, shape, f32)

    return {
        'x': jax.random.normal(ks[0], (BATCH, SEQ, D_MODEL), f32),
        'ffn1_norm': gain(ks[1], (DEPTH, D_MODEL)),
        'ffn1_w_in': w(ks[2], (DEPTH, D_MODEL, 2 * D_FF), D_MODEL),
        'ffn1_w_out': w(ks[3], (DEPTH, D_FF, D_MODEL), D_FF),
        'mix_norm': gain(ks[4], (DEPTH, D_MODEL)),
        'ffn2_norm': gain(ks[5], (DEPTH, D_MODEL)),
        'ffn2_w_in': w(ks[6], (DEPTH, D_MODEL, 2 * D_FF), D_MODEL),
        'ffn2_w_out': w(ks[7], (DEPTH, D_FF, D_MODEL), D_FF),
        'hyb_w_in': w(ks[8], (N_EVEN, D_MODEL, HYB_IN), D_MODEL),
        'hyb_w_out': w(ks[9], (N_EVEN, HYB_OUT, D_MODEL), HYB_OUT),
        'nsa_cmp_pe': 0.02 * jax.random.normal(ks[10], (N_EVEN, 2, CMP_LEN, HEAD_DIM), f32),
        'nsa_cmp_w1': w(ks[11], (N_EVEN, 2, CMP_LEN * HEAD_DIM, CMP_HIDDEN), CMP_LEN * HEAD_DIM),
        'nsa_cmp_w2': w(ks[12], (N_EVEN, 2, CMP_HIDDEN, HEAD_DIM), CMP_HIDDEN),
        'fox_f_bias': jax.random.uniform(ks[13], (N_EVEN, FOX_HEADS), f32, 1.0, 4.0),
        'mla_w_in': w(ks[14], (N_ODD, D_MODEL, MLA_IN), D_MODEL),
        'mla_q_norm': gain(ks[15], (N_ODD, MLA_Q_RANK)),
        'mla_kv_norm': gain(ks[16], (N_ODD, MLA_KV_RANK)),
        'mla_w_uq': w(ks[17], (N_ODD, MLA_Q_RANK, MLA_HEADS * (MLA_NOPE + MLA_ROPE)), MLA_Q_RANK),
        'mla_w_ukv': w(ks[18], (N_ODD, MLA_KV_RANK, MLA_HEADS * (MLA_NOPE + MLA_V)), MLA_KV_RANK),
        'mla_w_out': w(ks[19], (N_ODD, MLA_HEADS * MLA_V, D_MODEL), MLA_HEADS * MLA_V),
        'final_norm': gain(ks[20], (D_MODEL,)),
    }


def reference(x, ffn1_norm, ffn1_w_in, ffn1_w_out, mix_norm, ffn2_norm, ffn2_w_in, ffn2_w_out,
              hyb_w_in, hyb_w_out, nsa_cmp_pe, nsa_cmp_w1, nsa_cmp_w2, fox_f_bias,
              mla_w_in, mla_q_norm, mla_kv_norm, mla_w_uq, mla_w_ukv, mla_w_out, final_norm):
    for i in range(DEPTH):
        x = x + 0.5 * swiglu(rms_norm(x, ffn1_norm[i]), ffn1_w_in[i], ffn1_w_out[i])
        h = rms_norm(x, mix_norm[i])
        if i % 2 == 0:
            e = i // 2
            x = x + hybrid_mixer(h, hyb_w_in[e], hyb_w_out[e], nsa_cmp_pe[e], nsa_cmp_w1[e],
                                 nsa_cmp_w2[e], fox_f_bias[e])
        else:
            o = i // 2
            x = x + mla_mixer(h, mla_w_in[o], mla_q_norm[o], mla_kv_norm[o], mla_w_uq[o],
                              mla_w_ukv[o], mla_w_out[o])
        x = x + 0.5 * swiglu(rms_norm(x, ffn2_norm[i]), ffn2_w_in[i], ffn2_w_out[i])
    return rms_norm(x, final_norm)
```

```python
import functools

import numpy as np
import jax
import jax.numpy as jnp
from jax import lax
from jax.experimental import pallas as pl
from jax.experimental.pallas import tpu as pltpu

D_MODEL = 2048
HEAD_DIM = 128
ROPE_THETA = 10000.0
NORM_EPS = 1e-6
NEG = -1e30

NSA_HEADS = 8
NSA_GROUPS = 2
HEADS_PER_GROUP = NSA_HEADS // NSA_GROUPS
CMP_LEN = 32
CMP_STRIDE = 16
CMP_HIDDEN = 256
SLC_LEN = 64
SLC_TOP = 8
WIN = 512
FOX_HEADS = 8
MLA_HEADS = 16
MLA_Q_RANK = 512
MLA_KV_RANK = 512
MLA_NOPE = 128
MLA_ROPE = 64
MLA_V = 128
D_FF = 5632

LANES = 128
VMEM_LIMIT = 56 * 1024 * 1024

BF = jnp.bfloat16
F32 = jnp.float32


def _params(sem, vmem=VMEM_LIMIT):
    return pltpu.CompilerParams(dimension_semantics=sem, vmem_limit_bytes=vmem)


def _rope_lanes(x, cos, sin):
    return x * cos + pltpu.roll(x, LANES // 2, axis=1) * sin


def _dot_t(a, b):
    return lax.dot_general(a, b, (((1,), (1,)), ((), ())), preferred_element_type=F32)


def _softmax_rows(s):
    m = jnp.max(s, axis=-1, keepdims=True)
    e = jnp.exp(s - m)
    return e, jnp.sum(e, axis=-1, keepdims=True)


def _ffn_kernel(*refs, n_f, emit_x, norm_dtype):
    x_ref, g_ref, wa_ref, wb_ref, w2_ref, gn_ref = refs[:6]
    outs = refs[6:-2]
    xn_sc, acc_sc = refs[-2:]
    j = pl.program_id(1)

    @pl.when(j == 0)
    def _():
        x = x_ref[...]
        ms = jnp.mean(x * x, axis=-1, keepdims=True)
        xn_sc[...] = (x * lax.rsqrt(ms + NORM_EPS) * g_ref[...]).astype(BF)
        acc_sc[...] = jnp.zeros_like(acc_sc)

    xn = xn_sc[...]
    h1 = jnp.dot(xn, wa_ref[...], preferred_element_type=F32)
    h2 = jnp.dot(xn, wb_ref[...], preferred_element_type=F32)
    a = (h1 * jax.nn.sigmoid(h1) * h2).astype(BF)
    acc_sc[...] += jnp.dot(a, w2_ref[...], preferred_element_type=F32)

    @pl.when(j == n_f - 1)
    def _():
        y = x_ref[...] + 0.5 * acc_sc[...]
        k = 0
        if emit_x:
            outs[k][...] = y
            k += 1
        if norm_dtype is not None:
            ms = jnp.mean(y * y, axis=-1, keepdims=True)
            outs[k][...] = (y * lax.rsqrt(ms + NORM_EPS) * gn_ref[...]).astype(norm_dtype)


def _ffn(x, g, w_in, w_out, g_next, *, emit_x, norm_dtype, tm=512, tf=512):
    n, d = x.shape
    n_f = D_FF // tf
    out_shape, out_specs = [], []
    if emit_x:
        out_shape.append(jax.ShapeDtypeStruct((n, d), F32))
        out_specs.append(pl.BlockSpec((tm, d), lambda i, j: (i, 0)))
    if norm_dtype is not None:
        out_shape.append(jax.ShapeDtypeStruct((n, d), norm_dtype))
        out_specs.append(pl.BlockSpec((tm, d), lambda i, j: (i, 0)))
    return pl.pallas_call(
        functools.partial(_ffn_kernel, n_f=n_f, emit_x=emit_x, norm_dtype=norm_dtype),
        grid=(n // tm, n_f),
        in_specs=[
            pl.BlockSpec((tm, d), lambda i, j: (i, 0)),
            pl.BlockSpec((1, d), lambda i, j: (0, 0)),
            pl.BlockSpec((d, tf), lambda i, j: (0, j)),
            pl.BlockSpec((d, tf), lambda i, j: (0, n_f + j)),
            pl.BlockSpec((tf, d), lambda i, j: (j, 0)),
            pl.BlockSpec((1, d), lambda i, j: (0, 0)),
        ],
        out_specs=out_specs,
        out_shape=out_shape,
        scratch_shapes=[pltpu.VMEM((tm, d), BF), pltpu.VMEM((tm, d), F32)],
        compiler_params=_params(("parallel", "arbitrary")),
        name="ffn",
    )(x, g.reshape(1, d), w_in, w_in, w_out, g_next.reshape(1, d))


def _mm_kernel(*refs, n_a, has_res, rope_j, rope_chunks):
    a_refs = refs[:n_a]
    w_refs = refs[n_a:2 * n_a]
    k = 2 * n_a
    res_ref = None
    if has_res:
        res_ref = refs[k]
        k += 1
    if rope_chunks is not None:
        cos_ref, sin_ref = refs[k], refs[k + 1]
        k += 2
    o_ref = refs[k]

    acc = jnp.dot(a_refs[0][...], w_refs[0][...], preferred_element_type=F32)
    for a_ref, w_ref in zip(a_refs[1:], w_refs[1:]):
        acc += jnp.dot(a_ref[...], w_ref[...], preferred_element_type=F32)
    if has_res:
        acc += res_ref[...]

    def store(v):
        o_ref[...] = v.reshape(o_ref.shape).astype(o_ref.dtype)

    if rope_chunks is None:
        store(acc)
    else:
        j = pl.program_id(1)
        is_rope = (j >= rope_j[0]) & (j < rope_j[1])

        @pl.when(is_rope)
        def _():
            cos, sin = cos_ref[...], sin_ref[...]
            parts = []
            for c, on in enumerate(rope_chunks):
                blk = acc[:, c * LANES:(c + 1) * LANES]
                parts.append(_rope_lanes(blk, cos, sin) if on else blk)
            store(jnp.concatenate(parts, axis=1))

        @pl.when(jnp.logical_not(is_rope))
        def _():
            store(acc)


def _mm(a_list, w_list, *, out_dtype, tm, tn, res=None, rope=None, seq=None, stacked_out=False, name="mm"):
    n = a_list[0].shape[0]
    n_out = w_list[0].shape[1]
    n_a = len(a_list)
    in_specs, args = [], []
    for a in a_list:
        in_specs.append(pl.BlockSpec((tm, a.shape[1]), lambda i, j: (i, 0)))
        args.append(a)
    for w in w_list:
        in_specs.append(pl.BlockSpec((w.shape[0], tn), lambda i, j: (0, j)))
        args.append(w)
    if res is not None:
        in_specs.append(pl.BlockSpec((tm, tn), lambda i, j: (i, j)))
        args.append(res)
    rope_j = rope_chunks = None
    if rope is not None:
        cos, sin, rope_j, rope_chunks = rope
        n_seq = seq // tm
        for t in (cos, sin):
            in_specs.append(pl.BlockSpec((tm, LANES), lambda i, j: (i % n_seq, 0)))
            args.append(t)
    if stacked_out:
        out_shape = jax.ShapeDtypeStruct((n_out // tn, n, tn), out_dtype)
        out_spec = pl.BlockSpec((1, tm, tn), lambda i, j: (j, i, 0))
    else:
        out_shape = jax.ShapeDtypeStruct((n, n_out), out_dtype)
        out_spec = pl.BlockSpec((tm, tn), lambda i, j: (i, j))
    return pl.pallas_call(
        functools.partial(_mm_kernel, n_a=n_a, has_res=res is not None, rope_j=rope_j, rope_chunks=rope_chunks),
        grid=(n // tm, n_out // tn),
        in_specs=in_specs,
        out_specs=out_spec,
        out_shape=out_shape,
        compiler_params=_params(("parallel", "arbitrary")),
        name=name,
    )(*args)


def _compress_kernel(x_ref, pe_ref, w1_ref, w2_ref, o_ref):
    x = x_ref[0, 0]
    half = CMP_STRIDE * HEAD_DIM
    xa = (x + pe_ref[0, 0:1, :]).astype(BF)
    xb = (x + pe_ref[0, 1:2, :]).astype(BF)
    ha = jnp.dot(xa, w1_ref[0, :half, :], preferred_element_type=F32)
    hb = jnp.dot(xb, w1_ref[0, half:, :], preferred_element_type=F32)
    hid = ha + pltpu.roll(hb, hb.shape[0] - 1, axis=0)
    act = jax.nn.gelu(hid, approximate=True).astype(BF)
    o_ref[0, 0] = jnp.dot(act, w2_ref[0], preferred_element_type=F32)


def _compress(xc, pe, w1, w2, batch):
    rows = xc.shape[2]
    return pl.pallas_call(
        _compress_kernel,
        grid=(2 * NSA_GROUPS, batch),
        in_specs=[
            pl.BlockSpec((1, 1, rows, CMP_STRIDE * HEAD_DIM), lambda c, b: (c, b, 0, 0)),
            pl.BlockSpec((1, 2, CMP_STRIDE * HEAD_DIM), lambda c, b: (c // NSA_GROUPS, 0, 0)),
            pl.BlockSpec((1, CMP_LEN * HEAD_DIM, CMP_HIDDEN), lambda c, b: (c // NSA_GROUPS, 0, 0)),
            pl.BlockSpec((1, CMP_HIDDEN, HEAD_DIM), lambda c, b: (c // NSA_GROUPS, 0, 0)),
        ],
        out_specs=pl.BlockSpec((1, 1, rows, HEAD_DIM), lambda c, b: (c, b, 0, 0)),
        out_shape=jax.ShapeDtypeStruct((2 * NSA_GROUPS, batch, rows, HEAD_DIM), F32),
        compiler_params=_params(("arbitrary", "arbitrary")),
        name="nsa_compress",
    )(xc, pe, w1, w2)


def _nsa_kernel(q_ref, ks_ref, kw_ref, vs_ref, vw_ref, kvc_ref, gate_ref, cos_ref, sin_ref, m_ref, o_ref,
                *, tq, seq):
    qi = pl.program_id(1)
    hg = HEADS_PER_GROUP
    scale = HEAD_DIM ** -0.5
    n_cmp = (seq - CMP_LEN) // CMP_STRIDE + 1
    n_slc = seq // SLC_LEN
    band = WIN + tq

    t_row = qi * tq + lax.broadcasted_iota(jnp.int32, (tq, 1), 0)
    t_rows = jnp.concatenate([t_row] * hg, axis=0)
    cos = jnp.concatenate([cos_ref[...]] * hg, axis=0)
    sin = jnp.concatenate([sin_ref[...]] * hg, axis=0)
    gates = jax.nn.sigmoid(gate_ref[0])

    n_lane = lax.broadcasted_iota(jnp.int32, (1, LANES), 1)
    cmask = (n_lane * CMP_STRIDE + (CMP_LEN - 1) <= t_rows) & (n_lane < n_cmp)
    j_lane = lax.broadcasted_iota(jnp.int32, (1, n_slc), 1)
    slc_shift = SLC_LEN.bit_length() - 1
    cur = jnp.right_shift(t_row, slc_shift)
    forced = (j_lane == 0) | (j_lane == cur) | (j_lane == cur - 1)
    expand = jnp.where(jnp.right_shift(lax.broadcasted_iota(jnp.int32, (n_slc, seq), 1), slc_shift)
                       == lax.broadcasted_iota(jnp.int32, (n_slc, seq), 0), 1.0, 0.0).astype(BF)
    s_pos = lax.broadcasted_iota(jnp.int32, (1, seq), 1)
    causal = s_pos <= t_rows
    w_start = pl.multiple_of(jnp.maximum(qi * tq - WIN, 0), tq)
    w_pos = w_start + lax.broadcasted_iota(jnp.int32, (1, band), 1)
    wmask = (w_pos <= t_rows) & (w_pos > t_rows - WIN)

    for g in range(NSA_GROUPS):
        qg = q_ref[0, :, g * hg * HEAD_DIM:(g + 1) * hg * HEAD_DIM]
        qp = jnp.concatenate([qg[:, h * HEAD_DIM:(h + 1) * HEAD_DIM] for h in range(hg)], axis=0)
        qr = _rope_lanes(qp, cos, sin).astype(BF)
        qp = qp.astype(BF)

        kc = kvc_ref[g, 0].astype(BF)
        vc = kvc_ref[NSA_GROUPS + g, 0].astype(BF)
        s = jnp.where(cmask, _dot_t(qp, kc) * scale, NEG)
        e, l = _softmax_rows(s)
        p = jnp.where(cmask, e / l, 0.0)
        o_cmp = jnp.dot(p.astype(BF), vc, preferred_element_type=F32)

        p_sum = p[0:tq]
        for h in range(1, hg):
            p_sum = p_sum + p[h * tq:(h + 1) * tq]
        imp = jnp.dot(p_sum, m_ref[...], preferred_element_type=F32, precision=lax.Precision.HIGHEST)
        imp = jnp.where(j_lane > cur, -jnp.inf, jnp.where(forced, jnp.inf, imp))
        rank = jnp.zeros((tq, n_slc), jnp.int32)
        for i in range(n_slc):
            ci = imp[:, i:i + 1]
            beats = (ci > imp) | ((ci == imp) & (j_lane > i))
            rank = rank + beats.astype(jnp.int32)
        sel = jnp.where(rank < SLC_TOP, 1.0, 0.0).astype(BF)
        sel_keys = jnp.dot(sel, expand, preferred_element_type=F32)
        sel_keys = jnp.concatenate([sel_keys] * hg, axis=0) > 0.5

        ks = ks_ref[0, :, g * HEAD_DIM:(g + 1) * HEAD_DIM]
        vs = vs_ref[0, :, g * HEAD_DIM:(g + 1) * HEAD_DIM]
        s = jnp.where(sel_keys & causal, _dot_t(qr, ks) * scale, NEG)
        e, l = _softmax_rows(s)
        o_slc = jnp.dot(e.astype(BF), vs, preferred_element_type=F32) / l

        kw = kw_ref[0, pl.ds(w_start, band), g * HEAD_DIM:(g + 1) * HEAD_DIM]
        vw = vw_ref[0, pl.ds(w_start, band), g * HEAD_DIM:(g + 1) * HEAD_DIM]
        s = jnp.where(wmask, _dot_t(qr, kw) * scale, NEG)
        e, l = _softmax_rows(s)
        o_win = jnp.dot(e.astype(BF), vw, preferred_element_type=F32) / l

        for hh in range(hg):
            h = g * hg + hh
            rows = slice(hh * tq, (hh + 1) * tq)
            o = (o_cmp[rows] * gates[:, 3 * h:3 * h + 1]
                 + o_slc[rows] * gates[:, 3 * h + 1:3 * h + 2]
                 + o_win[rows] * gates[:, 3 * h + 2:3 * h + 3])
            o_ref[0, :, h * HEAD_DIM:(h + 1) * HEAD_DIM] = o.astype(o_ref.dtype)


def _nsa(q, main, kvc, planes, cos, sin, m_cs, *, batch, seq, tq=128):
    nq = seq // tq
    kv_w = NSA_GROUPS * HEAD_DIM
    return pl.pallas_call(
        functools.partial(_nsa_kernel, tq=tq, seq=seq),
        grid=(batch, nq),
        in_specs=[
            pl.BlockSpec((1, tq, NSA_HEADS * HEAD_DIM), lambda b, i: (b, i, 0)),
            pl.BlockSpec((1, seq, kv_w), lambda b, i: (b, 0, 0)),
            pl.BlockSpec((1, seq, kv_w), lambda b, i: (b, 0, 1)),
            pl.BlockSpec((1, seq, kv_w), lambda b, i: (b, 0, 2)),
            pl.BlockSpec((1, seq, kv_w), lambda b, i: (b, 0, 3)),
            pl.BlockSpec((2 * NSA_GROUPS, 1, LANES, HEAD_DIM), lambda b, i: (0, b, 0, 0)),
            pl.BlockSpec((1, tq, LANES), lambda b, i: (2 * NSA_GROUPS, b * nq + i, 0)),
            pl.BlockSpec((tq, LANES), lambda b, i: (i, 0)),
            pl.BlockSpec((tq, LANES), lambda b, i: (i, 0)),
            pl.BlockSpec(m_cs.shape, lambda b, i: (0, 0)),
        ],
        out_specs=pl.BlockSpec((1, tq, NSA_HEADS * HEAD_DIM), lambda b, i: (b, i, 0)),
        out_shape=jax.ShapeDtypeStruct((batch, seq, NSA_HEADS * HEAD_DIM), BF),
        compiler_params=_params(("parallel", "arbitrary")),
        name="nsa_attention",
    )(q, main, main, main, main, kvc, planes, cos, sin, m_cs)


def _cumdecay_kernel(x_ref, b_ref, o_ref):
    z = x_ref[0] + b_ref[...]
    c = jnp.minimum(z, 0.0) - jnp.log1p(jnp.exp(-jnp.abs(z)))
    rows = c.shape[0]
    r = lax.broadcasted_iota(jnp.int32, (rows, 1), 0)
    k = 1
    while k < rows:
        c = c + jnp.where(r >= k, pltpu.roll(c, k, axis=0), 0.0)
        k *= 2
    o_ref[0] = c


def _cumdecay(planes, bias, *, batch, seq):
    return pl.pallas_call(
        _cumdecay_kernel,
        grid=(batch,),
        in_specs=[pl.BlockSpec((1, seq, LANES), lambda b: (2 * NSA_GROUPS, b, 0)),
                  pl.BlockSpec((1, LANES), lambda b: (0, 0))],
        out_specs=pl.BlockSpec((1, seq, LANES), lambda b: (b, 0, 0)),
        out_shape=jax.ShapeDtypeStruct((batch, seq, LANES), F32),
        compiler_params=_params(("arbitrary",)),
        name="fox_cumdecay",
    )(planes, bias)


def _attn_kernel(*refs, tq, seq, scale, n_qk, decay):
    q_ref = refs[0]
    k_refs = refs[1:1 + n_qk]
    v_ref = refs[1 + n_qk]
    k = 2 + n_qk
    if decay:
        cq_ref, ck_ref = refs[k], refs[k + 1]
        k += 2
    o_ref = refs[k]
    qi = pl.program_id(2)

    q = q_ref[0]
    s = _dot_t(q[:, :HEAD_DIM], k_refs[0][0])
    for c in range(1, n_qk):
        s = s + _dot_t(q[:, c * HEAD_DIM:(c + 1) * HEAD_DIM], k_refs[c][0])
    s = s * scale
    if decay:
        s = s + (cq_ref[0] - ck_ref[0])
    t_pos = qi * tq + lax.broadcasted_iota(jnp.int32, (tq, 1), 0)
    s_pos = lax.broadcasted_iota(jnp.int32, (1, seq), 1)
    s = jnp.where(s_pos <= t_pos, s, NEG)
    e, l = _softmax_rows(s)
    o_ref[0] = (jnp.dot(e.astype(BF), v_ref[0], preferred_element_type=F32) / l).astype(o_ref.dtype)


def _attention(q_arr, q_blk0, k_arrs, k_blk0s, k_per_head, v_arr, v_blk0, *, batch, seq, heads, scale,
               decay=None, tq=256, name="attention"):
    n_qk = len(k_arrs)
    nq = seq // tq
    in_specs = [pl.BlockSpec((1, tq, n_qk * HEAD_DIM), lambda b, h, i: (b, i, q_blk0 + h))]
    args = [q_arr]
    for arr, blk0, per_head in zip(k_arrs, k_blk0s, k_per_head):
        if per_head:
            in_specs.append(pl.BlockSpec((1, seq, HEAD_DIM), lambda b, h, i, blk0=blk0: (b, 0, blk0 + h)))
        else:
            in_specs.append(pl.BlockSpec((1, seq, HEAD_DIM), lambda b, h, i, blk0=blk0: (b, 0, blk0)))
        args.append(arr)
    in_specs.append(pl.BlockSpec((1, seq, HEAD_DIM), lambda b, h, i: (b, 0, v_blk0 + h)))
    args.append(v_arr)
    if decay is not None:
        cq, ck = decay
        in_specs.append(pl.BlockSpec((1, tq, 1), lambda b, h, i: (b * heads + h, i, 0)))
        in_specs.append(pl.BlockSpec((1, 1, seq), lambda b, h, i: (b * heads + h, 0, 0)))
        args += [cq, ck]
    return pl.pallas_call(
        functools.partial(_attn_kernel, tq=tq, seq=seq, scale=scale, n_qk=n_qk, decay=decay is not None),
        grid=(batch, heads, nq),
        in_specs=in_specs,
        out_specs=pl.BlockSpec((1, tq, HEAD_DIM), lambda b, h, i: (b, i, h)),
        out_shape=jax.ShapeDtypeStruct((batch, seq, heads * HEAD_DIM), BF),
        compiler_params=_params(("parallel", "parallel", "arbitrary")),
        name=name,
    )(*args)


def _mla_latent_kernel(h_ref, w_ref, gq_ref, gkv_ref, cos_ref, sin_ref, cq_ref, ckv_ref, kr_ref):
    c = jnp.dot(h_ref[...], w_ref[...], preferred_element_type=F32)

    def norm(v, g):
        ms = jnp.mean(v * v, axis=-1, keepdims=True)
        return v * lax.rsqrt(ms + NORM_EPS) * g

    cq_ref[...] = norm(c[:, :MLA_Q_RANK], gq_ref[...]).astype(BF)
    ckv_ref[...] = norm(c[:, MLA_Q_RANK:MLA_Q_RANK + MLA_KV_RANK], gkv_ref[...]).astype(BF)
    kr_ref[...] = _rope_lanes(c[:, MLA_Q_RANK + MLA_KV_RANK:], cos_ref[...], sin_ref[...]).astype(BF)


def _mla_latent(h, w, gq, gkv, cos, sin, *, seq, tm=512):
    n = h.shape[0]
    n_seq = seq // tm
    return pl.pallas_call(
        _mla_latent_kernel,
        grid=(n // tm,),
        in_specs=[
            pl.BlockSpec((tm, h.shape[1]), lambda i: (i, 0)),
            pl.BlockSpec(w.shape, lambda i: (0, 0)),
            pl.BlockSpec((1, MLA_Q_RANK), lambda i: (0, 0)),
            pl.BlockSpec((1, MLA_KV_RANK), lambda i: (0, 0)),
            pl.BlockSpec((tm, LANES), lambda i: (i % n_seq, 0)),
            pl.BlockSpec((tm, LANES), lambda i: (i % n_seq, 0)),
        ],
        out_specs=[
            pl.BlockSpec((tm, MLA_Q_RANK), lambda i: (i, 0)),
            pl.BlockSpec((tm, MLA_KV_RANK), lambda i: (i, 0)),
            pl.BlockSpec((tm, LANES), lambda i: (i, 0)),
        ],
        out_shape=[
            jax.ShapeDtypeStruct((n, MLA_Q_RANK), BF),
            jax.ShapeDtypeStruct((n, MLA_KV_RANK), BF),
            jax.ShapeDtypeStruct((n, LANES), BF),
        ],
        compiler_params=_params(("parallel",)),
        name="mla_latent",
    )(h, w, gq.reshape(1, -1), gkv.reshape(1, -1), cos, sin)


def _rope_tables(seq, dim):
    half = dim // 2
    inv = 1.0 / (ROPE_THETA ** (np.arange(0, dim, 2, dtype=np.float32) / dim))
    ang = np.arange(seq, dtype=np.float32)[:, None] * inv.astype(np.float32)[None, :]
    cos = np.zeros((seq, LANES), np.float32)
    sin = np.zeros((seq, LANES), np.float32)
    cos[:, :half] = np.cos(ang)
    cos[:, LANES // 2:LANES // 2 + half] = np.cos(ang)
    sin[:, :half] = -np.sin(ang)
    sin[:, LANES // 2:LANES // 2 + half] = np.sin(ang)
    return jnp.asarray(cos), jnp.asarray(sin)


def _cmp_to_slc(seq):
    n_cmp = (seq - CMP_LEN) // CMP_STRIDE + 1
    n_slc = seq // SLC_LEN
    c0 = np.arange(n_cmp) * CMP_STRIDE
    c1 = c0 + CMP_LEN
    s0 = np.arange(n_slc) * SLC_LEN
    s1 = s0 + SLC_LEN
    ov = np.clip(np.minimum(c1[:, None], s1[None, :]) - np.maximum(c0[:, None], s0[None, :]), 0, None)
    m = np.zeros((LANES, n_slc), np.float32)
    m[:n_cmp] = ov / CMP_LEN
    return jnp.asarray(m)


def _spread_rope_cols(w):
    half = MLA_ROPE // 2
    z = jnp.zeros(w.shape[:-1] + (LANES // 2 - half,), w.dtype)
    return jnp.concatenate([w[..., :half], z, w[..., half:], z], axis=-1)


def _hybrid_mixer(x, h, w_in, w_out, cmp_pe, cmp_w1, cmp_w2, f_bias, *, batch, seq):
    n = batch * seq
    nq_w = NSA_HEADS * HEAD_DIM
    kv_w = NSA_GROUPS * HEAD_DIM
    fx_w = FOX_HEADS * HEAD_DIM
    o = 0
    cols = {}
    for name, width in (("q_n", nq_w), ("k_c", kv_w), ("v_c", kv_w), ("k_s", kv_w), ("v_s", kv_w),
                        ("k_w", kv_w), ("v_w", kv_w), ("g", 3 * NSA_HEADS), ("q_f", fx_w), ("k_f", fx_w),
                        ("v_f", fx_w), ("f", FOX_HEADS)):
        cols[name] = w_in[:, o:o + width]
        o += width
    n_misc = 3 * NSA_HEADS + FOX_HEADS
    w_qn = cols["q_n"].astype(BF)
    w_main = jnp.concatenate([cols[c] for c in ("k_s", "k_w", "v_s", "v_w", "q_f", "k_f", "v_f")], axis=1).astype(BF)
    w_planes = jnp.concatenate([cols["k_c"], cols["v_c"], cols["g"], cols["f"],
                                jnp.zeros((D_MODEL, LANES - n_misc), F32)], axis=1).astype(BF)

    cos, sin = _rope_tables(seq, HEAD_DIM)
    q_n = _mm([h], [w_qn], out_dtype=F32, tm=1024, tn=512, name="proj_q_nsa")
    main = _mm([h], [w_main], out_dtype=BF, tm=1024, tn=512, rope=(cos, sin, (0, 1), (True,) * 4), seq=seq,
               name="proj_main")
    planes = _mm([h], [w_planes], out_dtype=F32, tm=1024, tn=LANES, stacked_out=True, name="proj_planes")

    rows16 = seq // CMP_STRIDE
    kvc = _compress(planes.reshape(2 * NSA_GROUPS + 1, batch, rows16, CMP_STRIDE * HEAD_DIM),
                    cmp_pe.reshape(2, 2, CMP_STRIDE * HEAD_DIM), cmp_w1.astype(BF), cmp_w2.astype(BF), batch)

    o_nsa = _nsa(q_n.reshape(batch, seq, nq_w), main.reshape(batch, seq, -1), kvc, planes, cos, sin,
                 _cmp_to_slc(seq), batch=batch, seq=seq)

    bias = jnp.zeros((1, LANES), F32).at[0, 3 * NSA_HEADS:n_misc].set(f_bias)
    cum = _cumdecay(planes, bias, batch=batch, seq=seq)[:, :, 3 * NSA_HEADS:n_misc]
    cum_t = jnp.swapaxes(cum, 1, 2).reshape(batch * FOX_HEADS, seq)
    main3 = main.reshape(batch, seq, -1)
    blk = lambda col: col // HEAD_DIM
    o_fox = _attention(main3, blk(4 * kv_w), [main3], [blk(4 * kv_w + fx_w)], [True], main3, blk(4 * kv_w + 2 * fx_w),
                       batch=batch, seq=seq, heads=FOX_HEADS, scale=HEAD_DIM ** -0.5,
                       decay=(cum_t[:, :, None], cum_t[:, None, :]), name="fox_attention")

    w_o = w_out.astype(BF)
    return _mm([o_nsa.reshape(n, nq_w), o_fox.reshape(n, fx_w)], [w_o[:nq_w], w_o[nq_w:]],
               out_dtype=F32, tm=1024, tn=512, res=x, name="proj_out_hybrid")


def _mla_mixer(x, h, w_in, q_norm, kv_norm, w_uq, w_ukv, w_out, *, batch, seq):
    n = batch * seq
    qk = MLA_NOPE + MLA_ROPE
    w_lat = jnp.concatenate([w_in[:, :MLA_Q_RANK + MLA_KV_RANK],
                             _spread_rope_cols(w_in[:, MLA_Q_RANK + MLA_KV_RANK:])], axis=1).astype(BF)
    wq = w_uq.reshape(MLA_Q_RANK, MLA_HEADS, qk)
    wq = jnp.concatenate([wq[..., :MLA_NOPE], _spread_rope_cols(wq[..., MLA_NOPE:])], axis=-1)
    wq = wq.reshape(MLA_Q_RANK, MLA_HEADS * 2 * HEAD_DIM).astype(BF)
    wkv = w_ukv.reshape(MLA_KV_RANK, MLA_HEADS, MLA_NOPE + MLA_V)
    wkv = jnp.concatenate([wkv[..., :MLA_NOPE].reshape(MLA_KV_RANK, -1),
                           wkv[..., MLA_NOPE:].reshape(MLA_KV_RANK, -1)], axis=1).astype(BF)

    cos, sin = _rope_tables(seq, MLA_ROPE)
    cq, ckv, kr = _mla_latent(h, w_lat, q_norm, kv_norm, cos, sin, seq=seq)
    q = _mm([cq], [wq], out_dtype=BF, tm=1024, tn=512, rope=(cos, sin, (0, wq.shape[1] // 512), (False, True) * 2),
            seq=seq, name="proj_mla_q")
    kv = _mm([ckv], [wkv], out_dtype=BF, tm=1024, tn=512, name="proj_mla_kv")

    q3 = q.reshape(batch, seq, -1)
    kv3 = kv.reshape(batch, seq, -1)
    kr3 = kr.reshape(batch, seq, LANES)
    o = _attention(q3, 0, [kv3, kr3], [0, 0], [True, False], kv3, MLA_HEADS, batch=batch, seq=seq,
                   heads=MLA_HEADS, scale=qk ** -0.5, name="mla_attention")
    return _mm([o.reshape(n, MLA_HEADS * MLA_V)], [w_out.astype(BF)], out_dtype=F32, tm=1024, tn=512, res=x,
               name="proj_out_mla")


def kernel(x, ffn1_norm, ffn1_w_in, ffn1_w_out, mix_norm, ffn2_norm, ffn2_w_in, ffn2_w_out, hyb_w_in, hyb_w_out, nsa_cmp_pe, nsa_cmp_w1, nsa_cmp_w2, fox_f_bias, mla_w_in, mla_q_norm, mla_kv_norm, mla_w_uq, mla_w_ukv, mla_w_out, final_norm):
    batch, seq, d = x.shape
    depth = ffn1_norm.shape[0]
    x = x.reshape(batch * seq, d)
    for i in range(depth):
        x, h = _ffn(x, ffn1_norm[i], ffn1_w_in[i].astype(BF), ffn1_w_out[i].astype(BF), mix_norm[i],
                    emit_x=True, norm_dtype=BF)
        if i % 2 == 0:
            e = i // 2
            x = _hybrid_mixer(x, h, hyb_w_in[e], hyb_w_out[e], nsa_cmp_pe[e], nsa_cmp_w1[e], nsa_cmp_w2[e],
                              fox_f_bias[e], batch=batch, seq=seq)
        else:
            o = i // 2
            x = _mla_mixer(x, h, mla_w_in[o], mla_q_norm[o], mla_kv_norm[o], mla_w_uq[o], mla_w_ukv[o],
                           mla_w_out[o], batch=batch, seq=seq)
        last = i == depth - 1
        (x,) = _ffn(x, ffn2_norm[i], ffn2_w_in[i].astype(BF), ffn2_w_out[i].astype(BF),
                    final_norm if last else ffn2_norm[i], emit_x=not last, norm_dtype=F32 if last else None)
    return x.reshape(batch, seq, d)
```

```python
import functools

import numpy as np
import jax
import jax.numpy as jnp
from jax import lax
from jax.experimental import pallas as pl
from jax.experimental.pallas import tpu as pltpu

D_MODEL = 2048
HEAD_DIM = 128
ROPE_THETA = 10000.0
NORM_EPS = 1e-6
NEG = -1e30

NSA_HEADS = 8
NSA_GROUPS = 2
HEADS_PER_GROUP = NSA_HEADS // NSA_GROUPS
CMP_LEN = 32
CMP_STRIDE = 16
CMP_HIDDEN = 256
SLC_LEN = 64
SLC_TOP = 8
WIN = 512
FOX_HEADS = 8
MLA_HEADS = 16
MLA_Q_RANK = 512
MLA_KV_RANK = 512
MLA_NOPE = 128
MLA_ROPE = 64
MLA_V = 128
D_FF = 5632

LANES = 128
VMEM_LIMIT = 56 * 1024 * 1024

BF = jnp.bfloat16
F32 = jnp.float32


def _params(sem, vmem=VMEM_LIMIT):
    return pltpu.CompilerParams(dimension_semantics=sem, vmem_limit_bytes=vmem)


def _rope_lanes(x, cos, sin):
    return x * cos + pltpu.roll(x, LANES // 2, axis=1) * sin


def _dot_t(a, b):
    return lax.dot_general(a, b, (((1,), (1,)), ((), ())), preferred_element_type=F32)


def _softmax_rows(s):
    m = jnp.max(s, axis=-1, keepdims=True)
    e = jnp.exp(s - m)
    return e, jnp.sum(e, axis=-1, keepdims=True)


def _ffn_kernel(*refs, n_f, emit_x, norm_dtype):
    x_ref, g_ref, wa_ref, wb_ref, w2_ref, gn_ref = refs[:6]
    outs = refs[6:-2]
    xn_sc, acc_sc = refs[-2:]
    j = pl.program_id(1)

    @pl.when(j == 0)
    def _():
        x = x_ref[...]
        ms = jnp.mean(x * x, axis=-1, keepdims=True)
        xn_sc[...] = (x * lax.rsqrt(ms + NORM_EPS) * g_ref[...]).astype(BF)
        acc_sc[...] = jnp.zeros_like(acc_sc)

    xn = xn_sc[...]
    h1 = jnp.dot(xn, wa_ref[...], preferred_element_type=F32)
    h2 = jnp.dot(xn, wb_ref[...], preferred_element_type=F32)
    a = (h1 * jax.nn.sigmoid(h1) * h2).astype(BF)
    acc_sc[...] += jnp.dot(a, w2_ref[...], preferred_element_type=F32)

    @pl.when(j == n_f - 1)
    def _():
        y = x_ref[...] + 0.5 * acc_sc[...]
        k = 0
        if emit_x:
            outs[k][...] = y
            k += 1
        if norm_dtype is not None:
            ms = jnp.mean(y * y, axis=-1, keepdims=True)
            outs[k][...] = (y * lax.rsqrt(ms + NORM_EPS) * gn_ref[...]).astype(norm_dtype)


def _ffn(x, g, w_in, w_out, g_next, *, emit_x, norm_dtype, tm=512, tf=512):
    n, d = x.shape
    n_f = D_FF // tf
    out_shape, out_specs = [], []
    if emit_x:
        out_shape.append(jax.ShapeDtypeStruct((n, d), F32))
        out_specs.append(pl.BlockSpec((tm, d), lambda i, j: (i, 0)))
    if norm_dtype is not None:
        out_shape.append(jax.ShapeDtypeStruct((n, d), norm_dtype))
        out_specs.append(pl.BlockSpec((tm, d), lambda i, j: (i, 0)))
    return pl.pallas_call(
        functools.partial(_ffn_kernel, n_f=n_f, emit_x=emit_x, norm_dtype=norm_dtype),
        grid=(n // tm, n_f),
        in_specs=[
            pl.BlockSpec((tm, d), lambda i, j: (i, 0)),
            pl.BlockSpec((1, d), lambda i, j: (0, 0)),
            pl.BlockSpec((d, tf), lambda i, j: (0, j)),
            pl.BlockSpec((d, tf), lambda i, j: (0, n_f + j)),
            pl.BlockSpec((tf, d), lambda i, j: (j, 0)),
            pl.BlockSpec((1, d), lambda i, j: (0, 0)),
        ],
        out_specs=out_specs,
        out_shape=out_shape,
        scratch_shapes=[pltpu.VMEM((tm, d), BF), pltpu.VMEM((tm, d), F32)],
        compiler_params=_params(("parallel", "arbitrary")),
        name="ffn",
    )(x, g.reshape(1, d), w_in, w_in, w_out, g_next.reshape(1, d))


def _mm_kernel(*refs, n_a, has_res, rope_j, rope_chunks):
    a_refs = refs[:n_a]
    w_refs = refs[n_a:2 * n_a]
    k = 2 * n_a
    res_ref = None
    if has_res:
        res_ref = refs[k]
        k += 1
    if rope_chunks is not None:
        cos_ref, sin_ref = refs[k], refs[k + 1]
        k += 2
    o_ref = refs[k]

    acc = jnp.dot(a_refs[0][...], w_refs[0][...], preferred_element_type=F32)
    for a_ref, w_ref in zip(a_refs[1:], w_refs[1:]):
        acc += jnp.dot(a_ref[...], w_ref[...], preferred_element_type=F32)
    if has_res:
        acc += res_ref[...]

    def store(v):
        o_ref[...] = v.reshape(o_ref.shape).astype(o_ref.dtype)

    if rope_chunks is None:
        store(acc)
    else:
        j = pl.program_id(1)
        is_rope = (j >= rope_j[0]) & (j < rope_j[1])

        @pl.when(is_rope)
        def _():
            cos, sin = cos_ref[...], sin_ref[...]
            parts = []
            for c, on in enumerate(rope_chunks):
                blk = acc[:, c * LANES:(c + 1) * LANES]
                parts.append(_rope_lanes(blk, cos, sin) if on else blk)
            store(jnp.concatenate(parts, axis=1))

        @pl.when(jnp.logical_not(is_rope))
        def _():
            store(acc)


def _mm(a_list, w_list, *, out_dtype, tm, tn, res=None, rope=None, seq=None, stacked_out=False, name="mm"):
    n = a_list[0].shape[0]
    n_out = w_list[0].shape[1]
    n_a = len(a_list)
    in_specs, args = [], []
    for a in a_list:
        in_specs.append(pl.BlockSpec((tm, a.shape[1]), lambda i, j: (i, 0)))
        args.append(a)
    for w in w_list:
        in_specs.append(pl.BlockSpec((w.shape[0], tn), lambda i, j: (0, j)))
        args.append(w)
    if res is not None:
        in_specs.append(pl.BlockSpec((tm, tn), lambda i, j: (i, j)))
        args.append(res)
    rope_j = rope_chunks = None
    if rope is not None:
        cos, sin, rope_j, rope_chunks = rope
        n_seq = seq // tm
        for t in (cos, sin):
            in_specs.append(pl.BlockSpec((tm, LANES), lambda i, j: (i % n_seq, 0)))
            args.append(t)
    if stacked_out:
        out_shape = jax.ShapeDtypeStruct((n_out // tn, n, tn), out_dtype)
        out_spec = pl.BlockSpec((1, tm, tn), lambda i, j: (j, i, 0))
    else:
        out_shape = jax.ShapeDtypeStruct((n, n_out), out_dtype)
        out_spec = pl.BlockSpec((tm, tn), lambda i, j: (i, j))
    return pl.pallas_call(
        functools.partial(_mm_kernel, n_a=n_a, has_res=res is not None, rope_j=rope_j, rope_chunks=rope_chunks),
        grid=(n // tm, n_out // tn),
        in_specs=in_specs,
        out_specs=out_spec,
        out_shape=out_shape,
        compiler_params=_params(("parallel", "arbitrary")),
        name=name,
    )(*args)


def _compress_kernel(x_ref, pe_ref, w1_ref, w2_ref, o_ref):
    x = x_ref[0, 0]
    half = CMP_STRIDE * HEAD_DIM
    xa = (x + pe_ref[0, 0:1, :]).astype(BF)
    xb = (x + pe_ref[0, 1:2, :]).astype(BF)
    ha = jnp.dot(xa, w1_ref[0, :half, :], preferred_element_type=F32)
    hb = jnp.dot(xb, w1_ref[0, half:, :], preferred_element_type=F32)
    hid = ha + pltpu.roll(hb, hb.shape[0] - 1, axis=0)
    act = jax.nn.gelu(hid, approximate=True).astype(BF)
    o_ref[0, 0] = jnp.dot(act, w2_ref[0], preferred_element_type=F32)


def _compress(xc, pe, w1, w2, batch):
    rows = xc.shape[2]
    return pl.pallas_call(
        _compress_kernel,
        grid=(2 * NSA_GROUPS, batch),
        in_specs=[
            pl.BlockSpec((1, 1, rows, CMP_STRIDE * HEAD_DIM), lambda c, b: (c, b, 0, 0)),
            pl.BlockSpec((1, 2, CMP_STRIDE * HEAD_DIM), lambda c, b: (c // NSA_GROUPS, 0, 0)),
            pl.BlockSpec((1, CMP_LEN * HEAD_DIM, CMP_HIDDEN), lambda c, b: (c // NSA_GROUPS, 0, 0)),
            pl.BlockSpec((1, CMP_HIDDEN, HEAD_DIM), lambda c, b: (c // NSA_GROUPS, 0, 0)),
        ],
        out_specs=pl.BlockSpec((1, 1, rows, HEAD_DIM), lambda c, b: (c, b, 0, 0)),
        out_shape=jax.ShapeDtypeStruct((2 * NSA_GROUPS, batch, rows, HEAD_DIM), F32),
        compiler_params=_params(("arbitrary", "arbitrary")),
        name="nsa_compress",
    )(xc, pe, w1, w2)


def _stacked_softmax_pv(s_raw, bias, v, scale, n_stack, tq):
    es, ls = [], []
    for h in range(n_stack):
        u = s_raw[h * tq:(h + 1) * tq] * scale + bias
        e = jnp.exp(u - jnp.max(u, axis=-1, keepdims=True))
        ls.append(jnp.sum(e, axis=-1, keepdims=True))
        es.append(e.astype(BF))
    o = jnp.dot(jnp.concatenate(es, axis=0), v, preferred_element_type=F32)
    return o / jnp.concatenate(ls, axis=0)


def _nsa_kernel(q_ref, ks_ref, kw_ref, vs_ref, vw_ref, kvc_ref, gate_ref, cos_ref, sin_ref, m_ref, o_ref,
                oslc_sc, *, tq, seq, key_step):
    qi = pl.program_id(1)
    hg = HEADS_PER_GROUP
    scale = HEAD_DIM ** -0.5
    n_cmp = (seq - CMP_LEN) // CMP_STRIDE + 1
    n_slc = seq // SLC_LEN
    band = WIN + tq
    neg_bf = jnp.asarray(NEG, BF)

    t_row = qi * tq + lax.broadcasted_iota(jnp.int32, (tq, 1), 0)
    cos = jnp.concatenate([cos_ref[...]] * hg, axis=0)
    sin = jnp.concatenate([sin_ref[...]] * hg, axis=0)
    gates = jax.nn.sigmoid(gate_ref[0])

    n_lane = lax.broadcasted_iota(jnp.int32, (1, LANES), 1)
    cmask = (n_lane * CMP_STRIDE + (CMP_LEN - 1) <= t_row) & (n_lane < n_cmp)
    cbias = jnp.where(cmask, 0.0, NEG)
    j_lane = lax.broadcasted_iota(jnp.int32, (1, n_slc), 1)
    slc_shift = SLC_LEN.bit_length() - 1
    cur = jnp.right_shift(t_row, slc_shift)
    forced = (j_lane == 0) | (j_lane == cur) | (j_lane == cur - 1)
    w_start = pl.multiple_of(jnp.maximum(qi * tq - WIN, 0), tq)
    w_pos = w_start + lax.broadcasted_iota(jnp.int32, (1, band), 1)
    wbias = jnp.where((w_pos <= t_row) & (w_pos > t_row - WIN), 0.0, NEG)

    qrs, sel_negs, o_cmps = [], [], []
    for g in range(NSA_GROUPS):
        qg = q_ref[0, :, g * hg * HEAD_DIM:(g + 1) * hg * HEAD_DIM]
        qp = jnp.concatenate([qg[:, h * HEAD_DIM:(h + 1) * HEAD_DIM] for h in range(hg)], axis=0)
        qrs.append(_rope_lanes(qp, cos, sin).astype(BF))
        qp = qp.astype(BF)

        kc = kvc_ref[g, 0].astype(BF)
        vc = kvc_ref[NSA_GROUPS + g, 0].astype(BF)
        s = _dot_t(qp, kc)
        ps = []
        for h in range(hg):
            u = s[h * tq:(h + 1) * tq] * scale + cbias
            e, l = _softmax_rows(u)
            ps.append(jnp.where(cmask, e / l, 0.0))
        o_cmps.append(jnp.dot(jnp.concatenate(ps, axis=0).astype(BF), vc, preferred_element_type=F32))

        p_sum = ps[0]
        for h in range(1, hg):
            p_sum = p_sum + ps[h]
        imp = jnp.dot(p_sum, m_ref[...], preferred_element_type=F32, precision=lax.Precision.HIGHEST)
        imp = jnp.where(j_lane > cur, -jnp.inf, jnp.where(forced, jnp.inf, imp))
        rank = jnp.zeros((tq, n_slc), jnp.int32)
        for i in range(n_slc):
            ci = imp[:, i:i + 1]
            beats = (ci > imp) | ((ci == imp) & (j_lane > i))
            rank = rank + beats.astype(jnp.int32)
        sel_negs.append(jnp.where(rank < SLC_TOP, 0.0, NEG).astype(BF))

    def slc_branch(width):
        blocks = width // SLC_LEN
        expand = jnp.where(jnp.right_shift(lax.broadcasted_iota(jnp.int32, (blocks, width), 1), slc_shift)
                           == lax.broadcasted_iota(jnp.int32, (blocks, width), 0), 1.0, 0.0).astype(BF)
        s_pos = lax.broadcasted_iota(jnp.int32, (1, width), 1)
        for g in range(NSA_GROUPS):
            bias = jnp.dot(sel_negs[g][:, :blocks], expand, preferred_element_type=F32)
            bias = jnp.where(s_pos <= t_row, bias, NEG)
            ks = ks_ref[0, :width, g * HEAD_DIM:(g + 1) * HEAD_DIM]
            vs = vs_ref[0, :width, g * HEAD_DIM:(g + 1) * HEAD_DIM]
            oslc_sc[g] = _stacked_softmax_pv(_dot_t(qrs[g], ks), bias, vs, scale, hg, tq)

    n_widths = seq // key_step
    super_row = qi // (key_step // tq)
    for k in range(n_widths):
        pl.when(super_row == k)(functools.partial(slc_branch, (k + 1) * key_step))

    for g in range(NSA_GROUPS):
        kw = kw_ref[0, pl.ds(w_start, band), g * HEAD_DIM:(g + 1) * HEAD_DIM]
        vw = vw_ref[0, pl.ds(w_start, band), g * HEAD_DIM:(g + 1) * HEAD_DIM]
        o_win = _stacked_softmax_pv(_dot_t(qrs[g], kw), wbias, vw, scale, hg, tq)
        o_slc = oslc_sc[g]
        o_cmp = o_cmps[g]
        for hh in range(hg):
            h = g * hg + hh
            rows = slice(hh * tq, (hh + 1) * tq)
            o = (o_cmp[rows] * gates[:, 3 * h:3 * h + 1]
                 + o_slc[rows] * gates[:, 3 * h + 1:3 * h + 2]
                 + o_win[rows] * gates[:, 3 * h + 2:3 * h + 3])
            o_ref[0, :, h * HEAD_DIM:(h + 1) * HEAD_DIM] = o.astype(o_ref.dtype)


def _nsa(q, main, kvc, planes, cos, sin, m_cs, *, batch, seq, tq=128, key_step=512):
    nq = seq // tq
    kv_w = NSA_GROUPS * HEAD_DIM
    return pl.pallas_call(
        functools.partial(_nsa_kernel, tq=tq, seq=seq, key_step=key_step),
        grid=(batch, nq),
        in_specs=[
            pl.BlockSpec((1, tq, NSA_HEADS * HEAD_DIM), lambda b, i: (b, i, 0)),
            pl.BlockSpec((1, seq, kv_w), lambda b, i: (b, 0, 0)),
            pl.BlockSpec((1, seq, kv_w), lambda b, i: (b, 0, 1)),
            pl.BlockSpec((1, seq, kv_w), lambda b, i: (b, 0, 2)),
            pl.BlockSpec((1, seq, kv_w), lambda b, i: (b, 0, 3)),
            pl.BlockSpec((2 * NSA_GROUPS, 1, LANES, HEAD_DIM), lambda b, i: (0, b, 0, 0)),
            pl.BlockSpec((1, tq, LANES), lambda b, i: (2 * NSA_GROUPS, b * nq + i, 0)),
            pl.BlockSpec((tq, LANES), lambda b, i: (i, 0)),
            pl.BlockSpec((tq, LANES), lambda b, i: (i, 0)),
            pl.BlockSpec(m_cs.shape, lambda b, i: (0, 0)),
        ],
        out_specs=pl.BlockSpec((1, tq, NSA_HEADS * HEAD_DIM), lambda b, i: (b, i, 0)),
        out_shape=jax.ShapeDtypeStruct((batch, seq, NSA_HEADS * HEAD_DIM), BF),
        scratch_shapes=[pltpu.VMEM((NSA_GROUPS, HEADS_PER_GROUP * tq, HEAD_DIM), F32)],
        compiler_params=_params(("parallel", "arbitrary")),
        name="nsa_attention",
    )(q, main, main, main, main, kvc, planes, cos, sin, m_cs)


def _cumdecay_kernel(x_ref, b_ref, o_ref):
    z = x_ref[0] + b_ref[...]
    c = jnp.minimum(z, 0.0) - jnp.log1p(jnp.exp(-jnp.abs(z)))
    rows = c.shape[0]
    r = lax.broadcasted_iota(jnp.int32, (rows, 1), 0)
    k = 1
    while k < rows:
        c = c + jnp.where(r >= k, pltpu.roll(c, k, axis=0), 0.0)
        k *= 2
    o_ref[0] = c


def _cumdecay(planes, bias, *, batch, seq):
    return pl.pallas_call(
        _cumdecay_kernel,
        grid=(batch,),
        in_specs=[pl.BlockSpec((1, seq, LANES), lambda b: (2 * NSA_GROUPS, b, 0)),
                  pl.BlockSpec((1, LANES), lambda b: (0, 0))],
        out_specs=pl.BlockSpec((1, seq, LANES), lambda b: (b, 0, 0)),
        out_shape=jax.ShapeDtypeStruct((batch, seq, LANES), F32),
        compiler_params=_params(("arbitrary",)),
        name="fox_cumdecay",
    )(planes, bias)


def _attn_kernel(*refs, tq, seq, scale, n_qk, decay):
    q_ref = refs[0]
    k_refs = refs[1:1 + n_qk]
    v_ref = refs[1 + n_qk]
    k = 2 + n_qk
    if decay:
        ck_ref = refs[k]
        k += 1
    o_ref = refs[k]

    row = lax.broadcasted_iota(jnp.int32, (tq, 1), 0)
    col = lax.broadcasted_iota(jnp.int32, (1, tq), 1)
    tri = col <= row
    exp2_scale = scale * float(np.log2(np.e))

    for i in range(seq // tq):
        lo, hi = i * tq, (i + 1) * tq
        q = q_ref[0, lo:hi, :]
        s = _dot_t(q[:, :HEAD_DIM], k_refs[0][0, :hi, :])
        for c in range(1, n_qk):
            s = s + _dot_t(q[:, c * HEAD_DIM:(c + 1) * HEAD_DIM], k_refs[c][0, :hi, :])
        if decay:
            ck = ck_ref[0, :, :hi]
            cq = jnp.sum(jnp.where(col == row, ck[:, lo:hi], 0.0), axis=1, keepdims=True)
            s = s * scale + (cq - ck)
            prob = lambda u, m: jnp.exp(u - m)
        else:
            prob = lambda u, m: jnp.exp2((u - m) * exp2_scale)
        s_diag = jnp.where(tri, s[:, lo:hi], NEG)
        m = jnp.max(s_diag, axis=-1, keepdims=True)
        if i > 0:
            m = jnp.maximum(m, jnp.max(s[:, :lo], axis=-1, keepdims=True))
        e_diag = prob(s_diag, m)
        l = jnp.sum(e_diag, axis=-1, keepdims=True)
        e = e_diag.astype(BF)
        if i > 0:
            e_off = prob(s[:, :lo], m)
            l = l + jnp.sum(e_off, axis=-1, keepdims=True)
            e = jnp.concatenate([e_off.astype(BF), e], axis=1)
        o = jnp.dot(e, v_ref[0, :hi, :], preferred_element_type=F32) / l
        o_ref[0, lo:hi, :] = o.astype(o_ref.dtype)


def _attention(q_arr, q_blk0, k_arrs, k_blk0s, k_per_head, v_arr, v_blk0, *, batch, seq, heads, scale,
               decay=None, tq=256, name="attention"):
    n_qk = len(k_arrs)
    in_specs = [pl.BlockSpec((1, seq, n_qk * HEAD_DIM), lambda b, h: (b, 0, q_blk0 + h))]
    args = [q_arr]
    for arr, blk0, per_head in zip(k_arrs, k_blk0s, k_per_head):
        if per_head:
            in_specs.append(pl.BlockSpec((1, seq, HEAD_DIM), lambda b, h, blk0=blk0: (b, 0, blk0 + h)))
        else:
            in_specs.append(pl.BlockSpec((1, seq, HEAD_DIM), lambda b, h, blk0=blk0: (b, 0, blk0)))
        args.append(arr)
    in_specs.append(pl.BlockSpec((1, seq, HEAD_DIM), lambda b, h: (b, 0, v_blk0 + h)))
    args.append(v_arr)
    if decay is not None:
        in_specs.append(pl.BlockSpec((1, 1, seq), lambda b, h: (b * heads + h, 0, 0)))
        args.append(decay)
    return pl.pallas_call(
        functools.partial(_attn_kernel, tq=tq, seq=seq, scale=scale, n_qk=n_qk, decay=decay is not None),
        grid=(batch, heads),
        in_specs=in_specs,
        out_specs=pl.BlockSpec((1, seq, HEAD_DIM), lambda b, h: (b, 0, h)),
        out_shape=jax.ShapeDtypeStruct((batch, seq, heads * HEAD_DIM), BF),
        compiler_params=_params(("parallel", "arbitrary")),
        name=name,
    )(*args)


def _mla_latent_kernel(h_ref, w_ref, gq_ref, gkv_ref, cos_ref, sin_ref, cq_ref, ckv_ref, kr_ref):
    c = jnp.dot(h_ref[...], w_ref[...], preferred_element_type=F32)

    def norm(v, g):
        ms = jnp.mean(v * v, axis=-1, keepdims=True)
        return v * lax.rsqrt(ms + NORM_EPS) * g

    cq_ref[...] = norm(c[:, :MLA_Q_RANK], gq_ref[...]).astype(BF)
    ckv_ref[...] = norm(c[:, MLA_Q_RANK:MLA_Q_RANK + MLA_KV_RANK], gkv_ref[...]).astype(BF)
    kr_ref[...] = _rope_lanes(c[:, MLA_Q_RANK + MLA_KV_RANK:], cos_ref[...], sin_ref[...]).astype(BF)


def _mla_latent(h, w, gq, gkv, cos, sin, *, seq, tm=512):
    n = h.shape[0]
    n_seq = seq // tm
    return pl.pallas_call(
        _mla_latent_kernel,
        grid=(n // tm,),
        in_specs=[
            pl.BlockSpec((tm, h.shape[1]), lambda i: (i, 0)),
            pl.BlockSpec(w.shape, lambda i: (0, 0)),
            pl.BlockSpec((1, MLA_Q_RANK), lambda i: (0, 0)),
            pl.BlockSpec((1, MLA_KV_RANK), lambda i: (0, 0)),
            pl.BlockSpec((tm, LANES), lambda i: (i % n_seq, 0)),
            pl.BlockSpec((tm, LANES), lambda i: (i % n_seq, 0)),
        ],
        out_specs=[
            pl.BlockSpec((tm, MLA_Q_RANK), lambda i: (i, 0)),
            pl.BlockSpec((tm, MLA_KV_RANK), lambda i: (i, 0)),
            pl.BlockSpec((tm, LANES), lambda i: (i, 0)),
        ],
        out_shape=[
            jax.ShapeDtypeStruct((n, MLA_Q_RANK), BF),
            jax.ShapeDtypeStruct((n, MLA_KV_RANK), BF),
            jax.ShapeDtypeStruct((n, LANES), BF),
        ],
        compiler_params=_params(("parallel",)),
        name="mla_latent",
    )(h, w, gq.reshape(1, -1), gkv.reshape(1, -1), cos, sin)


def _rope_tables(seq, dim):
    half = dim // 2
    inv = 1.0 / (ROPE_THETA ** (np.arange(0, dim, 2, dtype=np.float32) / dim))
    ang = np.arange(seq, dtype=np.float32)[:, None] * inv.astype(np.float32)[None, :]
    cos = np.zeros((seq, LANES), np.float32)
    sin = np.zeros((seq, LANES), np.float32)
    cos[:, :half] = np.cos(ang)
    cos[:, LANES // 2:LANES // 2 + half] = np.cos(ang)
    sin[:, :half] = -np.sin(ang)
    sin[:, LANES // 2:LANES // 2 + half] = np.sin(ang)
    return jnp.asarray(cos), jnp.asarray(sin)


def _cmp_to_slc(seq):
    n_cmp = (seq - CMP_LEN) // CMP_STRIDE + 1
    n_slc = seq // SLC_LEN
    c0 = np.arange(n_cmp) * CMP_STRIDE
    c1 = c0 + CMP_LEN
    s0 = np.arange(n_slc) * SLC_LEN
    s1 = s0 + SLC_LEN
    ov = np.clip(np.minimum(c1[:, None], s1[None, :]) - np.maximum(c0[:, None], s0[None, :]), 0, None)
    m = np.zeros((LANES, n_slc), np.float32)
    m[:n_cmp] = ov / CMP_LEN
    return jnp.asarray(m)


def _spread_rope_cols(w):
    half = MLA_ROPE // 2
    z = jnp.zeros(w.shape[:-1] + (LANES // 2 - half,), w.dtype)
    return jnp.concatenate([w[..., :half], z, w[..., half:], z], axis=-1)


def _hybrid_mixer(x, h, w_in, w_out, cmp_pe, cmp_w1, cmp_w2, f_bias, *, batch, seq):
    n = batch * seq
    nq_w = NSA_HEADS * HEAD_DIM
    kv_w = NSA_GROUPS * HEAD_DIM
    fx_w = FOX_HEADS * HEAD_DIM
    o = 0
    cols = {}
    for name, width in (("q_n", nq_w), ("k_c", kv_w), ("v_c", kv_w), ("k_s", kv_w), ("v_s", kv_w),
                        ("k_w", kv_w), ("v_w", kv_w), ("g", 3 * NSA_HEADS), ("q_f", fx_w), ("k_f", fx_w),
                        ("v_f", fx_w), ("f", FOX_HEADS)):
        cols[name] = w_in[:, o:o + width]
        o += width
    n_misc = 3 * NSA_HEADS + FOX_HEADS
    w_qn = cols["q_n"].astype(BF)
    w_main = jnp.concatenate([cols[c] for c in ("k_s", "k_w", "v_s", "v_w", "q_f", "k_f", "v_f")], axis=1).astype(BF)
    w_planes = jnp.concatenate([cols["k_c"], cols["v_c"], cols["g"], cols["f"],
                                jnp.zeros((D_MODEL, LANES - n_misc), F32)], axis=1).astype(BF)

    cos, sin = _rope_tables(seq, HEAD_DIM)
    q_n = _mm([h], [w_qn], out_dtype=F32, tm=1024, tn=512, name="proj_q_nsa")
    main = _mm([h], [w_main], out_dtype=BF, tm=1024, tn=512, rope=(cos, sin, (0, 1), (True,) * 4), seq=seq,
               name="proj_main")
    planes = _mm([h], [w_planes], out_dtype=F32, tm=1024, tn=LANES, stacked_out=True, name="proj_planes")

    rows16 = seq // CMP_STRIDE
    kvc = _compress(planes.reshape(2 * NSA_GROUPS + 1, batch, rows16, CMP_STRIDE * HEAD_DIM),
                    cmp_pe.reshape(2, 2, CMP_STRIDE * HEAD_DIM), cmp_w1.astype(BF), cmp_w2.astype(BF), batch)

    o_nsa = _nsa(q_n.reshape(batch, seq, nq_w), main.reshape(batch, seq, -1), kvc, planes, cos, sin,
                 _cmp_to_slc(seq), batch=batch, seq=seq)

    bias = jnp.zeros((1, LANES), F32).at[0, 3 * NSA_HEADS:n_misc].set(f_bias)
    cum = _cumdecay(planes, bias, batch=batch, seq=seq)[:, :, 3 * NSA_HEADS:n_misc]
    cum_t = jnp.swapaxes(cum, 1, 2).reshape(batch * FOX_HEADS, seq)
    main3 = main.reshape(batch, seq, -1)
    blk = lambda col: col // HEAD_DIM
    o_fox = _attention(main3, blk(4 * kv_w), [main3], [blk(4 * kv_w + fx_w)], [True], main3, blk(4 * kv_w + 2 * fx_w),
                       batch=batch, seq=seq, heads=FOX_HEADS, scale=HEAD_DIM ** -0.5,
                       decay=cum_t[:, None, :], name="fox_attention")

    w_o = w_out.astype(BF)
    return _mm([o_nsa.reshape(n, nq_w), o_fox.reshape(n, fx_w)], [w_o[:nq_w], w_o[nq_w:]],
               out_dtype=F32, tm=1024, tn=512, res=x, name="proj_out_hybrid")


def _mla_mixer(x, h, w_in, q_norm, kv_norm, w_uq, w_ukv, w_out, *, batch, seq):
    n = batch * seq
    qk = MLA_NOPE + MLA_ROPE
    w_lat = jnp.concatenate([w_in[:, :MLA_Q_RANK + MLA_KV_RANK],
                             _spread_rope_cols(w_in[:, MLA_Q_RANK + MLA_KV_RANK:])], axis=1).astype(BF)
    wq = w_uq.reshape(MLA_Q_RANK, MLA_HEADS, qk)
    wq = jnp.concatenate([wq[..., :MLA_NOPE], _spread_rope_cols(wq[..., MLA_NOPE:])], axis=-1)
    wq = wq.reshape(MLA_Q_RANK, MLA_HEADS * 2 * HEAD_DIM).astype(BF)
    wkv = w_ukv.reshape(MLA_KV_RANK, MLA_HEADS, MLA_NOPE + MLA_V)
    wkv = jnp.concatenate([wkv[..., :MLA_NOPE].reshape(MLA_KV_RANK, -1),
                           wkv[..., MLA_NOPE:].reshape(MLA_KV_RANK, -1)], axis=1).astype(BF)

    cos, sin = _rope_tables(seq, MLA_ROPE)
    cq, ckv, kr = _mla_latent(h, w_lat, q_norm, kv_norm, cos, sin, seq=seq)
    q = _mm([cq], [wq], out_dtype=BF, tm=1024, tn=512, rope=(cos, sin, (0, wq.shape[1] // 512), (False, True) * 2),
            seq=seq, name="proj_mla_q")
    kv = _mm([ckv], [wkv], out_dtype=BF, tm=1024, tn=512, name="proj_mla_kv")

    q3 = q.reshape(batch, seq, -1)
    kv3 = kv.reshape(batch, seq, -1)
    kr3 = kr.reshape(batch, seq, LANES)
    o = _attention(q3, 0, [kv3, kr3], [0, 0], [True, False], kv3, MLA_HEADS, batch=batch, seq=seq,
                   heads=MLA_HEADS, scale=qk ** -0.5, name="mla_attention")
    return _mm([o.reshape(n, MLA_HEADS * MLA_V)], [w_out.astype(BF)], out_dtype=F32, tm=1024, tn=512, res=x,
               name="proj_out_mla")


def kernel(x, ffn1_norm, ffn1_w_in, ffn1_w_out, mix_norm, ffn2_norm, ffn2_w_in, ffn2_w_out, hyb_w_in, hyb_w_out, nsa_cmp_pe, nsa_cmp_w1, nsa_cmp_w2, fox_f_bias, mla_w_in, mla_q_norm, mla_kv_norm, mla_w_uq, mla_w_ukv, mla_w_out, final_norm):
    batch, seq, d = x.shape
    depth = ffn1_norm.shape[0]
    x = x.reshape(batch * seq, d)
    for i in range(depth):
        x, h = _ffn(x, ffn1_norm[i], ffn1_w_in[i].astype(BF), ffn1_w_out[i].astype(BF), mix_norm[i],
                    emit_x=True, norm_dtype=BF)
        if i % 2 == 0:
            e = i // 2
            x = _hybrid_mixer(x, h, hyb_w_in[e], hyb_w_out[e], nsa_cmp_pe[e], nsa_cmp_w1[e], nsa_cmp_w2[e],
                              fox_f_bias[e], batch=batch, seq=seq)
        else:
            o = i // 2
            x = _mla_mixer(x, h, mla_w_in[o], mla_q_norm[o], mla_kv_norm[o], mla_w_uq[o], mla_w_ukv[o],
                           mla_w_out[o], batch=batch, seq=seq)
        last = i == depth - 1
        (x,) = _ffn(x, ffn2_norm[i], ffn2_w_in[i].astype(BF), ffn2_w_out[i].astype(BF),
                    final_norm if last else ffn2_norm[i], emit_x=not last, norm_dtype=F32 if last else None)
    return x.reshape(batch, seq, d)
```

```python
import functools

import numpy as np
import jax
import jax.numpy as jnp
from jax import lax
from jax.experimental import pallas as pl
from jax.experimental.pallas import tpu as pltpu

D_MODEL = 2048
HEAD_DIM = 128
ROPE_THETA = 10000.0
NORM_EPS = 1e-6
NEG = -1e30

NSA_HEADS = 8
NSA_GROUPS = 2
HEADS_PER_GROUP = NSA_HEADS // NSA_GROUPS
CMP_LEN = 32
CMP_STRIDE = 16
CMP_HIDDEN = 256
SLC_LEN = 64
SLC_TOP = 8
WIN = 512
FOX_HEADS = 8
MLA_HEADS = 16
MLA_Q_RANK = 512
MLA_KV_RANK = 512
MLA_NOPE = 128
MLA_ROPE = 64
MLA_V = 128
D_FF = 5632

LANES = 128
VMEM_LIMIT = 56 * 1024 * 1024

BF = jnp.bfloat16
F32 = jnp.float32


def _params(sem, vmem=VMEM_LIMIT):
    return pltpu.CompilerParams(dimension_semantics=sem, vmem_limit_bytes=vmem)


def _rope_lanes(x, cos, sin):
    return x * cos + pltpu.roll(x, LANES // 2, axis=1) * sin


def _dot_t(a, b):
    return lax.dot_general(a, b, (((1,), (1,)), ((), ())), preferred_element_type=F32)


def _softmax_rows(s):
    m = jnp.max(s, axis=-1, keepdims=True)
    e = jnp.exp(s - m)
    return e, jnp.sum(e, axis=-1, keepdims=True)


def _rms_normed(x, g):
    ms = jnp.mean(x * x, axis=-1, keepdims=True)
    return x * lax.rsqrt(ms + NORM_EPS) * g


def _ffn_up_kernel(x_ref, g_ref, wa_ref, wb_ref, o_ref, xn_sc):
    @pl.when(pl.program_id(1) == 0)
    def _():
        xn_sc[...] = _rms_normed(x_ref[...], g_ref[...]).astype(BF)

    xn = xn_sc[...]
    h1 = jnp.dot(xn, wa_ref[...].astype(BF), preferred_element_type=F32)
    h2 = jnp.dot(xn, wb_ref[...].astype(BF), preferred_element_type=F32)
    o_ref[...] = (h1 * jax.nn.sigmoid(h1) * h2 * 0.5).astype(BF)


def _ffn_up(x, g, w_in, layer, *, tm=1024, tf=512):
    n, d = x.shape
    n_f = D_FF // tf
    return pl.pallas_call(
        _ffn_up_kernel,
        grid=(n // tm, n_f),
        in_specs=[
            pl.BlockSpec((tm, d), lambda i, j: (i, 0)),
            pl.BlockSpec((None, 1, d), lambda i, j: (layer, 0, 0)),
            pl.BlockSpec((None, d, tf), lambda i, j: (layer, 0, j)),
            pl.BlockSpec((None, d, tf), lambda i, j: (layer, 0, n_f + j)),
        ],
        out_specs=pl.BlockSpec((tm, tf), lambda i, j: (i, j)),
        out_shape=jax.ShapeDtypeStruct((n, D_FF), BF),
        scratch_shapes=[pltpu.VMEM((tm, d), BF)],
        compiler_params=_params(("parallel", "arbitrary")),
        name="ffn_up",
    )(x, g, w_in, w_in)


def _norm_kernel(x_ref, g_ref, o_ref):
    o_ref[...] = _rms_normed(x_ref[...], g_ref[...]).astype(o_ref.dtype)


def _rmsnorm(x, g, *, tm=512):
    n, d = x.shape
    return pl.pallas_call(
        _norm_kernel,
        grid=(n // tm,),
        in_specs=[pl.BlockSpec((tm, d), lambda i: (i, 0)), pl.BlockSpec((1, d), lambda i: (0, 0))],
        out_specs=pl.BlockSpec((tm, d), lambda i: (i, 0)),
        out_shape=jax.ShapeDtypeStruct((n, d), x.dtype),
        compiler_params=_params(("parallel",)),
        name="rmsnorm",
    )(x, g.reshape(1, d))


class _W:
    def __init__(self, arr, *, layer=None, rows=None, row_blk=0, col_blk0=0):
        self.arr, self.layer, self.row_blk, self.col_blk0 = arr, layer, row_blk, col_blk0
        self.rows = rows if rows is not None else arr.shape[-2]

    def spec(self, tn):
        layer, row_blk, col_blk0 = self.layer, self.row_blk, self.col_blk0
        if self.arr.ndim == 3:
            return pl.BlockSpec((None, self.rows, tn), lambda i, j: (layer, row_blk, col_blk0 + j))
        return pl.BlockSpec((self.rows, tn), lambda i, j: (row_blk, col_blk0 + j))


def _mm_kernel(*refs, n_a, has_norm, has_res, rope_j, rope_chunks):
    a_refs = refs[:n_a]
    w_refs = refs[n_a:2 * n_a]
    k = 2 * n_a
    res_ref = None
    if has_norm:
        g_ref = refs[k]
        k += 1
    if has_res:
        res_ref = refs[k]
        k += 1
    if rope_chunks is not None:
        cos_ref, sin_ref = refs[k], refs[k + 1]
        k += 2
    o_ref = refs[k]

    if has_norm:
        xn_sc = refs[k + 1]

        @pl.when(pl.program_id(1) == 0)
        def _():
            xn_sc[...] = _rms_normed(a_refs[0][...], g_ref[...]).astype(BF)

        lhs = [xn_sc[...]]
    else:
        lhs = [a_ref[...] for a_ref in a_refs]

    acc = jnp.dot(lhs[0], w_refs[0][...].astype(BF), preferred_element_type=F32)
    for a, w_ref in zip(lhs[1:], w_refs[1:]):
        acc += jnp.dot(a, w_ref[...].astype(BF), preferred_element_type=F32)
    if has_res:
        acc += res_ref[...]

    def store(v):
        o_ref[...] = v.reshape(o_ref.shape).astype(o_ref.dtype)

    if rope_chunks is None:
        store(acc)
    else:
        j = pl.program_id(1)
        is_rope = (j >= rope_j[0]) & (j < rope_j[1])

        @pl.when(is_rope)
        def _():
            cos, sin = cos_ref[...], sin_ref[...]
            parts = []
            for c, on in enumerate(rope_chunks):
                blk = acc[:, c * LANES:(c + 1) * LANES]
                parts.append(_rope_lanes(blk, cos, sin) if on else blk)
            store(jnp.concatenate(parts, axis=1))

        @pl.when(jnp.logical_not(is_rope))
        def _():
            store(acc)


def _mm(a_list, w_list, n_out, *, out_dtype, tm, tn, norm_g=None, res=None, rope=None, seq=None,
        stacked_out=False, name="mm"):
    n = a_list[0].shape[0]
    n_a = len(a_list)
    in_specs, args, scratch = [], [], []
    for a in a_list:
        in_specs.append(pl.BlockSpec((tm, a.shape[1]), lambda i, j: (i, 0)))
        args.append(a)
    for w in w_list:
        in_specs.append(w.spec(tn))
        args.append(w.arr)
    if norm_g is not None:
        assert n_a == 1
        in_specs.append(pl.BlockSpec((1, norm_g.shape[-1]), lambda i, j: (0, 0)))
        args.append(norm_g.reshape(1, -1))
        scratch.append(pltpu.VMEM((tm, a_list[0].shape[1]), BF))
    if res is not None:
        in_specs.append(pl.BlockSpec((tm, tn), lambda i, j: (i, j)))
        args.append(res)
    rope_j = rope_chunks = None
    if rope is not None:
        cos, sin, rope_j, rope_chunks = rope
        n_seq = seq // tm
        for t in (cos, sin):
            in_specs.append(pl.BlockSpec((tm, LANES), lambda i, j: (i % n_seq, 0)))
            args.append(t)
    if stacked_out:
        out_shape = jax.ShapeDtypeStruct((n_out // tn, n, tn), out_dtype)
        out_spec = pl.BlockSpec((1, tm, tn), lambda i, j: (j, i, 0))
    else:
        out_shape = jax.ShapeDtypeStruct((n, n_out), out_dtype)
        out_spec = pl.BlockSpec((tm, tn), lambda i, j: (i, j))
    return pl.pallas_call(
        functools.partial(_mm_kernel, n_a=n_a, has_norm=norm_g is not None, has_res=res is not None,
                          rope_j=rope_j, rope_chunks=rope_chunks),
        grid=(n // tm, n_out // tn),
        in_specs=in_specs,
        out_specs=out_spec,
        out_shape=out_shape,
        scratch_shapes=scratch,
        compiler_params=_params(("parallel", "arbitrary")),
        name=name,
    )(*args)


def _compress_kernel(x_ref, pe_ref, w1_ref, w2_ref, o_ref):
    x = x_ref[0, 0]
    half = CMP_STRIDE * HEAD_DIM
    xa = (x + pe_ref[0, 0:1, :]).astype(BF)
    xb = (x + pe_ref[0, 1:2, :]).astype(BF)
    ha = jnp.dot(xa, w1_ref[0, :half, :].astype(BF), preferred_element_type=F32)
    hb = jnp.dot(xb, w1_ref[0, half:, :].astype(BF), preferred_element_type=F32)
    hid = ha + pltpu.roll(hb, hb.shape[0] - 1, axis=0)
    act = jax.nn.gelu(hid, approximate=True).astype(BF)
    o_ref[0, 0] = jnp.dot(act, w2_ref[0].astype(BF), preferred_element_type=F32)


def _compress(xc, pe, w1, w2, batch):
    rows = xc.shape[2]
    return pl.pallas_call(
        _compress_kernel,
        grid=(2 * NSA_GROUPS, batch),
        in_specs=[
            pl.BlockSpec((1, 1, rows, CMP_STRIDE * HEAD_DIM), lambda c, b: (c, b, 0, 0)),
            pl.BlockSpec((1, 2, CMP_STRIDE * HEAD_DIM), lambda c, b: (c // NSA_GROUPS, 0, 0)),
            pl.BlockSpec((1, CMP_LEN * HEAD_DIM, CMP_HIDDEN), lambda c, b: (c // NSA_GROUPS, 0, 0)),
            pl.BlockSpec((1, CMP_HIDDEN, HEAD_DIM), lambda c, b: (c // NSA_GROUPS, 0, 0)),
        ],
        out_specs=pl.BlockSpec((1, 1, rows, HEAD_DIM), lambda c, b: (c, b, 0, 0)),
        out_shape=jax.ShapeDtypeStruct((2 * NSA_GROUPS, batch, rows, HEAD_DIM), F32),
        compiler_params=_params(("arbitrary", "arbitrary")),
        name="nsa_compress",
    )(xc, pe, w1, w2)


def _stacked_softmax_pv(s_raw, bias, v, scale, n_stack, tq):
    exp2_scale = scale * float(np.log2(np.e))
    es, ls = [], []
    for h in range(n_stack):
        u = s_raw[h * tq:(h + 1) * tq] + bias
        e = jnp.exp2((u - jnp.max(u, axis=-1, keepdims=True)) * exp2_scale)
        ls.append(jnp.sum(e, axis=-1, keepdims=True))
        es.append(e.astype(BF))
    o = jnp.dot(jnp.concatenate(es, axis=0), v, preferred_element_type=F32)
    return o / jnp.concatenate(ls, axis=0)


def _nsa_kernel(q_ref, ks_ref, kw_ref, vs_ref, vw_ref, kvc_ref, gate_ref, cos_ref, sin_ref, m_ref, o_ref,
                oslc_sc, *, tq, seq, key_step):
    qi = pl.program_id(1)
    hg = HEADS_PER_GROUP
    scale = HEAD_DIM ** -0.5
    n_cmp = (seq - CMP_LEN) // CMP_STRIDE + 1
    n_slc = seq // SLC_LEN
    band = WIN + tq
    neg_bf = jnp.asarray(NEG, BF)

    t_row = qi * tq + lax.broadcasted_iota(jnp.int32, (tq, 1), 0)
    cos = jnp.concatenate([cos_ref[...]] * hg, axis=0)
    sin = jnp.concatenate([sin_ref[...]] * hg, axis=0)
    gates = jax.nn.sigmoid(gate_ref[0])

    n_lane = lax.broadcasted_iota(jnp.int32, (1, LANES), 1)
    cmask = (n_lane * CMP_STRIDE + (CMP_LEN - 1) <= t_row) & (n_lane < n_cmp)
    cbias = jnp.where(cmask, 0.0, NEG)
    j_blk = lax.broadcasted_iota(jnp.int32, (n_slc, 1), 0)
    slc_shift = SLC_LEN.bit_length() - 1
    cur = jnp.right_shift(qi * tq + lax.broadcasted_iota(jnp.int32, (1, tq), 1), slc_shift)
    forced = (j_blk == 0) | (j_blk == cur) | (j_blk == cur - 1)
    w_start = pl.multiple_of(jnp.maximum(qi * tq - WIN, 0), tq)
    w_pos = w_start + lax.broadcasted_iota(jnp.int32, (1, band), 1)
    wbias = jnp.where((w_pos <= t_row) & (w_pos > t_row - WIN), 0.0, NEG)

    qrs, sel_negs, o_cmps = [], [], []
    for g in range(NSA_GROUPS):
        qg = q_ref[0, :, g * hg * HEAD_DIM:(g + 1) * hg * HEAD_DIM]
        qp = jnp.concatenate([qg[:, h * HEAD_DIM:(h + 1) * HEAD_DIM] for h in range(hg)], axis=0)
        qrs.append(_rope_lanes(qp, cos, sin).astype(BF))
        qp = qp.astype(BF)

        kc = kvc_ref[g, 0].astype(BF)
        vc = kvc_ref[NSA_GROUPS + g, 0].astype(BF)
        s = _dot_t(qp, kc)
        ps = []
        for h in range(hg):
            u = s[h * tq:(h + 1) * tq] * scale + cbias
            e, l = _softmax_rows(u)
            ps.append(jnp.where(cmask, e / l, 0.0))
        o_cmps.append(jnp.dot(jnp.concatenate(ps, axis=0).astype(BF), vc, preferred_element_type=F32))

        p_sum = ps[0]
        for h in range(1, hg):
            p_sum = p_sum + ps[h]
        imp = lax.dot_general(m_ref[...], p_sum, (((1,), (1,)), ((), ())), preferred_element_type=F32,
                              precision=lax.Precision.HIGHEST)
        imp = jnp.where(j_blk > cur, -jnp.inf, jnp.where(forced, jnp.inf, imp))
        rank = jnp.zeros((n_slc, tq), jnp.int32)
        for i in range(n_slc):
            ci = imp[i:i + 1, :]
            beats = (ci > imp) | ((ci == imp) & (j_blk > i))
            rank = rank + beats.astype(jnp.int32)
        sel_negs.append(jnp.where(rank < SLC_TOP, 0.0, NEG).astype(BF))

    def slc_branch(width):
        blocks = width // SLC_LEN
        expand = jnp.where(jnp.right_shift(lax.broadcasted_iota(jnp.int32, (blocks, width), 1), slc_shift)
                           == lax.broadcasted_iota(jnp.int32, (blocks, width), 0), 1.0, 0.0).astype(BF)
        s_pos = lax.broadcasted_iota(jnp.int32, (1, width), 1)
        for g in range(NSA_GROUPS):
            bias = lax.dot_general(sel_negs[g][:blocks], expand, (((0,), (0,)), ((), ())),
                                   preferred_element_type=F32)
            bias = jnp.where(s_pos <= t_row, bias, NEG)
            ks = ks_ref[0, :width, g * HEAD_DIM:(g + 1) * HEAD_DIM]
            vs = vs_ref[0, :width, g * HEAD_DIM:(g + 1) * HEAD_DIM]
            oslc_sc[g] = _stacked_softmax_pv(_dot_t(qrs[g], ks), bias, vs, scale, hg, tq)

    n_widths = seq // key_step
    super_row = qi // (key_step // tq)
    for k in range(n_widths):
        pl.when(super_row == k)(functools.partial(slc_branch, (k + 1) * key_step))

    for g in range(NSA_GROUPS):
        kw = kw_ref[0, pl.ds(w_start, band), g * HEAD_DIM:(g + 1) * HEAD_DIM]
        vw = vw_ref[0, pl.ds(w_start, band), g * HEAD_DIM:(g + 1) * HEAD_DIM]
        o_win = _stacked_softmax_pv(_dot_t(qrs[g], kw), wbias, vw, scale, hg, tq)
        o_slc = oslc_sc[g]
        o_cmp = o_cmps[g]
        for hh in range(hg):
            h = g * hg + hh
            rows = slice(hh * tq, (hh + 1) * tq)
            o = (o_cmp[rows] * gates[:, 3 * h:3 * h + 1]
                 + o_slc[rows] * gates[:, 3 * h + 1:3 * h + 2]
                 + o_win[rows] * gates[:, 3 * h + 2:3 * h + 3])
            o_ref[0, :, h * HEAD_DIM:(h + 1) * HEAD_DIM] = o.astype(o_ref.dtype)


def _nsa(q, main, kvc, planes, cos, sin, m_cs, *, batch, seq, tq=128, key_step=512):
    nq = seq // tq
    kv_w = NSA_GROUPS * HEAD_DIM
    return pl.pallas_call(
        functools.partial(_nsa_kernel, tq=tq, seq=seq, key_step=key_step),
        grid=(batch, nq),
        in_specs=[
            pl.BlockSpec((1, tq, NSA_HEADS * HEAD_DIM), lambda b, i: (b, i, 0)),
            pl.BlockSpec((1, seq, kv_w), lambda b, i: (b, 0, 0)),
            pl.BlockSpec((1, seq, kv_w), lambda b, i: (b, 0, 2)),
            pl.BlockSpec((1, seq, kv_w), lambda b, i: (b, 0, 1)),
            pl.BlockSpec((1, seq, kv_w), lambda b, i: (b, 0, 3)),
            pl.BlockSpec((2 * NSA_GROUPS, 1, LANES, HEAD_DIM), lambda b, i: (0, b, 0, 0)),
            pl.BlockSpec((1, tq, LANES), lambda b, i: (2 * NSA_GROUPS, b * nq + i, 0)),
            pl.BlockSpec((tq, LANES), lambda b, i: (i, 0)),
            pl.BlockSpec((tq, LANES), lambda b, i: (i, 0)),
            pl.BlockSpec(m_cs.shape, lambda b, i: (0, 0)),
        ],
        out_specs=pl.BlockSpec((1, tq, NSA_HEADS * HEAD_DIM), lambda b, i: (b, i, 0)),
        out_shape=jax.ShapeDtypeStruct((batch, seq, NSA_HEADS * HEAD_DIM), BF),
        scratch_shapes=[pltpu.VMEM((NSA_GROUPS, HEADS_PER_GROUP * tq, HEAD_DIM), F32)],
        compiler_params=_params(("parallel", "arbitrary")),
        name="nsa_attention",
    )(q, main, main, main, main, kvc, planes, cos, sin, m_cs)


def _cumdecay_kernel(x_ref, b_ref, o_ref):
    z = x_ref[0] + b_ref[...]
    c = jnp.minimum(z, 0.0) - jnp.log1p(jnp.exp(-jnp.abs(z)))
    rows = c.shape[0]
    r = lax.broadcasted_iota(jnp.int32, (rows, 1), 0)
    k = 1
    while k < rows:
        c = c + jnp.where(r >= k, pltpu.roll(c, k, axis=0), 0.0)
        k *= 2
    o_ref[0] = c


def _cumdecay(planes, bias, *, batch, seq):
    return pl.pallas_call(
        _cumdecay_kernel,
        grid=(batch,),
        in_specs=[pl.BlockSpec((1, seq, LANES), lambda b: (2 * NSA_GROUPS, b, 0)),
                  pl.BlockSpec((1, LANES), lambda b: (0, 0))],
        out_specs=pl.BlockSpec((1, seq, LANES), lambda b: (b, 0, 0)),
        out_shape=jax.ShapeDtypeStruct((batch, seq, LANES), F32),
        compiler_params=_params(("arbitrary",)),
        name="fox_cumdecay",
    )(planes, bias)


def _attn_kernel(*refs, tq, seq, scale, n_qk, decay):
    q_ref = refs[0]
    k_refs = refs[1:1 + n_qk]
    v_ref = refs[1 + n_qk]
    k = 2 + n_qk
    if decay:
        ck_ref = refs[k]
        k += 1
    o_ref = refs[k]

    row = lax.broadcasted_iota(jnp.int32, (tq, 1), 0)
    col = lax.broadcasted_iota(jnp.int32, (1, tq), 1)
    tri = col <= row
    exp2_scale = scale * float(np.log2(np.e))

    for i in range(seq // tq):
        lo, hi = i * tq, (i + 1) * tq
        q = q_ref[0, lo:hi, :]
        s = _dot_t(q[:, :HEAD_DIM], k_refs[0][0, :hi, :])
        for c in range(1, n_qk):
            s = s + _dot_t(q[:, c * HEAD_DIM:(c + 1) * HEAD_DIM], k_refs[c][0, :hi, :])
        if decay:
            ck = ck_ref[0, :, :hi]
            cq = jnp.sum(jnp.where(col == row, ck[:, lo:hi], 0.0), axis=1, keepdims=True)
            s = s * scale + (cq - ck)
            prob = lambda u, m: jnp.exp(u - m)
        else:
            prob = lambda u, m: jnp.exp2((u - m) * exp2_scale)
        s_diag = jnp.where(tri, s[:, lo:hi], NEG)
        m = jnp.max(s_diag, axis=-1, keepdims=True)
        if i > 0:
            m = jnp.maximum(m, jnp.max(s[:, :lo], axis=-1, keepdims=True))
        e_diag = prob(s_diag, m)
        l = jnp.sum(e_diag, axis=-1, keepdims=True)
        e = e_diag.astype(BF)
        if i > 0:
            e_off = prob(s[:, :lo], m)
            l = l + jnp.sum(e_off, axis=-1, keepdims=True)
            e = jnp.concatenate([e_off.astype(BF), e], axis=1)
        o = jnp.dot(e, v_ref[0, :hi, :], preferred_element_type=F32) / l
        o_ref[0, lo:hi, :] = o.astype(o_ref.dtype)


def _attention(q_arr, q_blk0, k_arrs, k_blk0s, k_per_head, v_arr, v_blk0, *, batch, seq, heads, scale,
               decay=None, tq=256, name="attention"):
    n_qk = len(k_arrs)
    in_specs = [pl.BlockSpec((1, seq, n_qk * HEAD_DIM), lambda b, h: (b, 0, q_blk0 + h))]
    args = [q_arr]
    for arr, blk0, per_head in zip(k_arrs, k_blk0s, k_per_head):
        if per_head:
            in_specs.append(pl.BlockSpec((1, seq, HEAD_DIM), lambda b, h, blk0=blk0: (b, 0, blk0 + h)))
        else:
            in_specs.append(pl.BlockSpec((1, seq, HEAD_DIM), lambda b, h, blk0=blk0: (b, 0, blk0)))
        args.append(arr)
    in_specs.append(pl.BlockSpec((1, seq, HEAD_DIM), lambda b, h: (b, 0, v_blk0 + h)))
    args.append(v_arr)
    if decay is not None:
        in_specs.append(pl.BlockSpec((1, 1, seq), lambda b, h: (b * heads + h, 0, 0)))
        args.append(decay)
    return pl.pallas_call(
        functools.partial(_attn_kernel, tq=tq, seq=seq, scale=scale, n_qk=n_qk, decay=decay is not None),
        grid=(batch, heads),
        in_specs=in_specs,
        out_specs=pl.BlockSpec((1, seq, HEAD_DIM), lambda b, h: (b, 0, h)),
        out_shape=jax.ShapeDtypeStruct((batch, seq, heads * HEAD_DIM), BF),
        compiler_params=_params(("parallel", "arbitrary")),
        name=name,
    )(*args)


def _mla_latent_kernel(x_ref, g_ref, w_ref, gq_ref, gkv_ref, cos_ref, sin_ref, cq_ref, ckv_ref, kr_ref):
    h = _rms_normed(x_ref[...], g_ref[...]).astype(BF)
    c = jnp.dot(h, w_ref[...].astype(BF), preferred_element_type=F32)
    cq_ref[...] = _rms_normed(c[:, :MLA_Q_RANK], gq_ref[...]).astype(BF)
    ckv_ref[...] = _rms_normed(c[:, MLA_Q_RANK:MLA_Q_RANK + MLA_KV_RANK], gkv_ref[...]).astype(BF)
    kr_ref[...] = _rope_lanes(c[:, MLA_Q_RANK + MLA_KV_RANK:], cos_ref[...], sin_ref[...]).astype(BF)


def _mla_latent(x, g, w, gq, gkv, cos, sin, *, seq, tm=512):
    n = x.shape[0]
    n_seq = seq // tm
    return pl.pallas_call(
        _mla_latent_kernel,
        grid=(n // tm,),
        in_specs=[
            pl.BlockSpec((tm, x.shape[1]), lambda i: (i, 0)),
            pl.BlockSpec((1, x.shape[1]), lambda i: (0, 0)),
            pl.BlockSpec(w.shape, lambda i: (0, 0)),
            pl.BlockSpec((1, MLA_Q_RANK), lambda i: (0, 0)),
            pl.BlockSpec((1, MLA_KV_RANK), lambda i: (0, 0)),
            pl.BlockSpec((tm, LANES), lambda i: (i % n_seq, 0)),
            pl.BlockSpec((tm, LANES), lambda i: (i % n_seq, 0)),
        ],
        out_specs=[
            pl.BlockSpec((tm, MLA_Q_RANK), lambda i: (i, 0)),
            pl.BlockSpec((tm, MLA_KV_RANK), lambda i: (i, 0)),
            pl.BlockSpec((tm, LANES), lambda i: (i, 0)),
        ],
        out_shape=[
            jax.ShapeDtypeStruct((n, MLA_Q_RANK), BF),
            jax.ShapeDtypeStruct((n, MLA_KV_RANK), BF),
            jax.ShapeDtypeStruct((n, LANES), BF),
        ],
        compiler_params=_params(("parallel",)),
        name="mla_latent",
    )(x, g.reshape(1, -1), w, gq.reshape(1, -1), gkv.reshape(1, -1), cos, sin)


def _rope_tables(seq, dim):
    half = dim // 2
    inv = 1.0 / (ROPE_THETA ** (np.arange(0, dim, 2, dtype=np.float32) / dim))
    ang = np.arange(seq, dtype=np.float32)[:, None] * inv.astype(np.float32)[None, :]
    cos = np.zeros((seq, LANES), np.float32)
    sin = np.zeros((seq, LANES), np.float32)
    cos[:, :half] = np.cos(ang)
    cos[:, LANES // 2:LANES // 2 + half] = np.cos(ang)
    sin[:, :half] = -np.sin(ang)
    sin[:, LANES // 2:LANES // 2 + half] = np.sin(ang)
    return jnp.asarray(cos), jnp.asarray(sin)


def _cmp_to_slc(seq):
    n_cmp = (seq - CMP_LEN) // CMP_STRIDE + 1
    n_slc = seq // SLC_LEN
    c0 = np.arange(n_cmp) * CMP_STRIDE
    c1 = c0 + CMP_LEN
    s0 = np.arange(n_slc) * SLC_LEN
    s1 = s0 + SLC_LEN
    ov = np.clip(np.minimum(c1[:, None], s1[None, :]) - np.maximum(c0[:, None], s0[None, :]), 0, None)
    m = np.zeros((n_slc, LANES), np.float32)
    m[:, :n_cmp] = (ov / CMP_LEN).T
    return jnp.asarray(m)


def _spread_rope_cols(w):
    half = MLA_ROPE // 2
    z = jnp.zeros(w.shape[:-1] + (LANES // 2 - half,), w.dtype)
    return jnp.concatenate([w[..., :half], z, w[..., half:], z], axis=-1)


def _hybrid_mixer(x, g, w_in, w_out, layer, cmp_pe, cmp_w1, cmp_w2, f_bias, *, batch, seq):
    n = batch * seq
    nq_w = NSA_HEADS * HEAD_DIM
    kv_w = NSA_GROUPS * HEAD_DIM
    fx_w = FOX_HEADS * HEAD_DIM
    n_misc = 3 * NSA_HEADS + FOX_HEADS
    c_cmp, c_slc, c_gate = nq_w, nq_w + 2 * kv_w, nq_w + 6 * kv_w
    c_fox = c_gate + 3 * NSA_HEADS
    c_f = c_fox + 3 * fx_w
    w = w_in[layer]
    w_main = jnp.concatenate([w[:, c_slc:c_gate], w[:, c_fox:c_f]], axis=1)
    w_planes = jnp.concatenate([w[:, c_cmp:c_slc], w[:, c_gate:c_fox], w[:, c_f:c_f + FOX_HEADS],
                                jnp.zeros((D_MODEL, LANES - n_misc), F32)], axis=1)

    cos, sin = _rope_tables(seq, HEAD_DIM)
    q_n = _mm([x], [_W(w_in, layer=layer)], nq_w, out_dtype=F32, tm=1024, tn=512, norm_g=g, name="proj_q_nsa")
    main = _mm([x], [_W(w_main)], w_main.shape[1], out_dtype=BF, tm=1024, tn=512, norm_g=g,
               rope=(cos, sin, (0, 2), (True, True, False, False)), seq=seq, name="proj_main")
    planes = _mm([x], [_W(w_planes)], w_planes.shape[1], out_dtype=F32, tm=1024, tn=LANES, norm_g=g,
                 stacked_out=True, name="proj_planes")

    rows16 = seq // CMP_STRIDE
    kvc = _compress(planes.reshape(2 * NSA_GROUPS + 1, batch, rows16, CMP_STRIDE * HEAD_DIM),
                    cmp_pe.reshape(2, 2, CMP_STRIDE * HEAD_DIM), cmp_w1, cmp_w2, batch)

    o_nsa = _nsa(q_n.reshape(batch, seq, nq_w), main.reshape(batch, seq, -1), kvc, planes, cos, sin,
                 _cmp_to_slc(seq), batch=batch, seq=seq)

    bias = jnp.zeros((1, LANES), F32).at[0, 3 * NSA_HEADS:n_misc].set(f_bias)
    cum = _cumdecay(planes, bias, batch=batch, seq=seq)[:, :, 3 * NSA_HEADS:n_misc]
    cum_t = jnp.swapaxes(cum, 1, 2).reshape(batch * FOX_HEADS, seq)
    main3 = main.reshape(batch, seq, -1)
    blk = lambda col: col // HEAD_DIM
    o_fox = _attention(main3, blk(4 * kv_w), [main3], [blk(4 * kv_w + fx_w)], [True], main3, blk(4 * kv_w + 2 * fx_w),
                       batch=batch, seq=seq, heads=FOX_HEADS, scale=HEAD_DIM ** -0.5,
                       decay=cum_t[:, None, :], name="fox_attention")

    return _mm([o_nsa.reshape(n, nq_w), o_fox.reshape(n, fx_w)],
               [_W(w_out, layer=layer, rows=nq_w, row_blk=0), _W(w_out, layer=layer, rows=fx_w, row_blk=1)],
               D_MODEL, out_dtype=F32, tm=1024, tn=512, res=x, name="proj_out_hybrid")


def _mla_mixer(x, g, w_in, q_norm, kv_norm, w_uq, w_ukv, w_out, layer, *, batch, seq):
    n = batch * seq
    qk = MLA_NOPE + MLA_ROPE
    w_lat = jnp.concatenate([w_in[:, :MLA_Q_RANK + MLA_KV_RANK],
                             _spread_rope_cols(w_in[:, MLA_Q_RANK + MLA_KV_RANK:])], axis=1)
    wq = w_uq.reshape(MLA_Q_RANK, MLA_HEADS, qk)
    wq = jnp.concatenate([wq[..., :MLA_NOPE], _spread_rope_cols(wq[..., MLA_NOPE:])], axis=-1)
    wq = wq.reshape(MLA_Q_RANK, MLA_HEADS * 2 * HEAD_DIM)
    wkv = w_ukv.reshape(MLA_KV_RANK, MLA_HEADS, MLA_NOPE + MLA_V)
    wkv = jnp.concatenate([wkv[..., :MLA_NOPE].reshape(MLA_KV_RANK, -1),
                           wkv[..., MLA_NOPE:].reshape(MLA_KV_RANK, -1)], axis=1)

    cos, sin = _rope_tables(seq, MLA_ROPE)
    cq, ckv, kr = _mla_latent(x, g, w_lat, q_norm, kv_norm, cos, sin, seq=seq)
    q = _mm([cq], [_W(wq)], wq.shape[1], out_dtype=BF, tm=1024, tn=512,
            rope=(cos, sin, (0, wq.shape[1] // 512), (False, True) * 2), seq=seq, name="proj_mla_q")
    kv = _mm([ckv], [_W(wkv)], wkv.shape[1], out_dtype=BF, tm=1024, tn=512, name="proj_mla_kv")

    q3 = q.reshape(batch, seq, -1)
    kv3 = kv.reshape(batch, seq, -1)
    kr3 = kr.reshape(batch, seq, LANES)
    o = _attention(q3, 0, [kv3, kr3], [0, 0], [True, False], kv3, MLA_HEADS, batch=batch, seq=seq,
                   heads=MLA_HEADS, scale=qk ** -0.5, name="mla_attention")
    return _mm([o.reshape(n, MLA_HEADS * MLA_V)], [_W(w_out, layer=layer)], D_MODEL, out_dtype=F32, tm=1024, tn=512,
               res=x, name="proj_out_mla")


def _ffn(x, norms, w_in, w_out, layer):
    depth, d = norms.shape
    a = _ffn_up(x, norms.reshape(depth, 1, d), w_in, layer)
    return _mm([a], [_W(w_out, layer=layer)], d, out_dtype=F32, tm=1024, tn=256, res=x, name="ffn_down")


def kernel(x, ffn1_norm, ffn1_w_in, ffn1_w_out, mix_norm, ffn2_norm, ffn2_w_in, ffn2_w_out, hyb_w_in, hyb_w_out, nsa_cmp_pe, nsa_cmp_w1, nsa_cmp_w2, fox_f_bias, mla_w_in, mla_q_norm, mla_kv_norm, mla_w_uq, mla_w_ukv, mla_w_out, final_norm):
    batch, seq, d = x.shape
    depth = ffn1_norm.shape[0]
    x = x.reshape(batch * seq, d)
    for i in range(depth):
        x = _ffn(x, ffn1_norm, ffn1_w_in, ffn1_w_out, i)
        if i % 2 == 0:
            e = i // 2
            x = _hybrid_mixer(x, mix_norm[i], hyb_w_in, hyb_w_out, e, nsa_cmp_pe[e], nsa_cmp_w1[e], nsa_cmp_w2[e],
                              fox_f_bias[e], batch=batch, seq=seq)
        else:
            o = i // 2
            x = _mla_mixer(x, mix_norm[i], mla_w_in[o], mla_q_norm[o], mla_kv_norm[o], mla_w_uq[o], mla_w_ukv[o],
                           mla_w_out, o, batch=batch, seq=seq)
        x = _ffn(x, ffn2_norm, ffn2_w_in, ffn2_w_out, i)
    return _rmsnorm(x, final_norm).reshape(batch, seq, d)
```

```python
import functools

import numpy as np
import jax
import jax.numpy as jnp
from jax import lax
from jax.experimental import pallas as pl
from jax.experimental.pallas import tpu as pltpu

D_MODEL = 2048
HEAD_DIM = 128
ROPE_THETA = 10000.0
NORM_EPS = 1e-6
NEG = -1e30

NSA_HEADS = 8
NSA_GROUPS = 2
HEADS_PER_GROUP = NSA_HEADS // NSA_GROUPS
CMP_LEN = 32
CMP_STRIDE = 16
CMP_HIDDEN = 256
SLC_LEN = 64
SLC_TOP = 8
WIN = 512
FOX_HEADS = 8
MLA_HEADS = 16
MLA_Q_RANK = 512
MLA_KV_RANK = 512
MLA_NOPE = 128
MLA_ROPE = 64
MLA_V = 128
D_FF = 5632

LANES = 128
VMEM_LIMIT = 56 * 1024 * 1024

BF = jnp.bfloat16
F32 = jnp.float32


def _params(sem, vmem=VMEM_LIMIT):
    return pltpu.CompilerParams(dimension_semantics=sem, vmem_limit_bytes=vmem)


def _rope_lanes(x, cos, sin):
    return x * cos + pltpu.roll(x, LANES // 2, axis=1) * sin


def _dot_t(a, b):
    return lax.dot_general(a, b, (((1,), (1,)), ((), ())), preferred_element_type=F32)


def _softmax_rows(s):
    m = jnp.max(s, axis=-1, keepdims=True)
    e = jnp.exp(s - m)
    return e, jnp.sum(e, axis=-1, keepdims=True)


def _rms_normed(x, g):
    ms = jnp.mean(x * x, axis=-1, keepdims=True)
    return x * lax.rsqrt(ms + NORM_EPS) * g


def _ffn_up_kernel(x_ref, g_ref, wa_ref, wb_ref, o_ref, xn_sc):
    @pl.when(pl.program_id(1) == 0)
    def _():
        xn_sc[...] = _rms_normed(x_ref[...], g_ref[...]).astype(BF)

    xn = xn_sc[...]
    h1 = jnp.dot(xn, wa_ref[...].astype(BF), preferred_element_type=F32)
    h2 = jnp.dot(xn, wb_ref[...].astype(BF), preferred_element_type=F32)
    o_ref[...] = (h1 * jax.nn.sigmoid(h1) * h2 * 0.5).astype(BF)


def _ffn_up(x, g, w_in, layer, *, tm=1024, tf=512):
    n, d = x.shape
    n_f = D_FF // tf
    return pl.pallas_call(
        _ffn_up_kernel,
        grid=(n // tm, n_f),
        in_specs=[
            pl.BlockSpec((tm, d), lambda i, j: (i, 0)),
            pl.BlockSpec((None, 1, d), lambda i, j: (layer, 0, 0)),
            pl.BlockSpec((None, d, tf), lambda i, j: (layer, 0, j)),
            pl.BlockSpec((None, d, tf), lambda i, j: (layer, 0, n_f + j)),
        ],
        out_specs=pl.BlockSpec((tm, tf), lambda i, j: (i, j)),
        out_shape=jax.ShapeDtypeStruct((n, D_FF), BF),
        scratch_shapes=[pltpu.VMEM((tm, d), BF)],
        compiler_params=_params(("parallel", "arbitrary")),
        name="ffn_up",
    )(x, g, w_in, w_in)


def _norm_kernel(x_ref, g_ref, o_ref):
    o_ref[...] = _rms_normed(x_ref[...], g_ref[...]).astype(o_ref.dtype)


def _rmsnorm(x, g, *, tm=512):
    n, d = x.shape
    return pl.pallas_call(
        _norm_kernel,
        grid=(n // tm,),
        in_specs=[pl.BlockSpec((tm, d), lambda i: (i, 0)), pl.BlockSpec((1, d), lambda i: (0, 0))],
        out_specs=pl.BlockSpec((tm, d), lambda i: (i, 0)),
        out_shape=jax.ShapeDtypeStruct((n, d), x.dtype),
        compiler_params=_params(("parallel",)),
        name="rmsnorm",
    )(x, g.reshape(1, d))


class _W:
    def __init__(self, arr, *, layer=None, rows=None, row_blk=0, col_blk0=0):
        self.arr, self.layer, self.row_blk, self.col_blk0 = arr, layer, row_blk, col_blk0
        self.rows = rows if rows is not None else arr.shape[-2]

    def spec(self, tn, single_buffer=False):
        layer, row_blk, col_blk0 = self.layer, self.row_blk, self.col_blk0
        kw = dict(pipeline_mode=pl.Buffered(1)) if single_buffer else {}
        if self.arr.ndim == 3:
            return pl.BlockSpec((None, self.rows, tn), lambda i, j: (layer, row_blk, col_blk0 + j), **kw)
        return pl.BlockSpec((self.rows, tn), lambda i, j: (row_blk, col_blk0 + j), **kw)


def _mm_kernel(*refs, n_a, has_norm, has_res, rope_j, rope_chunks):
    a_refs = refs[:n_a]
    w_refs = refs[n_a:2 * n_a]
    k = 2 * n_a
    res_ref = None
    if has_norm:
        g_ref = refs[k]
        k += 1
    if has_res:
        res_ref = refs[k]
        k += 1
    if rope_chunks is not None:
        cos_ref, sin_ref = refs[k], refs[k + 1]
        k += 2
    o_ref = refs[k]

    if has_norm:
        xn_sc = refs[k + 1]

        @pl.when(pl.program_id(1) == 0)
        def _():
            xn_sc[...] = _rms_normed(a_refs[0][...], g_ref[...]).astype(BF)

        lhs = [xn_sc[...]]
    else:
        lhs = [a_ref[...] for a_ref in a_refs]

    acc = jnp.dot(lhs[0], w_refs[0][...].astype(BF), preferred_element_type=F32)
    for a, w_ref in zip(lhs[1:], w_refs[1:]):
        acc += jnp.dot(a, w_ref[...].astype(BF), preferred_element_type=F32)
    if has_res:
        acc += res_ref[...]

    def store(v):
        o_ref[...] = v.reshape(o_ref.shape).astype(o_ref.dtype)

    if rope_chunks is None:
        store(acc)
    else:
        j = pl.program_id(1)
        is_rope = (j >= rope_j[0]) & (j < rope_j[1])

        @pl.when(is_rope)
        def _():
            cos, sin = cos_ref[...], sin_ref[...]
            parts = []
            for c, on in enumerate(rope_chunks):
                blk = acc[:, c * LANES:(c + 1) * LANES]
                parts.append(_rope_lanes(blk, cos, sin) if on else blk)
            store(jnp.concatenate(parts, axis=1))

        @pl.when(jnp.logical_not(is_rope))
        def _():
            store(acc)


def _mm(a_list, w_list, n_out, *, out_dtype, tm, tn, norm_g=None, res=None, rope=None, seq=None,
        stacked_out=False, name="mm"):
    n = a_list[0].shape[0]
    n_a = len(a_list)
    in_specs, args, scratch = [], [], []
    for a in a_list:
        in_specs.append(pl.BlockSpec((tm, a.shape[1]), lambda i, j: (i, 0)))
        args.append(a)
    for w in w_list:
        in_specs.append(w.spec(tn, single_buffer=tn == n_out))
        args.append(w.arr)
    if norm_g is not None:
        assert n_a == 1
        in_specs.append(pl.BlockSpec((1, norm_g.shape[-1]), lambda i, j: (0, 0)))
        args.append(norm_g.reshape(1, -1))
        scratch.append(pltpu.VMEM((tm, a_list[0].shape[1]), BF))
    if res is not None:
        in_specs.append(pl.BlockSpec((tm, tn), lambda i, j: (i, j)))
        args.append(res)
    rope_j = rope_chunks = None
    if rope is not None:
        cos, sin, rope_j, rope_chunks = rope
        n_seq = seq // tm
        for t in (cos, sin):
            in_specs.append(pl.BlockSpec((tm, LANES), lambda i, j: (i % n_seq, 0)))
            args.append(t)
    if stacked_out:
        out_shape = jax.ShapeDtypeStruct((n_out // tn, n, tn), out_dtype)
        out_spec = pl.BlockSpec((1, tm, tn), lambda i, j: (j, i, 0))
    else:
        out_shape = jax.ShapeDtypeStruct((n, n_out), out_dtype)
        out_spec = pl.BlockSpec((tm, tn), lambda i, j: (i, j))
    return pl.pallas_call(
        functools.partial(_mm_kernel, n_a=n_a, has_norm=norm_g is not None, has_res=res is not None,
                          rope_j=rope_j, rope_chunks=rope_chunks),
        grid=(n // tm, n_out // tn),
        in_specs=in_specs,
        out_specs=out_spec,
        out_shape=out_shape,
        scratch_shapes=scratch,
        compiler_params=_params(("parallel", "arbitrary")),
        name=name,
    )(*args)


def _compress_kernel(x_ref, pe_ref, w1_ref, w2_ref, o_ref):
    x = x_ref[0, 0]
    half = CMP_STRIDE * HEAD_DIM
    xa = (x + pe_ref[0, 0:1, :]).astype(BF)
    xb = (x + pe_ref[0, 1:2, :]).astype(BF)
    ha = jnp.dot(xa, w1_ref[0, :half, :].astype(BF), preferred_element_type=F32)
    hb = jnp.dot(xb, w1_ref[0, half:, :].astype(BF), preferred_element_type=F32)
    hid = ha + pltpu.roll(hb, hb.shape[0] - 1, axis=0)
    act = jax.nn.gelu(hid, approximate=True).astype(BF)
    o_ref[0, 0] = jnp.dot(act, w2_ref[0].astype(BF), preferred_element_type=F32)


def _compress(xc, pe, w1, w2, batch):
    rows = xc.shape[2]
    return pl.pallas_call(
        _compress_kernel,
        grid=(2 * NSA_GROUPS, batch),
        in_specs=[
            pl.BlockSpec((1, 1, rows, CMP_STRIDE * HEAD_DIM), lambda c, b: (c, b, 0, 0)),
            pl.BlockSpec((1, 2, CMP_STRIDE * HEAD_DIM), lambda c, b: (c // NSA_GROUPS, 0, 0)),
            pl.BlockSpec((1, CMP_LEN * HEAD_DIM, CMP_HIDDEN), lambda c, b: (c // NSA_GROUPS, 0, 0)),
            pl.BlockSpec((1, CMP_HIDDEN, HEAD_DIM), lambda c, b: (c // NSA_GROUPS, 0, 0)),
        ],
        out_specs=pl.BlockSpec((1, 1, rows, HEAD_DIM), lambda c, b: (c, b, 0, 0)),
        out_shape=jax.ShapeDtypeStruct((2 * NSA_GROUPS, batch, rows, HEAD_DIM), F32),
        compiler_params=_params(("arbitrary", "arbitrary")),
        name="nsa_compress",
    )(xc, pe, w1, w2)


def _stacked_softmax_pv(s_raw, bias, v, scale, n_stack, tq):
    exp2_scale = scale * float(np.log2(np.e))
    es, ls = [], []
    for h in range(n_stack):
        u = s_raw[h * tq:(h + 1) * tq] + bias
        e = jnp.exp2((u - jnp.max(u, axis=-1, keepdims=True)) * exp2_scale)
        ls.append(jnp.sum(e, axis=-1, keepdims=True))
        es.append(e.astype(BF))
    o = jnp.dot(jnp.concatenate(es, axis=0), v, preferred_element_type=F32)
    return o / jnp.concatenate(ls, axis=0)


def _nsa_kernel(q_ref, ks_ref, kw_ref, vs_ref, vw_ref, kvc_ref, gate_ref, cos_ref, sin_ref, m_ref, o_ref,
                oslc_sc, *, tq, seq, key_step):
    qi = pl.program_id(1)
    hg = HEADS_PER_GROUP
    scale = HEAD_DIM ** -0.5
    n_cmp = (seq - CMP_LEN) // CMP_STRIDE + 1
    n_slc = seq // SLC_LEN
    band = WIN + tq
    neg_bf = jnp.asarray(NEG, BF)

    t_row = qi * tq + lax.broadcasted_iota(jnp.int32, (tq, 1), 0)
    cos = jnp.concatenate([cos_ref[...]] * hg, axis=0)
    sin = jnp.concatenate([sin_ref[...]] * hg, axis=0)
    gates = jax.nn.sigmoid(gate_ref[0])

    n_lane = lax.broadcasted_iota(jnp.int32, (1, LANES), 1)
    cmask = (n_lane * CMP_STRIDE + (CMP_LEN - 1) <= t_row) & (n_lane < n_cmp)
    cbias = jnp.where(cmask, 0.0, NEG)
    j_blk = lax.broadcasted_iota(jnp.int32, (n_slc, 1), 0)
    slc_shift = SLC_LEN.bit_length() - 1
    cur = jnp.right_shift(qi * tq + lax.broadcasted_iota(jnp.int32, (1, tq), 1), slc_shift)
    forced = (j_blk == 0) | (j_blk == cur) | (j_blk == cur - 1)
    w_start = pl.multiple_of(jnp.maximum(qi * tq - WIN, 0), tq)
    w_pos = w_start + lax.broadcasted_iota(jnp.int32, (1, band), 1)
    wbias = jnp.where((w_pos <= t_row) & (w_pos > t_row - WIN), 0.0, NEG)

    qrs, sel_negs, o_cmps = [], [], []
    for g in range(NSA_GROUPS):
        qg = q_ref[0, :, g * hg * HEAD_DIM:(g + 1) * hg * HEAD_DIM].astype(F32)
        qp = jnp.concatenate([qg[:, h * HEAD_DIM:(h + 1) * HEAD_DIM] for h in range(hg)], axis=0)
        qrs.append(_rope_lanes(qp, cos, sin).astype(BF))
        qp = qp.astype(BF)

        kc = kvc_ref[g, 0].astype(BF)
        vc = kvc_ref[NSA_GROUPS + g, 0].astype(BF)
        s = _dot_t(qp, kc)
        ps = []
        for h in range(hg):
            u = s[h * tq:(h + 1) * tq] * scale + cbias
            e, l = _softmax_rows(u)
            ps.append(jnp.where(cmask, e / l, 0.0))
        o_cmps.append(jnp.dot(jnp.concatenate(ps, axis=0).astype(BF), vc, preferred_element_type=F32))

        p_sum = ps[0]
        for h in range(1, hg):
            p_sum = p_sum + ps[h]
        imp = lax.dot_general(m_ref[...], p_sum, (((1,), (1,)), ((), ())), preferred_element_type=F32,
                              precision=lax.Precision.HIGHEST)
        imp = jnp.where(j_blk > cur, -jnp.inf, jnp.where(forced, jnp.inf, imp))
        rank = jnp.zeros((n_slc, tq), jnp.int32)
        for i in range(n_slc):
            ci = imp[i:i + 1, :]
            beats = (ci > imp) | ((ci == imp) & (j_blk > i))
            rank = rank + beats.astype(jnp.int32)
        sel_negs.append(jnp.where(rank < SLC_TOP, 0.0, NEG).astype(BF))

    def slc_branch(width):
        blocks = width // SLC_LEN
        expand = jnp.where(jnp.right_shift(lax.broadcasted_iota(jnp.int32, (blocks, width), 1), slc_shift)
                           == lax.broadcasted_iota(jnp.int32, (blocks, width), 0), 1.0, 0.0).astype(BF)
        s_pos = lax.broadcasted_iota(jnp.int32, (1, width), 1)
        for g in range(NSA_GROUPS):
            bias = lax.dot_general(sel_negs[g][:blocks], expand, (((0,), (0,)), ((), ())),
                                   preferred_element_type=F32)
            bias = jnp.where(s_pos <= t_row, bias, NEG)
            ks = ks_ref[0, :width, g * HEAD_DIM:(g + 1) * HEAD_DIM]
            vs = vs_ref[0, :width, g * HEAD_DIM:(g + 1) * HEAD_DIM]
            oslc_sc[g] = _stacked_softmax_pv(_dot_t(qrs[g], ks), bias, vs, scale, hg, tq)

    n_widths = seq // key_step
    super_row = qi // (key_step // tq)
    for k in range(n_widths):
        pl.when(super_row == k)(functools.partial(slc_branch, (k + 1) * key_step))

    for g in range(NSA_GROUPS):
        kw = kw_ref[0, pl.ds(w_start, band), g * HEAD_DIM:(g + 1) * HEAD_DIM]
        vw = vw_ref[0, pl.ds(w_start, band), g * HEAD_DIM:(g + 1) * HEAD_DIM]
        o_win = _stacked_softmax_pv(_dot_t(qrs[g], kw), wbias, vw, scale, hg, tq)
        o_slc = oslc_sc[g]
        o_cmp = o_cmps[g]
        for hh in range(hg):
            h = g * hg + hh
            rows = slice(hh * tq, (hh + 1) * tq)
            o = (o_cmp[rows] * gates[:, 3 * h:3 * h + 1]
                 + o_slc[rows] * gates[:, 3 * h + 1:3 * h + 2]
                 + o_win[rows] * gates[:, 3 * h + 2:3 * h + 3])
            o_ref[0, :, h * HEAD_DIM:(h + 1) * HEAD_DIM] = o.astype(o_ref.dtype)


def _nsa(main, kvc, planes, cos, sin, m_cs, *, batch, seq, tq=128, key_step=512):
    nq = seq // tq
    kv_w = NSA_GROUPS * HEAD_DIM
    kv0 = NSA_HEADS * HEAD_DIM // kv_w
    return pl.pallas_call(
        functools.partial(_nsa_kernel, tq=tq, seq=seq, key_step=key_step),
        grid=(batch, nq),
        in_specs=[
            pl.BlockSpec((1, tq, NSA_HEADS * HEAD_DIM), lambda b, i: (b, i, 0)),
            pl.BlockSpec((1, seq, kv_w), lambda b, i: (b, 0, kv0)),
            pl.BlockSpec((1, seq, kv_w), lambda b, i: (b, 0, kv0 + 2)),
            pl.BlockSpec((1, seq, kv_w), lambda b, i: (b, 0, kv0 + 1)),
            pl.BlockSpec((1, seq, kv_w), lambda b, i: (b, 0, kv0 + 3)),
            pl.BlockSpec((2 * NSA_GROUPS, 1, LANES, HEAD_DIM), lambda b, i: (0, b, 0, 0)),
            pl.BlockSpec((1, tq, LANES), lambda b, i: (2 * NSA_GROUPS, b * nq + i, 0)),
            pl.BlockSpec((tq, LANES), lambda b, i: (i, 0)),
            pl.BlockSpec((tq, LANES), lambda b, i: (i, 0)),
            pl.BlockSpec(m_cs.shape, lambda b, i: (0, 0)),
        ],
        out_specs=pl.BlockSpec((1, tq, NSA_HEADS * HEAD_DIM), lambda b, i: (b, i, 0)),
        out_shape=jax.ShapeDtypeStruct((batch, seq, NSA_HEADS * HEAD_DIM), BF),
        scratch_shapes=[pltpu.VMEM((NSA_GROUPS, HEADS_PER_GROUP * tq, HEAD_DIM), F32)],
        compiler_params=_params(("parallel", "arbitrary")),
        name="nsa_attention",
    )(main, main, main, main, main, kvc, planes, cos, sin, m_cs)


def _cumdecay_kernel(x_ref, b_ref, o_ref):
    z = x_ref[0] + b_ref[...]
    c = jnp.minimum(z, 0.0) - jnp.log1p(jnp.exp(-jnp.abs(z)))
    rows = c.shape[0]
    r = lax.broadcasted_iota(jnp.int32, (rows, 1), 0)
    k = 1
    while k < rows:
        c = c + jnp.where(r >= k, pltpu.roll(c, k, axis=0), 0.0)
        k *= 2
    o_ref[0] = c


def _cumdecay(planes, bias, *, batch, seq):
    return pl.pallas_call(
        _cumdecay_kernel,
        grid=(batch,),
        in_specs=[pl.BlockSpec((1, seq, LANES), lambda b: (2 * NSA_GROUPS, b, 0)),
                  pl.BlockSpec((1, LANES), lambda b: (0, 0))],
        out_specs=pl.BlockSpec((1, seq, LANES), lambda b: (b, 0, 0)),
        out_shape=jax.ShapeDtypeStruct((batch, seq, LANES), F32),
        compiler_params=_params(("arbitrary",)),
        name="fox_cumdecay",
    )(planes, bias)


def _attn_kernel(*refs, tq, seq, scale, n_qk, decay):
    q_ref = refs[0]
    k_refs = refs[1:1 + n_qk]
    v_ref = refs[1 + n_qk]
    k = 2 + n_qk
    if decay:
        ck_ref = refs[k]
        k += 1
    o_ref = refs[k]

    row = lax.broadcasted_iota(jnp.int32, (tq, 1), 0)
    col = lax.broadcasted_iota(jnp.int32, (1, tq), 1)
    tri = col <= row
    exp2_scale = scale * float(np.log2(np.e))

    k_all = k_refs[0][0] if n_qk == 1 else jnp.concatenate([k_ref[0] for k_ref in k_refs], axis=1)

    for i in range(seq // tq):
        lo, hi = i * tq, (i + 1) * tq
        s = _dot_t(q_ref[0, lo:hi, :], k_all[:hi])
        if decay:
            ck = ck_ref[0, :, :hi]
            cq = jnp.sum(jnp.where(col == row, ck[:, lo:hi], 0.0), axis=1, keepdims=True)
            s = s * scale + (cq - ck)
            prob = lambda u, m: jnp.exp(u - m)
        else:
            prob = lambda u, m: jnp.exp2((u - m) * exp2_scale)
        s_diag = jnp.where(tri, s[:, lo:hi], NEG)
        m = jnp.max(s_diag, axis=-1, keepdims=True)
        if i > 0:
            m = jnp.maximum(m, jnp.max(s[:, :lo], axis=-1, keepdims=True))
        e_diag = prob(s_diag, m)
        l = jnp.sum(e_diag, axis=-1, keepdims=True)
        e = e_diag.astype(BF)
        if i > 0:
            e_off = prob(s[:, :lo], m)
            l = l + jnp.sum(e_off, axis=-1, keepdims=True)
            e = jnp.concatenate([e_off.astype(BF), e], axis=1)
        o = jnp.dot(e, v_ref[0, :hi, :], preferred_element_type=F32) / l
        o_ref[0, lo:hi, :] = o.astype(o_ref.dtype)


def _attention(q_arr, q_blk0, k_arrs, k_blk0s, k_per_head, v_arr, v_blk0, *, batch, seq, heads, scale,
               decay=None, tq=256, name="attention"):
    n_qk = len(k_arrs)
    in_specs = [pl.BlockSpec((1, seq, n_qk * HEAD_DIM), lambda b, h: (b, 0, q_blk0 + h))]
    args = [q_arr]
    for arr, blk0, per_head in zip(k_arrs, k_blk0s, k_per_head):
        if per_head:
            in_specs.append(pl.BlockSpec((1, seq, HEAD_DIM), lambda b, h, blk0=blk0: (b, 0, blk0 + h)))
        else:
            in_specs.append(pl.BlockSpec((1, seq, HEAD_DIM), lambda b, h, blk0=blk0: (b, 0, blk0)))
        args.append(arr)
    in_specs.append(pl.BlockSpec((1, seq, HEAD_DIM), lambda b, h: (b, 0, v_blk0 + h)))
    args.append(v_arr)
    if decay is not None:
        in_specs.append(pl.BlockSpec((1, 1, seq), lambda b, h: (b * heads + h, 0, 0)))
        args.append(decay)
    return pl.pallas_call(
        functools.partial(_attn_kernel, tq=tq, seq=seq, scale=scale, n_qk=n_qk, decay=decay is not None),
        grid=(batch, heads),
        in_specs=in_specs,
        out_specs=pl.BlockSpec((1, seq, HEAD_DIM), lambda b, h: (b, 0, h)),
        out_shape=jax.ShapeDtypeStruct((batch, seq, heads * HEAD_DIM), BF),
        compiler_params=_params(("parallel", "arbitrary")),
        name=name,
    )(*args)


def _mla_latent_kernel(x_ref, g_ref, w_ref, gq_ref, gkv_ref, cos_ref, sin_ref, cq_ref, ckv_ref, kr_ref):
    h = _rms_normed(x_ref[...], g_ref[...]).astype(BF)
    c = jnp.dot(h, w_ref[...].astype(BF), preferred_element_type=F32)
    cq_ref[...] = _rms_normed(c[:, :MLA_Q_RANK], gq_ref[...]).astype(BF)
    ckv_ref[...] = _rms_normed(c[:, MLA_Q_RANK:MLA_Q_RANK + MLA_KV_RANK], gkv_ref[...]).astype(BF)
    kr_ref[...] = _rope_lanes(c[:, MLA_Q_RANK + MLA_KV_RANK:], cos_ref[...], sin_ref[...]).astype(BF)


def _mla_latent(x, g, w, gq, gkv, cos, sin, *, seq, tm=512):
    n = x.shape[0]
    n_seq = seq // tm
    return pl.pallas_call(
        _mla_latent_kernel,
        grid=(n // tm,),
        in_specs=[
            pl.BlockSpec((tm, x.shape[1]), lambda i: (i, 0)),
            pl.BlockSpec((1, x.shape[1]), lambda i: (0, 0)),
            pl.BlockSpec(w.shape, lambda i: (0, 0)),
            pl.BlockSpec((1, MLA_Q_RANK), lambda i: (0, 0)),
            pl.BlockSpec((1, MLA_KV_RANK), lambda i: (0, 0)),
            pl.BlockSpec((tm, LANES), lambda i: (i % n_seq, 0)),
            pl.BlockSpec((tm, LANES), lambda i: (i % n_seq, 0)),
        ],
        out_specs=[
            pl.BlockSpec((tm, MLA_Q_RANK), lambda i: (i, 0)),
            pl.BlockSpec((tm, MLA_KV_RANK), lambda i: (i, 0)),
            pl.BlockSpec((tm, LANES), lambda i: (i, 0)),
        ],
        out_shape=[
            jax.ShapeDtypeStruct((n, MLA_Q_RANK), BF),
            jax.ShapeDtypeStruct((n, MLA_KV_RANK), BF),
            jax.ShapeDtypeStruct((n, LANES), BF),
        ],
        compiler_params=_params(("parallel",)),
        name="mla_latent",
    )(x, g.reshape(1, -1), w, gq.reshape(1, -1), gkv.reshape(1, -1), cos, sin)


def _rope_tables(seq, dim):
    half = dim // 2
    inv = 1.0 / (ROPE_THETA ** (np.arange(0, dim, 2, dtype=np.float32) / dim))
    ang = np.arange(seq, dtype=np.float32)[:, None] * inv.astype(np.float32)[None, :]
    cos = np.zeros((seq, LANES), np.float32)
    sin = np.zeros((seq, LANES), np.float32)
    cos[:, :half] = np.cos(ang)
    cos[:, LANES // 2:LANES // 2 + half] = np.cos(ang)
    sin[:, :half] = -np.sin(ang)
    sin[:, LANES // 2:LANES // 2 + half] = np.sin(ang)
    return jnp.asarray(cos), jnp.asarray(sin)


def _cmp_to_slc(seq):
    n_cmp = (seq - CMP_LEN) // CMP_STRIDE + 1
    n_slc = seq // SLC_LEN
    c0 = np.arange(n_cmp) * CMP_STRIDE
    c1 = c0 + CMP_LEN
    s0 = np.arange(n_slc) * SLC_LEN
    s1 = s0 + SLC_LEN
    ov = np.clip(np.minimum(c1[:, None], s1[None, :]) - np.maximum(c0[:, None], s0[None, :]), 0, None)
    m = np.zeros((n_slc, LANES), np.float32)
    m[:, :n_cmp] = (ov / CMP_LEN).T
    return jnp.asarray(m)


def _spread_rope_cols(w):
    half = MLA_ROPE // 2
    z = jnp.zeros(w.shape[:-1] + (LANES // 2 - half,), w.dtype)
    return jnp.concatenate([w[..., :half], z, w[..., half:], z], axis=-1)


def _hybrid_mixer(x, g, w_in, w_out, layer, cmp_pe, cmp_w1, cmp_w2, f_bias, *, batch, seq):
    n = batch * seq
    nq_w = NSA_HEADS * HEAD_DIM
    kv_w = NSA_GROUPS * HEAD_DIM
    fx_w = FOX_HEADS * HEAD_DIM
    n_misc = 3 * NSA_HEADS + FOX_HEADS
    c_cmp, c_slc, c_gate = nq_w, nq_w + 2 * kv_w, nq_w + 6 * kv_w
    c_fox = c_gate + 3 * NSA_HEADS
    c_f = c_fox + 3 * fx_w
    w = w_in[layer]
    w_main = jnp.concatenate([w[:, :c_cmp], w[:, c_slc:c_gate], w[:, c_fox:c_f]], axis=1)
    w_planes = jnp.concatenate([w[:, c_cmp:c_slc], w[:, c_gate:c_fox], w[:, c_f:c_f + FOX_HEADS],
                                jnp.zeros((D_MODEL, LANES - n_misc), F32)], axis=1)

    cos, sin = _rope_tables(seq, HEAD_DIM)
    tn = 512
    main = _mm([x], [_W(w_main)], w_main.shape[1], out_dtype=BF, tm=1024, tn=tn, norm_g=g,
               rope=(cos, sin, (nq_w // tn, nq_w // tn + 2), (True, True, False, False)), seq=seq, name="proj_main")
    planes = _mm([x], [_W(w_planes)], w_planes.shape[1], out_dtype=F32, tm=1024, tn=LANES, norm_g=g,
                 stacked_out=True, name="proj_planes")

    rows16 = seq // CMP_STRIDE
    kvc = _compress(planes.reshape(2 * NSA_GROUPS + 1, batch, rows16, CMP_STRIDE * HEAD_DIM),
                    cmp_pe.reshape(2, 2, CMP_STRIDE * HEAD_DIM), cmp_w1, cmp_w2, batch)

    main3 = main.reshape(batch, seq, -1)
    o_nsa = _nsa(main3, kvc, planes, cos, sin, _cmp_to_slc(seq), batch=batch, seq=seq)

    bias = jnp.zeros((1, LANES), F32).at[0, 3 * NSA_HEADS:n_misc].set(f_bias)
    cum = _cumdecay(planes, bias, batch=batch, seq=seq)[:, :, 3 * NSA_HEADS:n_misc]
    cum_t = jnp.swapaxes(cum, 1, 2).reshape(batch * FOX_HEADS, seq)
    blk = lambda col: col // HEAD_DIM
    c_qf = nq_w + 4 * kv_w
    o_fox = _attention(main3, blk(c_qf), [main3], [blk(c_qf + fx_w)], [True], main3, blk(c_qf + 2 * fx_w),
                       batch=batch, seq=seq, heads=FOX_HEADS, scale=HEAD_DIM ** -0.5,
                       decay=cum_t[:, None, :], name="fox_attention")

    return _mm([o_nsa.reshape(n, nq_w), o_fox.reshape(n, fx_w)],
               [_W(w_out, layer=layer, rows=nq_w, row_blk=0), _W(w_out, layer=layer, rows=fx_w, row_blk=1)],
               D_MODEL, out_dtype=F32, tm=512, tn=D_MODEL, res=x, name="proj_out_hybrid")


def _mla_mixer(x, g, w_in, q_norm, kv_norm, w_uq, w_ukv, w_out, layer, *, batch, seq):
    n = batch * seq
    qk = MLA_NOPE + MLA_ROPE
    w_lat = jnp.concatenate([w_in[:, :MLA_Q_RANK + MLA_KV_RANK],
                             _spread_rope_cols(w_in[:, MLA_Q_RANK + MLA_KV_RANK:])], axis=1)
    wq = w_uq.reshape(MLA_Q_RANK, MLA_HEADS, qk)
    wq = jnp.concatenate([wq[..., :MLA_NOPE], _spread_rope_cols(wq[..., MLA_NOPE:])], axis=-1)
    wq = wq.reshape(MLA_Q_RANK, MLA_HEADS * 2 * HEAD_DIM)
    wkv = w_ukv.reshape(MLA_KV_RANK, MLA_HEADS, MLA_NOPE + MLA_V)
    wkv = jnp.concatenate([wkv[..., :MLA_NOPE].reshape(MLA_KV_RANK, -1),
                           wkv[..., MLA_NOPE:].reshape(MLA_KV_RANK, -1)], axis=1)

    cos, sin = _rope_tables(seq, MLA_ROPE)
    cq, ckv, kr = _mla_latent(x, g, w_lat, q_norm, kv_norm, cos, sin, seq=seq)
    tn = 1024
    q = _mm([cq], [_W(wq)], wq.shape[1], out_dtype=BF, tm=1024, tn=tn,
            rope=(cos, sin, (0, wq.shape[1] // tn), (False, True) * (tn // (2 * LANES))), seq=seq, name="proj_mla_q")
    kv = _mm([ckv], [_W(wkv)], wkv.shape[1], out_dtype=BF, tm=1024, tn=tn, name="proj_mla_kv")

    q3 = q.reshape(batch, seq, -1)
    kv3 = kv.reshape(batch, seq, -1)
    kr3 = kr.reshape(batch, seq, LANES)
    o = _attention(q3, 0, [kv3, kr3], [0, 0], [True, False], kv3, MLA_HEADS, batch=batch, seq=seq,
                   heads=MLA_HEADS, scale=qk ** -0.5, name="mla_attention")
    return _mm([o.reshape(n, MLA_HEADS * MLA_V)], [_W(w_out, layer=layer)], D_MODEL, out_dtype=F32, tm=512, tn=D_MODEL,
               res=x, name="proj_out_mla")


def _ffn(x, norms, w_in, w_out, layer):
    depth, d = norms.shape
    a = _ffn_up(x, norms.reshape(depth, 1, d), w_in, layer)
    return _mm([a], [_W(w_out, layer=layer)], d, out_dtype=F32, tm=1024, tn=256, res=x, name="ffn_down")


def kernel(x, ffn1_norm, ffn1_w_in, ffn1_w_out, mix_norm, ffn2_norm, ffn2_w_in, ffn2_w_out, hyb_w_in, hyb_w_out, nsa_cmp_pe, nsa_cmp_w1, nsa_cmp_w2, fox_f_bias, mla_w_in, mla_q_norm, mla_kv_norm, mla_w_uq, mla_w_ukv, mla_w_out, final_norm):
    batch, seq, d = x.shape
    depth = ffn1_norm.shape[0]
    x = x.reshape(batch * seq, d)
    for i in range(depth):
        x = _ffn(x, ffn1_norm, ffn1_w_in, ffn1_w_out, i)
        if i % 2 == 0:
            e = i // 2
            x = _hybrid_mixer(x, mix_norm[i], hyb_w_in, hyb_w_out, e, nsa_cmp_pe[e], nsa_cmp_w1[e], nsa_cmp_w2[e],
                              fox_f_bias[e], batch=batch, seq=seq)
        else:
            o = i // 2
            x = _mla_mixer(x, mix_norm[i], mla_w_in[o], mla_q_norm[o], mla_kv_norm[o], mla_w_uq[o], mla_w_ukv[o],
                           mla_w_out, o, batch=batch, seq=seq)
        x = _ffn(x, ffn2_norm, ffn2_w_in, ffn2_w_out, i)
    return _rmsnorm(x, final_norm).reshape(batch, seq, d)
```

```python
import functools

import numpy as np
import jax
import jax.numpy as jnp
from jax import lax
from jax.experimental import pallas as pl
from jax.experimental.pallas import tpu as pltpu

D_MODEL = 2048
HEAD_DIM = 128
ROPE_THETA = 10000.0
NORM_EPS = 1e-6
NEG = -1e30

NSA_HEADS = 8
NSA_GROUPS = 2
HEADS_PER_GROUP = NSA_HEADS // NSA_GROUPS
CMP_LEN = 32
CMP_STRIDE = 16
CMP_HIDDEN = 256
SLC_LEN = 64
SLC_TOP = 8
WIN = 512
FOX_HEADS = 8
MLA_HEADS = 16
MLA_Q_RANK = 512
MLA_KV_RANK = 512
MLA_NOPE = 128
MLA_ROPE = 64
MLA_V = 128
D_FF = 5632

LANES = 128
VMEM_LIMIT = 56 * 1024 * 1024

BF = jnp.bfloat16
F32 = jnp.float32


def _params(sem, vmem=VMEM_LIMIT):
    return pltpu.CompilerParams(dimension_semantics=sem, vmem_limit_bytes=vmem)


def _rope_lanes(x, cos, sin):
    return x * cos + pltpu.roll(x, LANES // 2, axis=1) * sin


def _dot_t(a, b):
    return lax.dot_general(a, b, (((1,), (1,)), ((), ())), preferred_element_type=F32)


def _softmax_rows(s):
    m = jnp.max(s, axis=-1, keepdims=True)
    e = jnp.exp(s - m)
    return e, jnp.sum(e, axis=-1, keepdims=True)


def _rms_normed(x, g):
    ms = jnp.mean(x * x, axis=-1, keepdims=True)
    return x * lax.rsqrt(ms + NORM_EPS) * g


def _ffn_up_kernel(x_ref, g_ref, wa_ref, wb_ref, o_ref, xn_sc):
    @pl.when(pl.program_id(1) == 0)
    def _():
        xn_sc[...] = _rms_normed(x_ref[...], g_ref[...]).astype(BF)

    xn = xn_sc[...]
    h1 = jnp.dot(xn, wa_ref[...].astype(BF), preferred_element_type=F32)
    h2 = jnp.dot(xn, wb_ref[...].astype(BF), preferred_element_type=F32)
    o_ref[...] = (h1 * jax.nn.sigmoid(h1) * h2 * 0.5).astype(BF)


def _ffn_up(x, g, w_in, layer, *, tm=1024, tf=512):
    n, d = x.shape
    n_f = D_FF // tf
    return pl.pallas_call(
        _ffn_up_kernel,
        grid=(n // tm, n_f),
        in_specs=[
            pl.BlockSpec((tm, d), lambda i, j: (i, 0)),
            pl.BlockSpec((None, 1, d), lambda i, j: (layer, 0, 0)),
            pl.BlockSpec((None, d, tf), lambda i, j: (layer, 0, j)),
            pl.BlockSpec((None, d, tf), lambda i, j: (layer, 0, n_f + j)),
        ],
        out_specs=pl.BlockSpec((tm, tf), lambda i, j: (i, j)),
        out_shape=jax.ShapeDtypeStruct((n, D_FF), BF),
        scratch_shapes=[pltpu.VMEM((tm, d), BF)],
        compiler_params=_params(("parallel", "arbitrary")),
        name="ffn_up",
    )(x, g, w_in, w_in)


def _norm_kernel(x_ref, g_ref, o_ref):
    o_ref[...] = _rms_normed(x_ref[...], g_ref[...]).astype(o_ref.dtype)


def _rmsnorm(x, g, *, tm=512):
    n, d = x.shape
    return pl.pallas_call(
        _norm_kernel,
        grid=(n // tm,),
        in_specs=[pl.BlockSpec((tm, d), lambda i: (i, 0)), pl.BlockSpec((1, d), lambda i: (0, 0))],
        out_specs=pl.BlockSpec((tm, d), lambda i: (i, 0)),
        out_shape=jax.ShapeDtypeStruct((n, d), x.dtype),
        compiler_params=_params(("parallel",)),
        name="rmsnorm",
    )(x, g.reshape(1, d))


class _W:
    def __init__(self, arr, *, layer=None, rows=None, row_blk=0, col_blk0=0):
        self.arr, self.layer, self.row_blk, self.col_blk0 = arr, layer, row_blk, col_blk0
        self.rows = rows if rows is not None else arr.shape[-2]

    def spec(self, tn, single_buffer=False):
        layer, row_blk, col_blk0 = self.layer, self.row_blk, self.col_blk0
        kw = dict(pipeline_mode=pl.Buffered(1)) if single_buffer else {}
        if self.arr.ndim == 3:
            return pl.BlockSpec((None, self.rows, tn), lambda i, j: (layer, row_blk, col_blk0 + j), **kw)
        return pl.BlockSpec((self.rows, tn), lambda i, j: (row_blk, col_blk0 + j), **kw)


def _mm_kernel(*refs, n_a, has_norm, has_res, rope_j, rope_chunks):
    a_refs = refs[:n_a]
    w_refs = refs[n_a:2 * n_a]
    k = 2 * n_a
    res_ref = None
    if has_norm:
        g_ref = refs[k]
        k += 1
    if has_res:
        res_ref = refs[k]
        k += 1
    if rope_chunks is not None:
        cos_ref, sin_ref = refs[k], refs[k + 1]
        k += 2
    o_ref = refs[k]

    if has_norm:
        xn_sc = refs[k + 1]

        @pl.when(pl.program_id(1) == 0)
        def _():
            xn_sc[...] = _rms_normed(a_refs[0][...], g_ref[...]).astype(BF)

        lhs = [xn_sc[...]]
    else:
        lhs = [a_ref[...] for a_ref in a_refs]

    acc = jnp.dot(lhs[0], w_refs[0][...].astype(BF), preferred_element_type=F32)
    for a, w_ref in zip(lhs[1:], w_refs[1:]):
        acc += jnp.dot(a, w_ref[...].astype(BF), preferred_element_type=F32)
    if has_res:
        acc += res_ref[...]

    def store(v):
        o_ref[...] = v.reshape(o_ref.shape).astype(o_ref.dtype)

    if rope_chunks is None:
        store(acc)
    else:
        j = pl.program_id(1)
        is_rope = (j >= rope_j[0]) & (j < rope_j[1])

        @pl.when(is_rope)
        def _():
            cos, sin = cos_ref[...], sin_ref[...]
            parts = []
            for c, on in enumerate(rope_chunks):
                blk = acc[:, c * LANES:(c + 1) * LANES]
                parts.append(_rope_lanes(blk, cos, sin) if on else blk)
            store(jnp.concatenate(parts, axis=1))

        @pl.when(jnp.logical_not(is_rope))
        def _():
            store(acc)


def _mm(a_list, w_list, n_out, *, out_dtype, tm, tn, norm_g=None, res=None, rope=None, seq=None,
        stacked_out=False, name="mm"):
    n = a_list[0].shape[0]
    n_a = len(a_list)
    in_specs, args, scratch = [], [], []
    for a in a_list:
        in_specs.append(pl.BlockSpec((tm, a.shape[1]), lambda i, j: (i, 0)))
        args.append(a)
    for w in w_list:
        in_specs.append(w.spec(tn, single_buffer=tn == n_out))
        args.append(w.arr)
    if norm_g is not None:
        assert n_a == 1
        in_specs.append(pl.BlockSpec((1, norm_g.shape[-1]), lambda i, j: (0, 0)))
        args.append(norm_g.reshape(1, -1))
        scratch.append(pltpu.VMEM((tm, a_list[0].shape[1]), BF))
    if res is not None:
        in_specs.append(pl.BlockSpec((tm, tn), lambda i, j: (i, j)))
        args.append(res)
    rope_j = rope_chunks = None
    if rope is not None:
        cos, sin, rope_j, rope_chunks = rope
        n_seq = seq // tm
        for t in (cos, sin):
            in_specs.append(pl.BlockSpec((tm, LANES), lambda i, j: (i % n_seq, 0)))
            args.append(t)
    if stacked_out:
        out_shape = jax.ShapeDtypeStruct((n_out // tn, n, tn), out_dtype)
        out_spec = pl.BlockSpec((1, tm, tn), lambda i, j: (j, i, 0))
    else:
        out_shape = jax.ShapeDtypeStruct((n, n_out), out_dtype)
        out_spec = pl.BlockSpec((tm, tn), lambda i, j: (i, j))
    return pl.pallas_call(
        functools.partial(_mm_kernel, n_a=n_a, has_norm=norm_g is not None, has_res=res is not None,
                          rope_j=rope_j, rope_chunks=rope_chunks),
        grid=(n // tm, n_out // tn),
        in_specs=in_specs,
        out_specs=out_spec,
        out_shape=out_shape,
        scratch_shapes=scratch,
        compiler_params=_params(("parallel", "arbitrary")),
        name=name,
    )(*args)


def _compress_kernel(x_ref, pe_ref, w1_ref, w2_ref, o_ref):
    x = x_ref[0, 0]
    half = CMP_STRIDE * HEAD_DIM
    xa = (x + pe_ref[0, 0:1, :]).astype(BF)
    xb = (x + pe_ref[0, 1:2, :]).astype(BF)
    ha = jnp.dot(xa, w1_ref[0, :half, :].astype(BF), preferred_element_type=F32)
    hb = jnp.dot(xb, w1_ref[0, half:, :].astype(BF), preferred_element_type=F32)
    hid = ha + pltpu.roll(hb, hb.shape[0] - 1, axis=0)
    act = jax.nn.gelu(hid, approximate=True).astype(BF)
    o_ref[0, 0] = jnp.dot(act, w2_ref[0].astype(BF), preferred_element_type=F32)


def _compress(xc, pe, w1, w2, batch):
    rows = xc.shape[2]
    return pl.pallas_call(
        _compress_kernel,
        grid=(2 * NSA_GROUPS, batch),
        in_specs=[
            pl.BlockSpec((1, 1, rows, CMP_STRIDE * HEAD_DIM), lambda c, b: (c, b, 0, 0)),
            pl.BlockSpec((1, 2, CMP_STRIDE * HEAD_DIM), lambda c, b: (c // NSA_GROUPS, 0, 0)),
            pl.BlockSpec((1, CMP_LEN * HEAD_DIM, CMP_HIDDEN), lambda c, b: (c // NSA_GROUPS, 0, 0)),
            pl.BlockSpec((1, CMP_HIDDEN, HEAD_DIM), lambda c, b: (c // NSA_GROUPS, 0, 0)),
        ],
        out_specs=pl.BlockSpec((1, 1, rows, HEAD_DIM), lambda c, b: (c, b, 0, 0)),
        out_shape=jax.ShapeDtypeStruct((2 * NSA_GROUPS, batch, rows, HEAD_DIM), F32),
        compiler_params=_params(("arbitrary", "arbitrary")),
        name="nsa_compress",
    )(xc, pe, w1, w2)


def _stacked_softmax_pv(s_raw, bias, v, scale, n_stack, tq):
    exp2_scale = scale * float(np.log2(np.e))
    es, ls = [], []
    for h in range(n_stack):
        u = s_raw[h * tq:(h + 1) * tq] + bias
        e = jnp.exp2((u - jnp.max(u, axis=-1, keepdims=True)) * exp2_scale)
        ls.append(jnp.sum(e, axis=-1, keepdims=True))
        es.append(e.astype(BF))
    o = jnp.dot(jnp.concatenate(es, axis=0), v, preferred_element_type=F32)
    return o / jnp.concatenate(ls, axis=0)


def _nsa_kernel(q_ref, ks_ref, kw_ref, vs_ref, vw_ref, kvc_ref, gate_ref, cos_ref, sin_ref, m_ref, o_ref,
                *, tq, seq, key_step):
    qi = pl.program_id(1)
    hg = HEADS_PER_GROUP
    scale = HEAD_DIM ** -0.5
    n_cmp = (seq - CMP_LEN) // CMP_STRIDE + 1
    n_slc = seq // SLC_LEN
    band = WIN + tq
    neg_bf = jnp.asarray(NEG, BF)

    t_row = qi * tq + lax.broadcasted_iota(jnp.int32, (tq, 1), 0)
    cos = jnp.concatenate([cos_ref[...]] * hg, axis=0)
    sin = jnp.concatenate([sin_ref[...]] * hg, axis=0)
    gates = jax.nn.sigmoid(gate_ref[0])

    n_lane = lax.broadcasted_iota(jnp.int32, (1, LANES), 1)
    cmask = (n_lane * CMP_STRIDE + (CMP_LEN - 1) <= t_row) & (n_lane < n_cmp)
    cbias = jnp.where(cmask, 0.0, NEG)
    j_blk = lax.broadcasted_iota(jnp.int32, (n_slc, 1), 0)
    slc_shift = SLC_LEN.bit_length() - 1
    cur = jnp.right_shift(qi * tq + lax.broadcasted_iota(jnp.int32, (1, tq), 1), slc_shift)
    forced = (j_blk == 0) | (j_blk == cur) | (j_blk == cur - 1)
    w_start = pl.multiple_of(jnp.maximum(qi * tq - WIN, 0), tq)
    w_pos = w_start + lax.broadcasted_iota(jnp.int32, (1, band), 1)
    wbias = jnp.where((w_pos <= t_row) & (w_pos > t_row - WIN), 0.0, NEG)

    def group_queries(g):
        qg = q_ref[0, :, g * hg * HEAD_DIM:(g + 1) * hg * HEAD_DIM].astype(F32)
        qp = jnp.concatenate([qg[:, h * HEAD_DIM:(h + 1) * HEAD_DIM] for h in range(hg)], axis=0)
        return qp.astype(BF), _rope_lanes(qp, cos, sin).astype(BF)

    def compressed_branch(g, qp):
        kc = kvc_ref[g, 0].astype(BF)
        vc = kvc_ref[NSA_GROUPS + g, 0].astype(BF)
        s = _dot_t(qp, kc)
        ps = []
        for h in range(hg):
            u = s[h * tq:(h + 1) * tq] * scale + cbias
            e, l = _softmax_rows(u)
            ps.append(jnp.where(cmask, e / l, 0.0))
        o_cmp = jnp.dot(jnp.concatenate(ps, axis=0).astype(BF), vc, preferred_element_type=F32)

        p_sum = ps[0]
        for h in range(1, hg):
            p_sum = p_sum + ps[h]
        imp = lax.dot_general(m_ref[...], p_sum, (((1,), (1,)), ((), ())), preferred_element_type=F32,
                              precision=lax.Precision.HIGHEST)
        imp = jnp.where(j_blk > cur, -jnp.inf, jnp.where(forced, jnp.inf, imp))
        rank = jnp.zeros((n_slc, tq), jnp.int32)
        for i in range(n_slc):
            ci = imp[i:i + 1, :]
            beats = (ci > imp) | ((ci == imp) & (j_blk > i))
            rank = rank + beats.astype(jnp.int32)
        return o_cmp, jnp.where(rank < SLC_TOP, 0.0, NEG).astype(BF)

    qs = [group_queries(g) for g in range(NSA_GROUPS)]
    cmps = [compressed_branch(g, qs[g][0]) for g in range(NSA_GROUPS)]

    def attend(width):
        blocks = width // SLC_LEN
        expand = jnp.where(jnp.right_shift(lax.broadcasted_iota(jnp.int32, (blocks, width), 1), slc_shift)
                           == lax.broadcasted_iota(jnp.int32, (blocks, width), 0), 1.0, 0.0).astype(BF)
        s_pos = lax.broadcasted_iota(jnp.int32, (1, width), 1)
        lanes = [slice(g * HEAD_DIM, (g + 1) * HEAD_DIM) for g in range(NSA_GROUPS)]
        s_win = [_dot_t(qs[g][1], kw_ref[0, pl.ds(w_start, band), lanes[g]]) for g in range(NSA_GROUPS)]
        s_slc = [_dot_t(qs[g][1], ks_ref[0, :width, lanes[g]]) for g in range(NSA_GROUPS)]
        for g in range(NSA_GROUPS):
            o_win = _stacked_softmax_pv(s_win[g], wbias, vw_ref[0, pl.ds(w_start, band), lanes[g]], scale, hg, tq)
            o_cmp, sel_neg = cmps[g]
            bias = lax.dot_general(sel_neg[:blocks], expand, (((0,), (0,)), ((), ())),
                                   preferred_element_type=F32)
            bias = jnp.where(s_pos <= t_row, bias, NEG)
            o_slc = _stacked_softmax_pv(s_slc[g], bias, vs_ref[0, :width, lanes[g]], scale, hg, tq)
            for hh in range(hg):
                h = g * hg + hh
                rows = slice(hh * tq, (hh + 1) * tq)
                o = (o_cmp[rows] * gates[:, 3 * h:3 * h + 1]
                     + o_slc[rows] * gates[:, 3 * h + 1:3 * h + 2]
                     + o_win[rows] * gates[:, 3 * h + 2:3 * h + 3])
                o_ref[0, :, h * HEAD_DIM:(h + 1) * HEAD_DIM] = o.astype(o_ref.dtype)

    n_widths = seq // key_step
    super_row = qi // (key_step // tq)
    for k in range(n_widths):
        pl.when(super_row == k)(functools.partial(attend, (k + 1) * key_step))


def _nsa(main, kvc, planes, cos, sin, m_cs, *, batch, seq, tq=128, key_step=512):
    nq = seq // tq
    kv_w = NSA_GROUPS * HEAD_DIM
    kv0 = NSA_HEADS * HEAD_DIM // kv_w
    return pl.pallas_call(
        functools.partial(_nsa_kernel, tq=tq, seq=seq, key_step=key_step),
        grid=(batch, nq),
        in_specs=[
            pl.BlockSpec((1, tq, NSA_HEADS * HEAD_DIM), lambda b, i: (b, i, 0)),
            pl.BlockSpec((1, seq, kv_w), lambda b, i: (b, 0, kv0)),
            pl.BlockSpec((1, seq, kv_w), lambda b, i: (b, 0, kv0 + 2)),
            pl.BlockSpec((1, seq, kv_w), lambda b, i: (b, 0, kv0 + 1)),
            pl.BlockSpec((1, seq, kv_w), lambda b, i: (b, 0, kv0 + 3)),
            pl.BlockSpec((2 * NSA_GROUPS, 1, LANES, HEAD_DIM), lambda b, i: (0, b, 0, 0)),
            pl.BlockSpec((1, tq, LANES), lambda b, i: (2 * NSA_GROUPS, b * nq + i, 0)),
            pl.BlockSpec((tq, LANES), lambda b, i: (i, 0)),
            pl.BlockSpec((tq, LANES), lambda b, i: (i, 0)),
            pl.BlockSpec(m_cs.shape, lambda b, i: (0, 0)),
        ],
        out_specs=pl.BlockSpec((1, tq, NSA_HEADS * HEAD_DIM), lambda b, i: (b, i, 0)),
        out_shape=jax.ShapeDtypeStruct((batch, seq, NSA_HEADS * HEAD_DIM), BF),
        compiler_params=_params(("parallel", "arbitrary")),
        name="nsa_attention",
    )(main, main, main, main, main, kvc, planes, cos, sin, m_cs)


def _cumdecay_kernel(x_ref, b_ref, o_ref):
    z = x_ref[0] + b_ref[...]
    c = jnp.minimum(z, 0.0) - jnp.log1p(jnp.exp(-jnp.abs(z)))
    rows = c.shape[0]
    r = lax.broadcasted_iota(jnp.int32, (rows, 1), 0)
    k = 1
    while k < rows:
        c = c + jnp.where(r >= k, pltpu.roll(c, k, axis=0), 0.0)
        k *= 2
    o_ref[0] = c


def _cumdecay(planes, bias, *, batch, seq):
    return pl.pallas_call(
        _cumdecay_kernel,
        grid=(batch,),
        in_specs=[pl.BlockSpec((1, seq, LANES), lambda b: (2 * NSA_GROUPS, b, 0)),
                  pl.BlockSpec((1, LANES), lambda b: (0, 0))],
        out_specs=pl.BlockSpec((1, seq, LANES), lambda b: (b, 0, 0)),
        out_shape=jax.ShapeDtypeStruct((batch, seq, LANES), F32),
        compiler_params=_params(("arbitrary",)),
        name="fox_cumdecay",
    )(planes, bias)


def _attn_kernel(*refs, tq, seq, scale, n_qk, decay):
    q_ref = refs[0]
    k_refs = refs[1:1 + n_qk]
    v_ref = refs[1 + n_qk]
    k = 2 + n_qk
    if decay:
        ck_ref = refs[k]
        k += 1
    o_ref = refs[k]

    row = lax.broadcasted_iota(jnp.int32, (tq, 1), 0)
    col = lax.broadcasted_iota(jnp.int32, (1, tq), 1)
    tri = col <= row
    exp2_scale = scale * float(np.log2(np.e))

    k_all = k_refs[0][0] if n_qk == 1 else jnp.concatenate([k_ref[0] for k_ref in k_refs], axis=1)

    n_tiles = seq // tq
    scores = lambda i: _dot_t(q_ref[0, i * tq:(i + 1) * tq, :], k_all[:(i + 1) * tq])
    s_next = scores(n_tiles - 1)
    for i in reversed(range(n_tiles)):
        lo, hi = i * tq, (i + 1) * tq
        s = s_next
        if i > 0:
            s_next = scores(i - 1)
        if decay:
            ck = ck_ref[0, :, :hi]
            cq = jnp.sum(jnp.where(col == row, ck[:, lo:hi], 0.0), axis=1, keepdims=True)
            s = s * scale + (cq - ck)
            prob = lambda u, m: jnp.exp(u - m)
        else:
            prob = lambda u, m: jnp.exp2((u - m) * exp2_scale)
        s_diag = jnp.where(tri, s[:, lo:hi], NEG)
        m = jnp.max(s_diag, axis=-1, keepdims=True)
        if i > 0:
            m = jnp.maximum(m, jnp.max(s[:, :lo], axis=-1, keepdims=True))
        e_diag = prob(s_diag, m)
        l = jnp.sum(e_diag, axis=-1, keepdims=True)
        e = e_diag.astype(BF)
        if i > 0:
            e_off = prob(s[:, :lo], m)
            l = l + jnp.sum(e_off, axis=-1, keepdims=True)
            e = jnp.concatenate([e_off.astype(BF), e], axis=1)
        o = jnp.dot(e, v_ref[0, :hi, :], preferred_element_type=F32) / l
        o_ref[0, lo:hi, :] = o.astype(o_ref.dtype)


def _attention(q_arr, q_blk0, k_arrs, k_blk0s, k_per_head, v_arr, v_blk0, *, batch, seq, heads, scale,
               decay=None, tq=256, name="attention"):
    n_qk = len(k_arrs)
    in_specs = [pl.BlockSpec((1, seq, n_qk * HEAD_DIM), lambda b, h: (b, 0, q_blk0 + h))]
    args = [q_arr]
    for arr, blk0, per_head in zip(k_arrs, k_blk0s, k_per_head):
        if per_head:
            in_specs.append(pl.BlockSpec((1, seq, HEAD_DIM), lambda b, h, blk0=blk0: (b, 0, blk0 + h)))
        else:
            in_specs.append(pl.BlockSpec((1, seq, HEAD_DIM), lambda b, h, blk0=blk0: (b, 0, blk0)))
        args.append(arr)
    in_specs.append(pl.BlockSpec((1, seq, HEAD_DIM), lambda b, h: (b, 0, v_blk0 + h)))
    args.append(v_arr)
    if decay is not None:
        in_specs.append(pl.BlockSpec((1, 1, seq), lambda b, h: (b * heads + h, 0, 0)))
        args.append(decay)
    return pl.pallas_call(
        functools.partial(_attn_kernel, tq=tq, seq=seq, scale=scale, n_qk=n_qk, decay=decay is not None),
        grid=(batch, heads),
        in_specs=in_specs,
        out_specs=pl.BlockSpec((1, seq, HEAD_DIM), lambda b, h: (b, 0, h)),
        out_shape=jax.ShapeDtypeStruct((batch, seq, heads * HEAD_DIM), BF),
        compiler_params=_params(("parallel", "arbitrary")),
        name=name,
    )(*args)


def _mla_latent_kernel(x_ref, g_ref, w_ref, gq_ref, gkv_ref, cos_ref, sin_ref, cq_ref, ckv_ref, kr_ref):
    h = _rms_normed(x_ref[...], g_ref[...]).astype(BF)
    c = jnp.dot(h, w_ref[...].astype(BF), preferred_element_type=F32)
    cq_ref[...] = _rms_normed(c[:, :MLA_Q_RANK], gq_ref[...]).astype(BF)
    ckv_ref[...] = _rms_normed(c[:, MLA_Q_RANK:MLA_Q_RANK + MLA_KV_RANK], gkv_ref[...]).astype(BF)
    kr_ref[...] = _rope_lanes(c[:, MLA_Q_RANK + MLA_KV_RANK:], cos_ref[...], sin_ref[...]).astype(BF)


def _mla_latent(x, g, w, gq, gkv, cos, sin, *, seq, tm=512):
    n = x.shape[0]
    n_seq = seq // tm
    return pl.pallas_call(
        _mla_latent_kernel,
        grid=(n // tm,),
        in_specs=[
            pl.BlockSpec((tm, x.shape[1]), lambda i: (i, 0)),
            pl.BlockSpec((1, x.shape[1]), lambda i: (0, 0)),
            pl.BlockSpec(w.shape, lambda i: (0, 0)),
            pl.BlockSpec((1, MLA_Q_RANK), lambda i: (0, 0)),
            pl.BlockSpec((1, MLA_KV_RANK), lambda i: (0, 0)),
            pl.BlockSpec((tm, LANES), lambda i: (i % n_seq, 0)),
            pl.BlockSpec((tm, LANES), lambda i: (i % n_seq, 0)),
        ],
        out_specs=[
            pl.BlockSpec((tm, MLA_Q_RANK), lambda i: (i, 0)),
            pl.BlockSpec((tm, MLA_KV_RANK), lambda i: (i, 0)),
            pl.BlockSpec((tm, LANES), lambda i: (i, 0)),
        ],
        out_shape=[
            jax.ShapeDtypeStruct((n, MLA_Q_RANK), BF),
            jax.ShapeDtypeStruct((n, MLA_KV_RANK), BF),
            jax.ShapeDtypeStruct((n, LANES), BF),
        ],
        compiler_params=_params(("parallel",)),
        name="mla_latent",
    )(x, g.reshape(1, -1), w, gq.reshape(1, -1), gkv.reshape(1, -1), cos, sin)


def _rope_tables(seq, dim):
    half = dim // 2
    inv = 1.0 / (ROPE_THETA ** (np.arange(0, dim, 2, dtype=np.float32) / dim))
    ang = np.arange(seq, dtype=np.float32)[:, None] * inv.astype(np.float32)[None, :]
    cos = np.zeros((seq, LANES), np.float32)
    sin = np.zeros((seq, LANES), np.float32)
    cos[:, :half] = np.cos(ang)
    cos[:, LANES // 2:LANES // 2 + half] = np.cos(ang)
    sin[:, :half] = -np.sin(ang)
    sin[:, LANES // 2:LANES // 2 + half] = np.sin(ang)
    return jnp.asarray(cos), jnp.asarray(sin)


def _cmp_to_slc(seq):
    n_cmp = (seq - CMP_LEN) // CMP_STRIDE + 1
    n_slc = seq // SLC_LEN
    c0 = np.arange(n_cmp) * CMP_STRIDE
    c1 = c0 + CMP_LEN
    s0 = np.arange(n_slc) * SLC_LEN
    s1 = s0 + SLC_LEN
    ov = np.clip(np.minimum(c1[:, None], s1[None, :]) - np.maximum(c0[:, None], s0[None, :]), 0, None)
    m = np.zeros((n_slc, LANES), np.float32)
    m[:, :n_cmp] = (ov / CMP_LEN).T
    return jnp.asarray(m)


def _spread_rope_cols(w):
    half = MLA_ROPE // 2
    z = jnp.zeros(w.shape[:-1] + (LANES // 2 - half,), w.dtype)
    return jnp.concatenate([w[..., :half], z, w[..., half:], z], axis=-1)


def _hybrid_mixer(x, g, w_in, w_out, layer, cmp_pe, cmp_w1, cmp_w2, f_bias, *, batch, seq):
    n = batch * seq
    nq_w = NSA_HEADS * HEAD_DIM
    kv_w = NSA_GROUPS * HEAD_DIM
    fx_w = FOX_HEADS * HEAD_DIM
    n_misc = 3 * NSA_HEADS + FOX_HEADS
    c_cmp, c_slc, c_gate = nq_w, nq_w + 2 * kv_w, nq_w + 6 * kv_w
    c_fox = c_gate + 3 * NSA_HEADS
    c_f = c_fox + 3 * fx_w
    w = w_in[layer]
    w_main = jnp.concatenate([w[:, :c_cmp], w[:, c_slc:c_gate], w[:, c_fox:c_f]], axis=1)
    w_planes = jnp.concatenate([w[:, c_cmp:c_slc], w[:, c_gate:c_fox], w[:, c_f:c_f + FOX_HEADS],
                                jnp.zeros((D_MODEL, LANES - n_misc), F32)], axis=1)

    cos, sin = _rope_tables(seq, HEAD_DIM)
    tn = 4 * kv_w
    main = _mm([x], [_W(w_main)], w_main.shape[1], out_dtype=BF, tm=1024, tn=tn, norm_g=g,
               rope=(cos, sin, (nq_w // tn, nq_w // tn + 1), (True, True, False, False) * 2), seq=seq,
               name="proj_main")
    planes = _mm([x], [_W(w_planes)], w_planes.shape[1], out_dtype=F32, tm=1024, tn=LANES, norm_g=g,
                 stacked_out=True, name="proj_planes")

    rows16 = seq // CMP_STRIDE
    kvc = _compress(planes.reshape(2 * NSA_GROUPS + 1, batch, rows16, CMP_STRIDE * HEAD_DIM),
                    cmp_pe.reshape(2, 2, CMP_STRIDE * HEAD_DIM), cmp_w1, cmp_w2, batch)

    main3 = main.reshape(batch, seq, -1)
    o_nsa = _nsa(main3, kvc, planes, cos, sin, _cmp_to_slc(seq), batch=batch, seq=seq)

    bias = jnp.zeros((1, LANES), F32).at[0, 3 * NSA_HEADS:n_misc].set(f_bias)
    cum = _cumdecay(planes, bias, batch=batch, seq=seq)[:, :, 3 * NSA_HEADS:n_misc]
    cum_t = jnp.swapaxes(cum, 1, 2).reshape(batch * FOX_HEADS, seq)
    blk = lambda col: col // HEAD_DIM
    c_qf = nq_w + 4 * kv_w
    o_fox = _attention(main3, blk(c_qf), [main3], [blk(c_qf + fx_w)], [True], main3, blk(c_qf + 2 * fx_w),
                       batch=batch, seq=seq, heads=FOX_HEADS, scale=HEAD_DIM ** -0.5,
                       decay=cum_t[:, None, :], name="fox_attention")

    return _mm([o_nsa.reshape(n, nq_w), o_fox.reshape(n, fx_w)],
               [_W(w_out, layer=layer, rows=nq_w, row_blk=0), _W(w_out, layer=layer, rows=fx_w, row_blk=1)],
               D_MODEL, out_dtype=F32, tm=512, tn=D_MODEL, res=x, name="proj_out_hybrid")


def _mla_mixer(x, g, w_in, q_norm, kv_norm, w_uq, w_ukv, w_out, layer, *, batch, seq):
    n = batch * seq
    qk = MLA_NOPE + MLA_ROPE
    w_lat = jnp.concatenate([w_in[:, :MLA_Q_RANK + MLA_KV_RANK],
                             _spread_rope_cols(w_in[:, MLA_Q_RANK + MLA_KV_RANK:])], axis=1)
    wq = w_uq.reshape(MLA_Q_RANK, MLA_HEADS, qk)
    wq = jnp.concatenate([wq[..., :MLA_NOPE], _spread_rope_cols(wq[..., MLA_NOPE:])], axis=-1)
    wq = wq.reshape(MLA_Q_RANK, MLA_HEADS * 2 * HEAD_DIM)
    wkv = w_ukv.reshape(MLA_KV_RANK, MLA_HEADS, MLA_NOPE + MLA_V)
    wkv = jnp.concatenate([wkv[..., :MLA_NOPE].reshape(MLA_KV_RANK, -1),
                           wkv[..., MLA_NOPE:].reshape(MLA_KV_RANK, -1)], axis=1)

    cos, sin = _rope_tables(seq, MLA_ROPE)
    cq, ckv, kr = _mla_latent(x, g, w_lat, q_norm, kv_norm, cos, sin, seq=seq)
    tn = 1024
    q = _mm([cq], [_W(wq)], wq.shape[1], out_dtype=BF, tm=1024, tn=tn,
            rope=(cos, sin, (0, wq.shape[1] // tn), (False, True) * (tn // (2 * LANES))), seq=seq, name="proj_mla_q")
    kv = _mm([ckv], [_W(wkv)], wkv.shape[1], out_dtype=BF, tm=1024, tn=tn, name="proj_mla_kv")

    q3 = q.reshape(batch, seq, -1)
    kv3 = kv.reshape(batch, seq, -1)
    kr3 = kr.reshape(batch, seq, LANES)
    o = _attention(q3, 0, [kv3, kr3], [0, 0], [True, False], kv3, MLA_HEADS, batch=batch, seq=seq,
                   heads=MLA_HEADS, scale=qk ** -0.5, name="mla_attention")
    return _mm([o.reshape(n, MLA_HEADS * MLA_V)], [_W(w_out, layer=layer)], D_MODEL, out_dtype=F32, tm=512, tn=D_MODEL,
               res=x, name="proj_out_mla")


def _ffn(x, norms, w_in, w_out, layer):
    depth, d = norms.shape
    a = _ffn_up(x, norms.reshape(depth, 1, d), w_in, layer)
    return _mm([a], [_W(w_out, layer=layer)], d, out_dtype=F32, tm=1024, tn=256, res=x, name="ffn_down")


def kernel(x, ffn1_norm, ffn1_w_in, ffn1_w_out, mix_norm, ffn2_norm, ffn2_w_in, ffn2_w_out, hyb_w_in, hyb_w_out, nsa_cmp_pe, nsa_cmp_w1, nsa_cmp_w2, fox_f_bias, mla_w_in, mla_q_norm, mla_kv_norm, mla_w_uq, mla_w_ukv, mla_w_out, final_norm):
    batch, seq, d = x.shape
    depth = ffn1_norm.shape[0]
    x = x.reshape(batch * seq, d)
    for i in range(depth):
        x = _ffn(x, ffn1_norm, ffn1_w_in, ffn1_w_out, i)
        if i % 2 == 0:
            e = i // 2
            x = _hybrid_mixer(x, mix_norm[i], hyb_w_in, hyb_w_out, e, nsa_cmp_pe[e], nsa_cmp_w1[e], nsa_cmp_w2[e],
                              fox_f_bias[e], batch=batch, seq=seq)
        else:
            o = i // 2
            x = _mla_mixer(x, mix_norm[i], mla_w_in[o], mla_q_norm[o], mla_kv_norm[o], mla_w_uq[o], mla_w_ukv[o],
                           mla_w_out, o, batch=batch, seq=seq)
        x = _ffn(x, ffn2_norm, ffn2_w_in, ffn2_w_out, i)
    return _rmsnorm(x, final_norm).reshape(batch, seq, d)
```

```python
import functools

import numpy as np
import jax
import jax.numpy as jnp
from jax import lax
from jax.experimental import pallas as pl
from jax.experimental.pallas import tpu as pltpu

D_MODEL = 2048
HEAD_DIM = 128
ROPE_THETA = 10000.0
NORM_EPS = 1e-6
NEG = -1e30

NSA_HEADS = 8
NSA_GROUPS = 2
HEADS_PER_GROUP = NSA_HEADS // NSA_GROUPS
CMP_LEN = 32
CMP_STRIDE = 16
CMP_HIDDEN = 256
SLC_LEN = 64
SLC_TOP = 8
WIN = 512
FOX_HEADS = 8
MLA_HEADS = 16
MLA_Q_RANK = 512
MLA_KV_RANK = 512
MLA_NOPE = 128
MLA_ROPE = 64
MLA_V = 128
D_FF = 5632

LANES = 128
VMEM_LIMIT = 56 * 1024 * 1024

BF = jnp.bfloat16
F32 = jnp.float32


def _params(sem, vmem=VMEM_LIMIT):
    return pltpu.CompilerParams(dimension_semantics=sem, vmem_limit_bytes=vmem)


def _rope_lanes(x, cos, sin):
    return x * cos + pltpu.roll(x, LANES // 2, axis=1) * sin


def _dot_t(a, b):
    return lax.dot_general(a, b, (((1,), (1,)), ((), ())), preferred_element_type=F32)


def _softmax_rows(s):
    m = jnp.max(s, axis=-1, keepdims=True)
    e = jnp.exp(s - m)
    return e, jnp.sum(e, axis=-1, keepdims=True)


def _rms_normed(x, g):
    ms = jnp.mean(x * x, axis=-1, keepdims=True)
    return x * lax.rsqrt(ms + NORM_EPS) * g


def _ffn_up_kernel(x_ref, g_ref, wa_ref, wb_ref, o_ref, xn_sc):
    @pl.when(pl.program_id(1) == 0)
    def _():
        xn_sc[...] = _rms_normed(x_ref[...], g_ref[...]).astype(BF)

    xn = xn_sc[...]
    h1 = jnp.dot(xn, wa_ref[...].astype(BF), preferred_element_type=F32)
    h2 = jnp.dot(xn, wb_ref[...].astype(BF), preferred_element_type=F32)
    o_ref[...] = (h1 * jax.nn.sigmoid(h1) * h2 * 0.5).astype(BF)


def _ffn_up(x, g, w_in, layer, *, tm=1024, tf=512):
    n, d = x.shape
    n_f = D_FF // tf
    return pl.pallas_call(
        _ffn_up_kernel,
        grid=(n // tm, n_f),
        in_specs=[
            pl.BlockSpec((tm, d), lambda i, j: (i, 0)),
            pl.BlockSpec((None, 1, d), lambda i, j: (layer, 0, 0)),
            pl.BlockSpec((None, d, tf), lambda i, j: (layer, 0, j)),
            pl.BlockSpec((None, d, tf), lambda i, j: (layer, 0, n_f + j)),
        ],
        out_specs=pl.BlockSpec((tm, tf), lambda i, j: (i, j)),
        out_shape=jax.ShapeDtypeStruct((n, D_FF), BF),
        scratch_shapes=[pltpu.VMEM((tm, d), BF)],
        compiler_params=_params(("parallel", "arbitrary")),
        name="ffn_up",
    )(x, g, w_in, w_in)


def _norm_kernel(x_ref, g_ref, o_ref):
    o_ref[...] = _rms_normed(x_ref[...], g_ref[...]).astype(o_ref.dtype)


def _rmsnorm(x, g, *, tm=512):
    n, d = x.shape
    return pl.pallas_call(
        _norm_kernel,
        grid=(n // tm,),
        in_specs=[pl.BlockSpec((tm, d), lambda i: (i, 0)), pl.BlockSpec((1, d), lambda i: (0, 0))],
        out_specs=pl.BlockSpec((tm, d), lambda i: (i, 0)),
        out_shape=jax.ShapeDtypeStruct((n, d), x.dtype),
        compiler_params=_params(("parallel",)),
        name="rmsnorm",
    )(x, g.reshape(1, d))


class _W:
    def __init__(self, arr, *, layer=None, rows=None, row_blk=0, col_blk0=0, row_starts=None):
        self.arr, self.layer, self.row_blk, self.col_blk0 = arr, layer, row_blk, col_blk0
        self.transposed = row_starts is not None
        self.row_starts = row_starts
        assert row_starts is None or all(r % 8 == 0 for r in row_starts)
        self.rows = rows if rows is not None else arr.shape[-2]

    def spec(self, tn, single_buffer=False):
        layer, row_blk, col_blk0 = self.layer, self.row_blk, self.col_blk0
        if self.transposed:
            starts = self.row_starts

            def start(j):
                off = starts[-1]
                for k in range(len(starts) - 2, -1, -1):
                    off = jnp.where(j == k, starts[k], off)
                return pl.multiple_of(off, 8)

            return pl.BlockSpec((pl.Element(tn), pl.Element(self.arr.shape[1])), lambda i, j: (start(j), 0))
        kw = dict(pipeline_mode=pl.Buffered(1)) if single_buffer else {}
        if self.arr.ndim == 3:
            return pl.BlockSpec((None, self.rows, tn), lambda i, j: (layer, row_blk, col_blk0 + j), **kw)
        return pl.BlockSpec((self.rows, tn), lambda i, j: (row_blk, col_blk0 + j), **kw)


def _dot_w(a, w_ref, transposed):
    w = w_ref[...].astype(BF)
    return _dot_t(a, w) if transposed else jnp.dot(a, w, preferred_element_type=F32)


def _mm_kernel(*refs, n_a, w_transposed, has_norm, has_res, rope_j, rope_chunks):
    a_refs = refs[:n_a]
    w_refs = refs[n_a:2 * n_a]
    k = 2 * n_a
    res_ref = None
    if has_norm:
        g_ref = refs[k]
        k += 1
    if has_res:
        res_ref = refs[k]
        k += 1
    if rope_chunks is not None:
        cos_ref, sin_ref = refs[k], refs[k + 1]
        k += 2
    o_ref = refs[k]

    if has_norm:
        xn_sc = refs[k + 1]

        @pl.when(pl.program_id(1) == 0)
        def _():
            xn_sc[...] = _rms_normed(a_refs[0][...], g_ref[...]).astype(BF)

        lhs = [xn_sc[...]]
    else:
        lhs = [a_ref[...] for a_ref in a_refs]

    acc = _dot_w(lhs[0], w_refs[0], w_transposed[0])
    for a, w_ref, t in zip(lhs[1:], w_refs[1:], w_transposed[1:]):
        acc += _dot_w(a, w_ref, t)
    if has_res:
        acc += res_ref[...]

    def store(v):
        o_ref[...] = v.reshape(o_ref.shape).astype(o_ref.dtype)

    if rope_chunks is None:
        store(acc)
    else:
        j = pl.program_id(1)
        is_rope = (j >= rope_j[0]) & (j < rope_j[1])

        @pl.when(is_rope)
        def _():
            cos, sin = cos_ref[...], sin_ref[...]
            parts = []
            for c, on in enumerate(rope_chunks):
                blk = acc[:, c * LANES:(c + 1) * LANES]
                parts.append(_rope_lanes(blk, cos, sin) if on else blk)
            store(jnp.concatenate(parts, axis=1))

        @pl.when(jnp.logical_not(is_rope))
        def _():
            store(acc)


def _mm(a_list, w_list, n_out, *, out_dtype, tm, tn, norm_g=None, res=None, rope=None, seq=None,
        stacked_out=False, name="mm"):
    n = a_list[0].shape[0]
    n_a = len(a_list)
    in_specs, args, scratch = [], [], []
    for a in a_list:
        in_specs.append(pl.BlockSpec((tm, a.shape[1]), lambda i, j: (i, 0)))
        args.append(a)
    for w in w_list:
        in_specs.append(w.spec(tn, single_buffer=tn == n_out))
        args.append(w.arr)
    if norm_g is not None:
        assert n_a == 1
        in_specs.append(pl.BlockSpec((1, norm_g.shape[-1]), lambda i, j: (0, 0)))
        args.append(norm_g.reshape(1, -1))
        scratch.append(pltpu.VMEM((tm, a_list[0].shape[1]), BF))
    if res is not None:
        in_specs.append(pl.BlockSpec((tm, tn), lambda i, j: (i, j)))
        args.append(res)
    rope_j = rope_chunks = None
    if rope is not None:
        cos, sin, rope_j, rope_chunks = rope
        n_seq = seq // tm
        for t in (cos, sin):
            in_specs.append(pl.BlockSpec((tm, LANES), lambda i, j: (i % n_seq, 0)))
            args.append(t)
    if stacked_out:
        out_shape = jax.ShapeDtypeStruct((n_out // tn, n, tn), out_dtype)
        out_spec = pl.BlockSpec((1, tm, tn), lambda i, j: (j, i, 0))
    else:
        out_shape = jax.ShapeDtypeStruct((n, n_out), out_dtype)
        out_spec = pl.BlockSpec((tm, tn), lambda i, j: (i, j))
    return pl.pallas_call(
        functools.partial(_mm_kernel, n_a=n_a, w_transposed=tuple(w.transposed for w in w_list),
                          has_norm=norm_g is not None, has_res=res is not None, rope_j=rope_j, rope_chunks=rope_chunks),
        grid=(n // tm, n_out // tn),
        in_specs=in_specs,
        out_specs=out_spec,
        out_shape=out_shape,
        scratch_shapes=scratch,
        compiler_params=_params(("parallel", "arbitrary")),
        name=name,
    )(*args)


def _compress_kernel(x_ref, pe_ref, w1_ref, w2_ref, o_ref):
    x = x_ref[0, 0]
    half = CMP_STRIDE * HEAD_DIM
    xa = (x + pe_ref[0, 0:1, :]).astype(BF)
    xb = (x + pe_ref[0, 1:2, :]).astype(BF)
    ha = jnp.dot(xa, w1_ref[0, :half, :].astype(BF), preferred_element_type=F32)
    hb = jnp.dot(xb, w1_ref[0, half:, :].astype(BF), preferred_element_type=F32)
    hid = ha + pltpu.roll(hb, hb.shape[0] - 1, axis=0)
    act = jax.nn.gelu(hid, approximate=True).astype(BF)
    o_ref[0, 0] = jnp.dot(act, w2_ref[0].astype(BF), preferred_element_type=F32)


def _compress(xc, pe, w1, w2, batch):
    rows = xc.shape[2]
    return pl.pallas_call(
        _compress_kernel,
        grid=(2 * NSA_GROUPS, batch),
        in_specs=[
            pl.BlockSpec((1, 1, rows, CMP_STRIDE * HEAD_DIM), lambda c, b: (c, b, 0, 0)),
            pl.BlockSpec((1, 2, CMP_STRIDE * HEAD_DIM), lambda c, b: (c // NSA_GROUPS, 0, 0)),
            pl.BlockSpec((1, CMP_LEN * HEAD_DIM, CMP_HIDDEN), lambda c, b: (c // NSA_GROUPS, 0, 0)),
            pl.BlockSpec((1, CMP_HIDDEN, HEAD_DIM), lambda c, b: (c // NSA_GROUPS, 0, 0)),
        ],
        out_specs=pl.BlockSpec((1, 1, rows, HEAD_DIM), lambda c, b: (c, b, 0, 0)),
        out_shape=jax.ShapeDtypeStruct((2 * NSA_GROUPS, batch, rows, HEAD_DIM), F32),
        compiler_params=_params(("arbitrary", "arbitrary")),
        name="nsa_compress",
    )(xc, pe, w1, w2)


def _stacked_softmax_pv(s_raw, bias, v, scale, n_stack, tq):
    exp2_scale = scale * float(np.log2(np.e))
    es, ls = [], []
    for h in range(n_stack):
        u = s_raw[h * tq:(h + 1) * tq] + bias
        e = jnp.exp2((u - jnp.max(u, axis=-1, keepdims=True)) * exp2_scale)
        ls.append(jnp.sum(e, axis=-1, keepdims=True))
        es.append(e.astype(BF))
    o = jnp.dot(jnp.concatenate(es, axis=0), v, preferred_element_type=F32)
    return o / jnp.concatenate(ls, axis=0)


def _nsa_kernel(q_ref, ks_ref, kw_ref, vs_ref, vw_ref, kvc_ref, gate_ref, cos_ref, sin_ref, m_ref, o_ref,
                *, tq, seq, key_step):
    qi = pl.program_id(1)
    hg = HEADS_PER_GROUP
    scale = HEAD_DIM ** -0.5
    n_cmp = (seq - CMP_LEN) // CMP_STRIDE + 1
    n_slc = seq // SLC_LEN
    band = WIN + tq
    neg_bf = jnp.asarray(NEG, BF)

    t_row = qi * tq + lax.broadcasted_iota(jnp.int32, (tq, 1), 0)
    cos = jnp.concatenate([cos_ref[...]] * hg, axis=0)
    sin = jnp.concatenate([sin_ref[...]] * hg, axis=0)
    gates = jax.nn.sigmoid(gate_ref[0])

    n_lane = lax.broadcasted_iota(jnp.int32, (1, LANES), 1)
    cmask = (n_lane * CMP_STRIDE + (CMP_LEN - 1) <= t_row) & (n_lane < n_cmp)
    cbias = jnp.where(cmask, 0.0, NEG)
    j_blk = lax.broadcasted_iota(jnp.int32, (n_slc, 1), 0)
    slc_shift = SLC_LEN.bit_length() - 1
    cur = jnp.right_shift(qi * tq + lax.broadcasted_iota(jnp.int32, (1, tq), 1), slc_shift)
    forced = (j_blk == 0) | (j_blk == cur) | (j_blk == cur - 1)
    w_start = pl.multiple_of(jnp.maximum(qi * tq - WIN, 0), tq)
    w_pos = w_start + lax.broadcasted_iota(jnp.int32, (1, band), 1)
    wbias = jnp.where((w_pos <= t_row) & (w_pos > t_row - WIN), 0.0, NEG)

    def group_queries(g):
        qg = q_ref[0, :, g * hg * HEAD_DIM:(g + 1) * hg * HEAD_DIM].astype(F32)
        qp = jnp.concatenate([qg[:, h * HEAD_DIM:(h + 1) * HEAD_DIM] for h in range(hg)], axis=0)
        return qp.astype(BF), _rope_lanes(qp, cos, sin).astype(BF)

    def compressed_branch(g, qp):
        kc = kvc_ref[g, 0].astype(BF)
        vc = kvc_ref[NSA_GROUPS + g, 0].astype(BF)
        s = _dot_t(qp, kc)
        ps = []
        for h in range(hg):
            u = s[h * tq:(h + 1) * tq] * scale + cbias
            e, l = _softmax_rows(u)
            ps.append(jnp.where(cmask, e / l, 0.0))
        o_cmp = jnp.dot(jnp.concatenate(ps, axis=0).astype(BF), vc, preferred_element_type=F32)

        p_sum = ps[0]
        for h in range(1, hg):
            p_sum = p_sum + ps[h]
        imp = lax.dot_general(m_ref[...], p_sum, (((1,), (1,)), ((), ())), preferred_element_type=F32,
                              precision=lax.Precision.HIGHEST)
        imp = jnp.where(j_blk > cur, -jnp.inf, jnp.where(forced, jnp.inf, imp))
        rank = jnp.zeros((n_slc, tq), jnp.int32)
        for i in range(n_slc):
            ci = imp[i:i + 1, :]
            beats = (ci > imp) | ((ci == imp) & (j_blk > i))
            rank = rank + beats.astype(jnp.int32)
        return o_cmp, jnp.where(rank < SLC_TOP, 0.0, NEG).astype(BF)

    qs = [group_queries(g) for g in range(NSA_GROUPS)]
    cmps = [compressed_branch(g, qs[g][0]) for g in range(NSA_GROUPS)]

    def attend(width):
        blocks = width // SLC_LEN
        expand = jnp.where(jnp.right_shift(lax.broadcasted_iota(jnp.int32, (blocks, width), 1), slc_shift)
                           == lax.broadcasted_iota(jnp.int32, (blocks, width), 0), 1.0, 0.0).astype(BF)
        s_pos = lax.broadcasted_iota(jnp.int32, (1, width), 1)
        lanes = [slice(g * HEAD_DIM, (g + 1) * HEAD_DIM) for g in range(NSA_GROUPS)]
        s_win = [_dot_t(qs[g][1], kw_ref[0, pl.ds(w_start, band), lanes[g]]) for g in range(NSA_GROUPS)]
        s_slc = [_dot_t(qs[g][1], ks_ref[0, :width, lanes[g]]) for g in range(NSA_GROUPS)]
        for g in range(NSA_GROUPS):
            o_win = _stacked_softmax_pv(s_win[g], wbias, vw_ref[0, pl.ds(w_start, band), lanes[g]], scale, hg, tq)
            o_cmp, sel_neg = cmps[g]
            bias = lax.dot_general(sel_neg[:blocks], expand, (((0,), (0,)), ((), ())),
                                   preferred_element_type=F32)
            bias = jnp.where(s_pos <= t_row, bias, NEG)
            o_slc = _stacked_softmax_pv(s_slc[g], bias, vs_ref[0, :width, lanes[g]], scale, hg, tq)
            for hh in range(hg):
                h = g * hg + hh
                rows = slice(hh * tq, (hh + 1) * tq)
                o = (o_cmp[rows] * gates[:, 3 * h:3 * h + 1]
                     + o_slc[rows] * gates[:, 3 * h + 1:3 * h + 2]
                     + o_win[rows] * gates[:, 3 * h + 2:3 * h + 3])
                o_ref[0, :, h * HEAD_DIM:(h + 1) * HEAD_DIM] = o.astype(o_ref.dtype)

    n_widths = seq // key_step
    super_row = qi // (key_step // tq)
    for k in range(n_widths):
        pl.when(super_row == k)(functools.partial(attend, (k + 1) * key_step))


def _nsa(main, kvc, planes, gate_plane, cos, sin, m_cs, *, batch, seq, tq=128, key_step=512):
    nq = seq // tq
    kv_w = NSA_GROUPS * HEAD_DIM
    kv0 = NSA_HEADS * HEAD_DIM // kv_w
    return pl.pallas_call(
        functools.partial(_nsa_kernel, tq=tq, seq=seq, key_step=key_step),
        grid=(batch, nq),
        in_specs=[
            pl.BlockSpec((1, tq, NSA_HEADS * HEAD_DIM), lambda b, i: (b, i, 0)),
            pl.BlockSpec((1, seq, kv_w), lambda b, i: (b, 0, kv0)),
            pl.BlockSpec((1, seq, kv_w), lambda b, i: (b, 0, kv0 + 2)),
            pl.BlockSpec((1, seq, kv_w), lambda b, i: (b, 0, kv0 + 1)),
            pl.BlockSpec((1, seq, kv_w), lambda b, i: (b, 0, kv0 + 3)),
            pl.BlockSpec((2 * NSA_GROUPS, 1, LANES, HEAD_DIM), lambda b, i: (0, b, 0, 0)),
            pl.BlockSpec((1, tq, LANES), lambda b, i: (gate_plane, b * nq + i, 0)),
            pl.BlockSpec((tq, LANES), lambda b, i: (i, 0)),
            pl.BlockSpec((tq, LANES), lambda b, i: (i, 0)),
            pl.BlockSpec(m_cs.shape, lambda b, i: (0, 0)),
        ],
        out_specs=pl.BlockSpec((1, tq, NSA_HEADS * HEAD_DIM), lambda b, i: (b, i, 0)),
        out_shape=jax.ShapeDtypeStruct((batch, seq, NSA_HEADS * HEAD_DIM), BF),
        compiler_params=_params(("parallel", "arbitrary")),
        name="nsa_attention",
    )(main, main, main, main, main, kvc, planes, cos, sin, m_cs)


def _cumdecay_kernel(x_ref, b_ref, o_ref):
    z = x_ref[0] + b_ref[...]
    c = jnp.minimum(z, 0.0) - jnp.log1p(jnp.exp(-jnp.abs(z)))
    rows = c.shape[0]
    r = lax.broadcasted_iota(jnp.int32, (rows, 1), 0)
    k = 1
    while k < rows:
        c = c + jnp.where(r >= k, pltpu.roll(c, k, axis=0), 0.0)
        k *= 2
    o_ref[0] = c


def _cumdecay(planes, plane, bias, *, batch, seq):
    return pl.pallas_call(
        _cumdecay_kernel,
        grid=(batch,),
        in_specs=[pl.BlockSpec((1, seq, LANES), lambda b: (plane, b, 0)),
                  pl.BlockSpec((1, LANES), lambda b: (0, 0))],
        out_specs=pl.BlockSpec((1, seq, LANES), lambda b: (b, 0, 0)),
        out_shape=jax.ShapeDtypeStruct((batch, seq, LANES), F32),
        compiler_params=_params(("arbitrary",)),
        name="fox_cumdecay",
    )(planes, bias)


def _attn_kernel(*refs, tq, seq, scale, n_qk, decay):
    q_ref = refs[0]
    k_refs = refs[1:1 + n_qk]
    v_ref = refs[1 + n_qk]
    k = 2 + n_qk
    if decay:
        ck_ref = refs[k]
        k += 1
    o_ref = refs[k]

    row = lax.broadcasted_iota(jnp.int32, (tq, 1), 0)
    col = lax.broadcasted_iota(jnp.int32, (1, tq), 1)
    tri = col <= row
    exp2_scale = scale * float(np.log2(np.e))

    k_all = k_refs[0][0] if n_qk == 1 else jnp.concatenate([k_ref[0] for k_ref in k_refs], axis=1)

    n_tiles = seq // tq
    scores = lambda i: _dot_t(q_ref[0, i * tq:(i + 1) * tq, :], k_all[:(i + 1) * tq])
    s_next = scores(n_tiles - 1)
    for i in reversed(range(n_tiles)):
        lo, hi = i * tq, (i + 1) * tq
        s = s_next
        if i > 0:
            s_next = scores(i - 1)
        if decay:
            ck = ck_ref[0, :, :hi]
            cq = jnp.sum(jnp.where(col == row, ck[:, lo:hi], 0.0), axis=1, keepdims=True)
            s = s * scale + (cq - ck)
            prob = lambda u, m: jnp.exp(u - m)
        else:
            prob = lambda u, m: jnp.exp2((u - m) * exp2_scale)
        s_diag = jnp.where(tri, s[:, lo:hi], NEG)
        m = jnp.max(s_diag, axis=-1, keepdims=True)
        if i > 0:
            m = jnp.maximum(m, jnp.max(s[:, :lo], axis=-1, keepdims=True))
        e_diag = prob(s_diag, m)
        l = jnp.sum(e_diag, axis=-1, keepdims=True)
        e = e_diag.astype(BF)
        if i > 0:
            e_off = prob(s[:, :lo], m)
            l = l + jnp.sum(e_off, axis=-1, keepdims=True)
            e = jnp.concatenate([e_off.astype(BF), e], axis=1)
        o = jnp.dot(e, v_ref[0, :hi, :], preferred_element_type=F32) / l
        o_ref[0, lo:hi, :] = o.astype(o_ref.dtype)


def _attention(q_arr, q_blk0, k_arrs, k_blk0s, k_per_head, v_arr, v_blk0, *, batch, seq, heads, scale,
               decay=None, tq=256, name="attention"):
    n_qk = len(k_arrs)
    in_specs = [pl.BlockSpec((1, seq, n_qk * HEAD_DIM), lambda b, h: (b, 0, q_blk0 + h))]
    args = [q_arr]
    for arr, blk0, per_head in zip(k_arrs, k_blk0s, k_per_head):
        if per_head:
            in_specs.append(pl.BlockSpec((1, seq, HEAD_DIM), lambda b, h, blk0=blk0: (b, 0, blk0 + h)))
        else:
            in_specs.append(pl.BlockSpec((1, seq, HEAD_DIM), lambda b, h, blk0=blk0: (b, 0, blk0)))
        args.append(arr)
    in_specs.append(pl.BlockSpec((1, seq, HEAD_DIM), lambda b, h: (b, 0, v_blk0 + h)))
    args.append(v_arr)
    if decay is not None:
        in_specs.append(pl.BlockSpec((1, 1, seq), lambda b, h: (b * heads + h, 0, 0)))
        args.append(decay)
    return pl.pallas_call(
        functools.partial(_attn_kernel, tq=tq, seq=seq, scale=scale, n_qk=n_qk, decay=decay is not None),
        grid=(batch, heads),
        in_specs=in_specs,
        out_specs=pl.BlockSpec((1, seq, HEAD_DIM), lambda b, h: (b, 0, h)),
        out_shape=jax.ShapeDtypeStruct((batch, seq, heads * HEAD_DIM), BF),
        compiler_params=_params(("parallel", "arbitrary")),
        name=name,
    )(*args)


def _mla_latent_kernel(x_ref, g_ref, wt_ref, wkr_ref, gq_ref, gkv_ref, cos_ref, sin_ref, cq_ref, ckv_ref, kr_ref):
    h = _rms_normed(x_ref[...], g_ref[...]).astype(BF)
    c = _dot_t(h, wt_ref[...].astype(BF))
    cq_ref[...] = _rms_normed(c[:, :MLA_Q_RANK], gq_ref[...]).astype(BF)
    ckv_ref[...] = _rms_normed(c[:, MLA_Q_RANK:], gkv_ref[...]).astype(BF)
    kr_ref[...] = _rope_lanes(_dot_t(h, wkr_ref[...].astype(BF)), cos_ref[...], sin_ref[...]).astype(BF)


def _mla_latent(x, g, wt, w_kr, gq, gkv, cos, sin, *, seq, tm=512):
    n = x.shape[0]
    n_seq = seq // tm
    return pl.pallas_call(
        _mla_latent_kernel,
        grid=(n // tm,),
        in_specs=[
            pl.BlockSpec((tm, x.shape[1]), lambda i: (i, 0)),
            pl.BlockSpec((1, x.shape[1]), lambda i: (0, 0)),
            pl.BlockSpec((MLA_Q_RANK + MLA_KV_RANK, wt.shape[1]), lambda i: (0, 0)),
            pl.BlockSpec(w_kr.shape, lambda i: (0, 0)),
            pl.BlockSpec((1, MLA_Q_RANK), lambda i: (0, 0)),
            pl.BlockSpec((1, MLA_KV_RANK), lambda i: (0, 0)),
            pl.BlockSpec((tm, LANES), lambda i: (i % n_seq, 0)),
            pl.BlockSpec((tm, LANES), lambda i: (i % n_seq, 0)),
        ],
        out_specs=[
            pl.BlockSpec((tm, MLA_Q_RANK), lambda i: (i, 0)),
            pl.BlockSpec((tm, MLA_KV_RANK), lambda i: (i, 0)),
            pl.BlockSpec((tm, LANES), lambda i: (i, 0)),
        ],
        out_shape=[
            jax.ShapeDtypeStruct((n, MLA_Q_RANK), BF),
            jax.ShapeDtypeStruct((n, MLA_KV_RANK), BF),
            jax.ShapeDtypeStruct((n, LANES), BF),
        ],
        compiler_params=_params(("parallel",)),
        name="mla_latent",
    )(x, g.reshape(1, -1), wt, w_kr, gq.reshape(1, -1), gkv.reshape(1, -1), cos, sin)


def _rope_tables(seq, dim):
    half = dim // 2
    inv = 1.0 / (ROPE_THETA ** (np.arange(0, dim, 2, dtype=np.float32) / dim))
    ang = np.arange(seq, dtype=np.float32)[:, None] * inv.astype(np.float32)[None, :]
    cos = np.zeros((seq, LANES), np.float32)
    sin = np.zeros((seq, LANES), np.float32)
    cos[:, :half] = np.cos(ang)
    cos[:, LANES // 2:LANES // 2 + half] = np.cos(ang)
    sin[:, :half] = -np.sin(ang)
    sin[:, LANES // 2:LANES // 2 + half] = np.sin(ang)
    return jnp.asarray(cos), jnp.asarray(sin)


def _cmp_to_slc(seq):
    n_cmp = (seq - CMP_LEN) // CMP_STRIDE + 1
    n_slc = seq // SLC_LEN
    c0 = np.arange(n_cmp) * CMP_STRIDE
    c1 = c0 + CMP_LEN
    s0 = np.arange(n_slc) * SLC_LEN
    s1 = s0 + SLC_LEN
    ov = np.clip(np.minimum(c1[:, None], s1[None, :]) - np.maximum(c0[:, None], s0[None, :]), 0, None)
    m = np.zeros((n_slc, LANES), np.float32)
    m[:, :n_cmp] = (ov / CMP_LEN).T
    return jnp.asarray(m)


def _spread_rope_cols(w):
    half = MLA_ROPE // 2
    z = jnp.zeros(w.shape[:-1] + (LANES // 2 - half,), w.dtype)
    return jnp.concatenate([w[..., :half], z, w[..., half:], z], axis=-1)


def _hybrid_mixer(x, g, w_in, w_out, layer, cmp_pe, cmp_w1, cmp_w2, f_bias, *, batch, seq):
    n = batch * seq
    nq_w = NSA_HEADS * HEAD_DIM
    kv_w = NSA_GROUPS * HEAD_DIM
    fx_w = FOX_HEADS * HEAD_DIM
    c_cmp, c_slc, c_gate = nq_w, nq_w + 2 * kv_w, nq_w + 6 * kv_w
    c_fox = c_gate + 3 * NSA_HEADS
    c_f = c_fox + 3 * fx_w
    wt = jnp.swapaxes(w_in[layer], 0, 1)

    cos, sin = _rope_tables(seq, HEAD_DIM)
    tn = 4 * kv_w
    main_starts = (0, c_slc) + tuple(range(c_fox, c_f, tn))
    main = _mm([x], [_W(wt, row_starts=main_starts)], len(main_starts) * tn, out_dtype=BF, tm=1024, tn=tn, norm_g=g,
               rope=(cos, sin, (nq_w // tn, nq_w // tn + 1), (True, True, False, False) * 2), seq=seq,
               name="proj_main")
    f_lane0 = LANES - FOX_HEADS
    plane_starts = tuple(range(c_cmp, c_slc, LANES)) + (c_gate, c_f - f_lane0)
    gate_plane, f_plane = len(plane_starts) - 2, len(plane_starts) - 1
    planes = _mm([x], [_W(wt, row_starts=plane_starts)], len(plane_starts) * LANES, out_dtype=F32, tm=1024, tn=LANES,
                 norm_g=g, stacked_out=True, name="proj_planes")

    rows16 = seq // CMP_STRIDE
    kvc = _compress(planes.reshape(len(plane_starts), batch, rows16, CMP_STRIDE * HEAD_DIM),
                    cmp_pe.reshape(2, 2, CMP_STRIDE * HEAD_DIM), cmp_w1, cmp_w2, batch)

    main3 = main.reshape(batch, seq, -1)
    o_nsa = _nsa(main3, kvc, planes, gate_plane, cos, sin, _cmp_to_slc(seq), batch=batch, seq=seq)

    bias = jnp.zeros((1, LANES), F32).at[0, f_lane0:].set(f_bias)
    cum = _cumdecay(planes, f_plane, bias, batch=batch, seq=seq)[:, :, f_lane0:]
    cum_t = jnp.swapaxes(cum, 1, 2).reshape(batch * FOX_HEADS, seq)
    blk = lambda col: col // HEAD_DIM
    c_qf = nq_w + 4 * kv_w
    o_fox = _attention(main3, blk(c_qf), [main3], [blk(c_qf + fx_w)], [True], main3, blk(c_qf + 2 * fx_w),
                       batch=batch, seq=seq, heads=FOX_HEADS, scale=HEAD_DIM ** -0.5,
                       decay=cum_t[:, None, :], name="fox_attention")

    return _mm([o_nsa.reshape(n, nq_w), o_fox.reshape(n, fx_w)],
               [_W(w_out, layer=layer, rows=nq_w, row_blk=0), _W(w_out, layer=layer, rows=fx_w, row_blk=1)],
               D_MODEL, out_dtype=F32, tm=512, tn=D_MODEL, res=x, name="proj_out_hybrid")


def _mla_mixer(x, g, w_in, q_norm, kv_norm, w_uq, w_ukv, w_out, layer, *, batch, seq):
    n = batch * seq
    qk = MLA_NOPE + MLA_ROPE
    wt_lat = jnp.swapaxes(w_in, 0, 1)
    w_kr = jnp.swapaxes(_spread_rope_cols(w_in[:, MLA_Q_RANK + MLA_KV_RANK:]), 0, 1)
    wq = w_uq.reshape(MLA_Q_RANK, MLA_HEADS, qk)
    wq = jnp.concatenate([wq[..., :MLA_NOPE], _spread_rope_cols(wq[..., MLA_NOPE:])], axis=-1)
    wq = wq.reshape(MLA_Q_RANK, MLA_HEADS * 2 * HEAD_DIM)
    wkv = w_ukv.reshape(MLA_KV_RANK, MLA_HEADS, MLA_NOPE + MLA_V)
    wkv = jnp.concatenate([wkv[..., :MLA_NOPE].reshape(MLA_KV_RANK, -1),
                           wkv[..., MLA_NOPE:].reshape(MLA_KV_RANK, -1)], axis=1)

    cos, sin = _rope_tables(seq, MLA_ROPE)
    cq, ckv, kr = _mla_latent(x, g, wt_lat, w_kr, q_norm, kv_norm, cos, sin, seq=seq)
    tn = 1024
    q = _mm([cq], [_W(wq)], wq.shape[1], out_dtype=BF, tm=1024, tn=tn,
            rope=(cos, sin, (0, wq.shape[1] // tn), (False, True) * (tn // (2 * LANES))), seq=seq, name="proj_mla_q")
    kv = _mm([ckv], [_W(wkv)], wkv.shape[1], out_dtype=BF, tm=1024, tn=tn, name="proj_mla_kv")

    q3 = q.reshape(batch, seq, -1)
    kv3 = kv.reshape(batch, seq, -1)
    kr3 = kr.reshape(batch, seq, LANES)
    o = _attention(q3, 0, [kv3, kr3], [0, 0], [True, False], kv3, MLA_HEADS, batch=batch, seq=seq,
                   heads=MLA_HEADS, scale=qk ** -0.5, name="mla_attention")
    return _mm([o.reshape(n, MLA_HEADS * MLA_V)], [_W(w_out, layer=layer)], D_MODEL, out_dtype=F32, tm=512, tn=D_MODEL,
               res=x, name="proj_out_mla")


def _ffn(x, norms, w_in, w_out, layer):
    depth, d = norms.shape
    a = _ffn_up(x, norms.reshape(depth, 1, d), w_in, layer)
    return _mm([a], [_W(w_out, layer=layer)], d, out_dtype=F32, tm=1024, tn=256, res=x, name="ffn_down")


def kernel(x, ffn1_norm, ffn1_w_in, ffn1_w_out, mix_norm, ffn2_norm, ffn2_w_in, ffn2_w_out, hyb_w_in, hyb_w_out, nsa_cmp_pe, nsa_cmp_w1, nsa_cmp_w2, fox_f_bias, mla_w_in, mla_q_norm, mla_kv_norm, mla_w_uq, mla_w_ukv, mla_w_out, final_norm):
    batch, seq, d = x.shape
    depth = ffn1_norm.shape[0]
    x = x.reshape(batch * seq, d)
    for i in range(depth):
        x = _ffn(x, ffn1_norm, ffn1_w_in, ffn1_w_out, i)
        if i % 2 == 0:
            e = i // 2
            x = _hybrid_mixer(x, mix_norm[i], hyb_w_in, hyb_w_out, e, nsa_cmp_pe[e], nsa_cmp_w1[e], nsa_cmp_w2[e],
                              fox_f_bias[e], batch=batch, seq=seq)
        else:
            o = i // 2
            x = _mla_mixer(x, mix_norm[i], mla_w_in[o], mla_q_norm[o], mla_kv_norm[o], mla_w_uq[o], mla_w_ukv[o],
                           mla_w_out, o, batch=batch, seq=seq)
        x = _ffn(x, ffn2_norm, ffn2_w_in, ffn2_w_out, i)
    return _rmsnorm(x, final_norm).reshape(batch, seq, d)
```

```python
import functools

import numpy as np
import jax
import jax.numpy as jnp
from jax import lax
from jax.experimental import pallas as pl
from jax.experimental.pallas import tpu as pltpu

D_MODEL = 2048
HEAD_DIM = 128
ROPE_THETA = 10000.0
NORM_EPS = 1e-6
NEG = -1e30

NSA_HEADS = 8
NSA_GROUPS = 2
HEADS_PER_GROUP = NSA_HEADS // NSA_GROUPS
CMP_LEN = 32
CMP_STRIDE = 16
CMP_HIDDEN = 256
SLC_LEN = 64
SLC_TOP = 8
WIN = 512
FOX_HEADS = 8
MLA_HEADS = 16
MLA_Q_RANK = 512
MLA_KV_RANK = 512
MLA_NOPE = 128
MLA_ROPE = 64
MLA_V = 128
D_FF = 5632

LANES = 128
VMEM_LIMIT = 56 * 1024 * 1024

BF = jnp.bfloat16
F32 = jnp.float32


def _params(sem, vmem=VMEM_LIMIT):
    return pltpu.CompilerParams(dimension_semantics=sem, vmem_limit_bytes=vmem)


def _rope_lanes(x, cos, sin):
    return x * cos + pltpu.roll(x, LANES // 2, axis=1) * sin


def _dot_t(a, b):
    return lax.dot_general(a, b, (((1,), (1,)), ((), ())), preferred_element_type=F32)


def _softmax_rows(s):
    m = jnp.max(s, axis=-1, keepdims=True)
    e = jnp.exp(s - m)
    return e, jnp.sum(e, axis=-1, keepdims=True)


def _rms_normed(x, g):
    ms = jnp.mean(x * x, axis=-1, keepdims=True)
    return x * lax.rsqrt(ms + NORM_EPS) * g


def _ffn_up_kernel(x_ref, g_ref, wa_ref, wb_ref, w2_ref, o_ref, w2b_ref, xn_sc, *, n_cast):
    i, j = pl.program_id(0), pl.program_id(1)

    @pl.when(j == 0)
    def _():
        xn_sc[...] = _rms_normed(x_ref[...], g_ref[...]).astype(BF)

    @pl.when(i * pl.num_programs(1) + j < n_cast)
    def _():
        w2b_ref[...] = w2_ref[...].astype(BF)

    xn = xn_sc[...]
    h1 = jnp.dot(xn, wa_ref[...].astype(BF), preferred_element_type=F32)
    h2 = jnp.dot(xn, wb_ref[...].astype(BF), preferred_element_type=F32)
    o_ref[...] = (h1 * jax.nn.sigmoid(h1) * h2 * 0.5).astype(BF)


def _ffn_up(x, g, w_in, w_out, layer, *, tm=1024, tf=512, cast_rows=256):
    n, d = x.shape
    n_f = D_FF // tf
    n_cast = D_FF // cast_rows
    assert n_cast <= (n // tm) * n_f
    chunk = lambda i, j: jnp.minimum(i * n_f + j, n_cast - 1)
    return pl.pallas_call(
        functools.partial(_ffn_up_kernel, n_cast=n_cast),
        grid=(n // tm, n_f),
        in_specs=[
            pl.BlockSpec((tm, d), lambda i, j: (i, 0)),
            pl.BlockSpec((None, 1, d), lambda i, j: (layer, 0, 0)),
            pl.BlockSpec((None, d, tf), lambda i, j: (layer, 0, j)),
            pl.BlockSpec((None, d, tf), lambda i, j: (layer, 0, n_f + j)),
            pl.BlockSpec((None, cast_rows, d), lambda i, j: (layer, chunk(i, j), 0)),
        ],
        out_specs=[pl.BlockSpec((tm, tf), lambda i, j: (i, j)),
                   pl.BlockSpec((cast_rows, d), lambda i, j: (chunk(i, j), 0))],
        out_shape=[jax.ShapeDtypeStruct((n, D_FF), BF), jax.ShapeDtypeStruct((D_FF, d), BF)],
        scratch_shapes=[pltpu.VMEM((tm, d), BF)],
        compiler_params=_params(("arbitrary", "arbitrary")),
        name="ffn_up",
    )(x, g, w_in, w_in, w_out)


def _norm_kernel(x_ref, g_ref, o_ref):
    o_ref[...] = _rms_normed(x_ref[...], g_ref[...]).astype(o_ref.dtype)


def _rmsnorm(x, g, *, tm=512):
    n, d = x.shape
    return pl.pallas_call(
        _norm_kernel,
        grid=(n // tm,),
        in_specs=[pl.BlockSpec((tm, d), lambda i: (i, 0)), pl.BlockSpec((1, d), lambda i: (0, 0))],
        out_specs=pl.BlockSpec((tm, d), lambda i: (i, 0)),
        out_shape=jax.ShapeDtypeStruct((n, d), x.dtype),
        compiler_params=_params(("parallel",)),
        name="rmsnorm",
    )(x, g.reshape(1, d))


class _W:
    def __init__(self, arr, *, layer=None, rows=None, row_blk=0, col_blk0=0, row_starts=None):
        self.arr, self.layer, self.row_blk, self.col_blk0 = arr, layer, row_blk, col_blk0
        self.transposed = row_starts is not None
        self.row_starts = row_starts
        assert row_starts is None or all(r % 8 == 0 for r in row_starts)
        self.rows = rows if rows is not None else arr.shape[-2]

    def spec(self, tn, single_buffer=False):
        layer, row_blk, col_blk0 = self.layer, self.row_blk, self.col_blk0
        if self.transposed:
            starts = self.row_starts

            def start(j):
                off = starts[-1]
                for k in range(len(starts) - 2, -1, -1):
                    off = jnp.where(j == k, starts[k], off)
                return pl.multiple_of(off, 8)

            return pl.BlockSpec((pl.Element(tn), pl.Element(self.arr.shape[1])), lambda i, j: (start(j), 0))
        kw = dict(pipeline_mode=pl.Buffered(1)) if single_buffer else {}
        if self.arr.ndim == 3:
            return pl.BlockSpec((None, self.rows, tn), lambda i, j: (layer, row_blk, col_blk0 + j), **kw)
        return pl.BlockSpec((self.rows, tn), lambda i, j: (row_blk, col_blk0 + j), **kw)


def _dot_w(a, w_ref, transposed):
    w = w_ref[...].astype(BF)
    return _dot_t(a, w) if transposed else jnp.dot(a, w, preferred_element_type=F32)


def _mm_kernel(*refs, n_a, w_transposed, has_norm, has_res, rope_j, rope_chunks):
    a_refs = refs[:n_a]
    w_refs = refs[n_a:2 * n_a]
    k = 2 * n_a
    res_ref = None
    if has_norm:
        g_ref = refs[k]
        k += 1
    if has_res:
        res_ref = refs[k]
        k += 1
    if rope_chunks is not None:
        cos_ref, sin_ref = refs[k], refs[k + 1]
        k += 2
    o_ref = refs[k]

    if has_norm:
        xn_sc = refs[k + 1]

        @pl.when(pl.program_id(1) == 0)
        def _():
            xn_sc[...] = _rms_normed(a_refs[0][...], g_ref[...]).astype(BF)

        lhs = [xn_sc[...]]
    else:
        lhs = [a_ref[...] for a_ref in a_refs]

    acc = _dot_w(lhs[0], w_refs[0], w_transposed[0])
    for a, w_ref, t in zip(lhs[1:], w_refs[1:], w_transposed[1:]):
        acc += _dot_w(a, w_ref, t)
    if has_res:
        acc += res_ref[...]

    def store(v):
        o_ref[...] = v.reshape(o_ref.shape).astype(o_ref.dtype)

    if rope_chunks is None:
        store(acc)
    else:
        j = pl.program_id(1)
        is_rope = (j >= rope_j[0]) & (j < rope_j[1])

        @pl.when(is_rope)
        def _():
            cos, sin = cos_ref[...], sin_ref[...]
            parts = []
            for c, on in enumerate(rope_chunks):
                blk = acc[:, c * LANES:(c + 1) * LANES]
                parts.append(_rope_lanes(blk, cos, sin) if on else blk)
            store(jnp.concatenate(parts, axis=1))

        @pl.when(jnp.logical_not(is_rope))
        def _():
            store(acc)


def _mm(a_list, w_list, n_out, *, out_dtype, tm, tn, norm_g=None, res=None, rope=None, seq=None,
        stacked_out=False, name="mm"):
    n = a_list[0].shape[0]
    n_a = len(a_list)
    in_specs, args, scratch = [], [], []
    for a in a_list:
        in_specs.append(pl.BlockSpec((tm, a.shape[1]), lambda i, j: (i, 0)))
        args.append(a)
    for w in w_list:
        in_specs.append(w.spec(tn, single_buffer=tn == n_out))
        args.append(w.arr)
    if norm_g is not None:
        assert n_a == 1
        in_specs.append(pl.BlockSpec((1, norm_g.shape[-1]), lambda i, j: (0, 0)))
        args.append(norm_g.reshape(1, -1))
        scratch.append(pltpu.VMEM((tm, a_list[0].shape[1]), BF))
    if res is not None:
        in_specs.append(pl.BlockSpec((tm, tn), lambda i, j: (i, j)))
        args.append(res)
    rope_j = rope_chunks = None
    if rope is not None:
        cos, sin, rope_j, rope_chunks = rope
        n_seq = seq // tm
        for t in (cos, sin):
            in_specs.append(pl.BlockSpec((tm, LANES), lambda i, j: (i % n_seq, 0)))
            args.append(t)
    if stacked_out:
        out_shape = jax.ShapeDtypeStruct((n_out // tn, n, tn), out_dtype)
        out_spec = pl.BlockSpec((1, tm, tn), lambda i, j: (j, i, 0))
    else:
        out_shape = jax.ShapeDtypeStruct((n, n_out), out_dtype)
        out_spec = pl.BlockSpec((tm, tn), lambda i, j: (i, j))
    return pl.pallas_call(
        functools.partial(_mm_kernel, n_a=n_a, w_transposed=tuple(w.transposed for w in w_list),
                          has_norm=norm_g is not None, has_res=res is not None, rope_j=rope_j, rope_chunks=rope_chunks),
        grid=(n // tm, n_out // tn),
        in_specs=in_specs,
        out_specs=out_spec,
        out_shape=out_shape,
        scratch_shapes=scratch,
        compiler_params=_params(("parallel", "arbitrary")),
        name=name,
    )(*args)


def _compress_kernel(x_ref, pe_ref, w1_ref, w2_ref, o_ref):
    x = x_ref[0, 0]
    half = CMP_STRIDE * HEAD_DIM
    xa = (x + pe_ref[0, 0:1, :]).astype(BF)
    xb = (x + pe_ref[0, 1:2, :]).astype(BF)
    ha = jnp.dot(xa, w1_ref[0, :half, :].astype(BF), preferred_element_type=F32)
    hb = jnp.dot(xb, w1_ref[0, half:, :].astype(BF), preferred_element_type=F32)
    hid = ha + pltpu.roll(hb, hb.shape[0] - 1, axis=0)
    act = jax.nn.gelu(hid, approximate=True).astype(BF)
    o_ref[0, 0] = jnp.dot(act, w2_ref[0].astype(BF), preferred_element_type=F32)


def _compress(xc, pe, w1, w2, batch):
    rows = xc.shape[2]
    return pl.pallas_call(
        _compress_kernel,
        grid=(2 * NSA_GROUPS, batch),
        in_specs=[
            pl.BlockSpec((1, 1, rows, CMP_STRIDE * HEAD_DIM), lambda c, b: (c, b, 0, 0)),
            pl.BlockSpec((1, 2, CMP_STRIDE * HEAD_DIM), lambda c, b: (c // NSA_GROUPS, 0, 0)),
            pl.BlockSpec((1, CMP_LEN * HEAD_DIM, CMP_HIDDEN), lambda c, b: (c // NSA_GROUPS, 0, 0)),
            pl.BlockSpec((1, CMP_HIDDEN, HEAD_DIM), lambda c, b: (c // NSA_GROUPS, 0, 0)),
        ],
        out_specs=pl.BlockSpec((1, 1, rows, HEAD_DIM), lambda c, b: (c, b, 0, 0)),
        out_shape=jax.ShapeDtypeStruct((2 * NSA_GROUPS, batch, rows, HEAD_DIM), F32),
        compiler_params=_params(("arbitrary", "arbitrary")),
        name="nsa_compress",
    )(xc, pe, w1, w2)


def _stacked_softmax_pv(s_raw, bias, v, scale, n_stack, tq):
    exp2_scale = scale * float(np.log2(np.e))
    es, ls = [], []
    for h in range(n_stack):
        u = s_raw[h * tq:(h + 1) * tq] + bias
        e = jnp.exp2((u - jnp.max(u, axis=-1, keepdims=True)) * exp2_scale)
        ls.append(jnp.sum(e, axis=-1, keepdims=True))
        es.append(e.astype(BF))
    o = jnp.dot(jnp.concatenate(es, axis=0), v, preferred_element_type=F32)
    return o / jnp.concatenate(ls, axis=0)


def _nsa_kernel(q_ref, ks_ref, kw_ref, vs_ref, vw_ref, kvc_ref, gate_ref, cos_ref, sin_ref, m_ref, o_ref,
                *, tq, seq, key_step):
    qi = pl.program_id(1)
    hg = HEADS_PER_GROUP
    scale = HEAD_DIM ** -0.5
    n_cmp = (seq - CMP_LEN) // CMP_STRIDE + 1
    n_slc = seq // SLC_LEN
    band = WIN + tq
    neg_bf = jnp.asarray(NEG, BF)

    t_row = qi * tq + lax.broadcasted_iota(jnp.int32, (tq, 1), 0)
    cos = jnp.concatenate([cos_ref[...]] * hg, axis=0)
    sin = jnp.concatenate([sin_ref[...]] * hg, axis=0)
    gates = jax.nn.sigmoid(gate_ref[0])

    n_lane = lax.broadcasted_iota(jnp.int32, (1, LANES), 1)
    cmask = (n_lane * CMP_STRIDE + (CMP_LEN - 1) <= t_row) & (n_lane < n_cmp)
    cbias = jnp.where(cmask, 0.0, NEG)
    j_blk = lax.broadcasted_iota(jnp.int32, (n_slc, 1), 0)
    slc_shift = SLC_LEN.bit_length() - 1
    cur = jnp.right_shift(qi * tq + lax.broadcasted_iota(jnp.int32, (1, tq), 1), slc_shift)
    forced = (j_blk == 0) | (j_blk == cur) | (j_blk == cur - 1)
    w_start = pl.multiple_of(jnp.maximum(qi * tq - WIN, 0), tq)
    w_pos = w_start + lax.broadcasted_iota(jnp.int32, (1, band), 1)
    wbias = jnp.where((w_pos <= t_row) & (w_pos > t_row - WIN), 0.0, NEG)

    def group_queries(g):
        qg = q_ref[0, :, g * hg * HEAD_DIM:(g + 1) * hg * HEAD_DIM].astype(F32)
        qp = jnp.concatenate([qg[:, h * HEAD_DIM:(h + 1) * HEAD_DIM] for h in range(hg)], axis=0)
        return qp.astype(BF), _rope_lanes(qp, cos, sin).astype(BF)

    def compressed_branch(g, qp):
        kc = kvc_ref[g, 0].astype(BF)
        vc = kvc_ref[NSA_GROUPS + g, 0].astype(BF)
        s = _dot_t(qp, kc)
        ps = []
        for h in range(hg):
            u = s[h * tq:(h + 1) * tq] * scale + cbias
            e, l = _softmax_rows(u)
            ps.append(jnp.where(cmask, e / l, 0.0))
        o_cmp = jnp.dot(jnp.concatenate(ps, axis=0).astype(BF), vc, preferred_element_type=F32)

        p_sum = ps[0]
        for h in range(1, hg):
            p_sum = p_sum + ps[h]
        imp = lax.dot_general(m_ref[...], p_sum, (((1,), (1,)), ((), ())), preferred_element_type=F32,
                              precision=lax.Precision.HIGHEST)
        imp = jnp.where(j_blk > cur, -jnp.inf, jnp.where(forced, jnp.inf, imp))
        rank = jnp.zeros((n_slc, tq), jnp.int32)
        for i in range(n_slc):
            ci = imp[i:i + 1, :]
            beats = (ci > imp) | ((ci == imp) & (j_blk > i))
            rank = rank + beats.astype(jnp.int32)
        return o_cmp, jnp.where(rank < SLC_TOP, 0.0, NEG).astype(BF)

    qs = [group_queries(g) for g in range(NSA_GROUPS)]
    cmps = [compressed_branch(g, qs[g][0]) for g in range(NSA_GROUPS)]

    def attend(width):
        blocks = width // SLC_LEN
        expand = jnp.where(jnp.right_shift(lax.broadcasted_iota(jnp.int32, (blocks, width), 1), slc_shift)
                           == lax.broadcasted_iota(jnp.int32, (blocks, width), 0), 1.0, 0.0).astype(BF)
        s_pos = lax.broadcasted_iota(jnp.int32, (1, width), 1)
        lanes = [slice(g * HEAD_DIM, (g + 1) * HEAD_DIM) for g in range(NSA_GROUPS)]
        s_win = [_dot_t(qs[g][1], kw_ref[0, pl.ds(w_start, band), lanes[g]]) for g in range(NSA_GROUPS)]
        s_slc = [_dot_t(qs[g][1], ks_ref[0, :width, lanes[g]]) for g in range(NSA_GROUPS)]
        for g in range(NSA_GROUPS):
            o_win = _stacked_softmax_pv(s_win[g], wbias, vw_ref[0, pl.ds(w_start, band), lanes[g]], scale, hg, tq)
            o_cmp, sel_neg = cmps[g]
            bias = lax.dot_general(sel_neg[:blocks], expand, (((0,), (0,)), ((), ())),
                                   preferred_element_type=F32)
            bias = jnp.where(s_pos <= t_row, bias, NEG)
            o_slc = _stacked_softmax_pv(s_slc[g], bias, vs_ref[0, :width, lanes[g]], scale, hg, tq)
            for hh in range(hg):
                h = g * hg + hh
                rows = slice(hh * tq, (hh + 1) * tq)
                o = (o_cmp[rows] * gates[:, 3 * h:3 * h + 1]
                     + o_slc[rows] * gates[:, 3 * h + 1:3 * h + 2]
                     + o_win[rows] * gates[:, 3 * h + 2:3 * h + 3])
                o_ref[0, :, h * HEAD_DIM:(h + 1) * HEAD_DIM] = o.astype(o_ref.dtype)

    n_widths = seq // key_step
    super_row = qi // (key_step // tq)
    for k in range(n_widths):
        pl.when(super_row == k)(functools.partial(attend, (k + 1) * key_step))


def _nsa(main, kvc, planes, gate_plane, cos, sin, m_cs, *, batch, seq, tq=128, key_step=512):
    nq = seq // tq
    kv_w = NSA_GROUPS * HEAD_DIM
    kv0 = NSA_HEADS * HEAD_DIM // kv_w
    return pl.pallas_call(
        functools.partial(_nsa_kernel, tq=tq, seq=seq, key_step=key_step),
        grid=(batch, nq),
        in_specs=[
            pl.BlockSpec((1, tq, NSA_HEADS * HEAD_DIM), lambda b, i: (b, i, 0)),
            pl.BlockSpec((1, seq, kv_w), lambda b, i: (b, 0, kv0)),
            pl.BlockSpec((1, seq, kv_w), lambda b, i: (b, 0, kv0 + 2)),
            pl.BlockSpec((1, seq, kv_w), lambda b, i: (b, 0, kv0 + 1)),
            pl.BlockSpec((1, seq, kv_w), lambda b, i: (b, 0, kv0 + 3)),
            pl.BlockSpec((2 * NSA_GROUPS, 1, LANES, HEAD_DIM), lambda b, i: (0, b, 0, 0)),
            pl.BlockSpec((1, tq, LANES), lambda b, i: (gate_plane, b * nq + i, 0)),
            pl.BlockSpec((tq, LANES), lambda b, i: (i, 0)),
            pl.BlockSpec((tq, LANES), lambda b, i: (i, 0)),
            pl.BlockSpec(m_cs.shape, lambda b, i: (0, 0)),
        ],
        out_specs=pl.BlockSpec((1, tq, NSA_HEADS * HEAD_DIM), lambda b, i: (b, i, 0)),
        out_shape=jax.ShapeDtypeStruct((batch, seq, NSA_HEADS * HEAD_DIM), BF),
        compiler_params=_params(("parallel", "arbitrary")),
        name="nsa_attention",
    )(main, main, main, main, main, kvc, planes, cos, sin, m_cs)


def _cumdecay_kernel(x_ref, b_ref, o_ref):
    z = x_ref[0] + b_ref[...]
    c = jnp.minimum(z, 0.0) - jnp.log1p(jnp.exp(-jnp.abs(z)))
    rows = c.shape[0]
    r = lax.broadcasted_iota(jnp.int32, (rows, 1), 0)
    k = 1
    while k < rows:
        c = c + jnp.where(r >= k, pltpu.roll(c, k, axis=0), 0.0)
        k *= 2
    o_ref[0] = c


def _cumdecay(planes, plane, bias, *, batch, seq):
    return pl.pallas_call(
        _cumdecay_kernel,
        grid=(batch,),
        in_specs=[pl.BlockSpec((1, seq, LANES), lambda b: (plane, b, 0)),
                  pl.BlockSpec((1, LANES), lambda b: (0, 0))],
        out_specs=pl.BlockSpec((1, seq, LANES), lambda b: (b, 0, 0)),
        out_shape=jax.ShapeDtypeStruct((batch, seq, LANES), F32),
        compiler_params=_params(("arbitrary",)),
        name="fox_cumdecay",
    )(planes, bias)


def _attn_kernel(*refs, tq, seq, scale, n_qk, decay):
    q_ref = refs[0]
    k_refs = refs[1:1 + n_qk]
    v_ref = refs[1 + n_qk]
    k = 2 + n_qk
    if decay:
        ck_ref = refs[k]
        k += 1
    o_ref = refs[k]

    row = lax.broadcasted_iota(jnp.int32, (tq, 1), 0)
    col = lax.broadcasted_iota(jnp.int32, (1, tq), 1)
    tri = col <= row
    exp2_scale = scale * float(np.log2(np.e))

    k_all = k_refs[0][0] if n_qk == 1 else jnp.concatenate([k_ref[0] for k_ref in k_refs], axis=1)

    n_tiles = seq // tq
    scores = lambda i: _dot_t(q_ref[0, i * tq:(i + 1) * tq, :], k_all[:(i + 1) * tq])
    s_next = scores(n_tiles - 1)
    for i in reversed(range(n_tiles)):
        lo, hi = i * tq, (i + 1) * tq
        s = s_next
        if i > 0:
            s_next = scores(i - 1)
        if decay:
            ck = ck_ref[0, :, :hi]
            cq = jnp.sum(jnp.where(col == row, ck[:, lo:hi], 0.0), axis=1, keepdims=True)
            s = s * scale + (cq - ck)
            prob = lambda u, m: jnp.exp(u - m)
        else:
            prob = lambda u, m: jnp.exp2((u - m) * exp2_scale)
        s_diag = jnp.where(tri, s[:, lo:hi], NEG)
        m = jnp.max(s_diag, axis=-1, keepdims=True)
        if i > 0:
            m = jnp.maximum(m, jnp.max(s[:, :lo], axis=-1, keepdims=True))
        e_diag = prob(s_diag, m)
        l = jnp.sum(e_diag, axis=-1, keepdims=True)
        e = e_diag.astype(BF)
        if i > 0:
            e_off = prob(s[:, :lo], m)
            l = l + jnp.sum(e_off, axis=-1, keepdims=True)
            e = jnp.concatenate([e_off.astype(BF), e], axis=1)
        o = jnp.dot(e, v_ref[0, :hi, :], preferred_element_type=F32) / l
        o_ref[0, lo:hi, :] = o.astype(o_ref.dtype)


def _attention(q_arr, q_blk0, k_arrs, k_blk0s, k_per_head, v_arr, v_blk0, *, batch, seq, heads, scale,
               decay=None, tq=256, name="attention"):
    n_qk = len(k_arrs)
    in_specs = [pl.BlockSpec((1, seq, n_qk * HEAD_DIM), lambda b, h: (b, 0, q_blk0 + h))]
    args = [q_arr]
    for arr, blk0, per_head in zip(k_arrs, k_blk0s, k_per_head):
        if per_head:
            in_specs.append(pl.BlockSpec((1, seq, HEAD_DIM), lambda b, h, blk0=blk0: (b, 0, blk0 + h)))
        else:
            in_specs.append(pl.BlockSpec((1, seq, HEAD_DIM), lambda b, h, blk0=blk0: (b, 0, blk0)))
        args.append(arr)
    in_specs.append(pl.BlockSpec((1, seq, HEAD_DIM), lambda b, h: (b, 0, v_blk0 + h)))
    args.append(v_arr)
    if decay is not None:
        in_specs.append(pl.BlockSpec((1, 1, seq), lambda b, h: (b * heads + h, 0, 0)))
        args.append(decay)
    return pl.pallas_call(
        functools.partial(_attn_kernel, tq=tq, seq=seq, scale=scale, n_qk=n_qk, decay=decay is not None),
        grid=(batch, heads),
        in_specs=in_specs,
        out_specs=pl.BlockSpec((1, seq, HEAD_DIM), lambda b, h: (b, 0, h)),
        out_shape=jax.ShapeDtypeStruct((batch, seq, heads * HEAD_DIM), BF),
        compiler_params=_params(("parallel", "arbitrary")),
        name=name,
    )(*args)


def _mla_latent_kernel(x_ref, g_ref, wt_ref, wkr_ref, gq_ref, gkv_ref, cos_ref, sin_ref, cq_ref, ckv_ref, kr_ref):
    h = _rms_normed(x_ref[...], g_ref[...]).astype(BF)
    c = _dot_t(h, wt_ref[...].astype(BF))
    cq_ref[...] = _rms_normed(c[:, :MLA_Q_RANK], gq_ref[...]).astype(BF)
    ckv_ref[...] = _rms_normed(c[:, MLA_Q_RANK:], gkv_ref[...]).astype(BF)
    kr_ref[...] = _rope_lanes(_dot_t(h, wkr_ref[...].astype(BF)), cos_ref[...], sin_ref[...]).astype(BF)


def _mla_latent(x, g, wt, w_kr, gq, gkv, cos, sin, *, seq, tm=512):
    n = x.shape[0]
    n_seq = seq // tm
    return pl.pallas_call(
        _mla_latent_kernel,
        grid=(n // tm,),
        in_specs=[
            pl.BlockSpec((tm, x.shape[1]), lambda i: (i, 0)),
            pl.BlockSpec((1, x.shape[1]), lambda i: (0, 0)),
            pl.BlockSpec((MLA_Q_RANK + MLA_KV_RANK, wt.shape[1]), lambda i: (0, 0)),
            pl.BlockSpec(w_kr.shape, lambda i: (0, 0)),
            pl.BlockSpec((1, MLA_Q_RANK), lambda i: (0, 0)),
            pl.BlockSpec((1, MLA_KV_RANK), lambda i: (0, 0)),
            pl.BlockSpec((tm, LANES), lambda i: (i % n_seq, 0)),
            pl.BlockSpec((tm, LANES), lambda i: (i % n_seq, 0)),
        ],
        out_specs=[
            pl.BlockSpec((tm, MLA_Q_RANK), lambda i: (i, 0)),
            pl.BlockSpec((tm, MLA_KV_RANK), lambda i: (i, 0)),
            pl.BlockSpec((tm, LANES), lambda i: (i, 0)),
        ],
        out_shape=[
            jax.ShapeDtypeStruct((n, MLA_Q_RANK), BF),
            jax.ShapeDtypeStruct((n, MLA_KV_RANK), BF),
            jax.ShapeDtypeStruct((n, LANES), BF),
        ],
        compiler_params=_params(("parallel",)),
        name="mla_latent",
    )(x, g.reshape(1, -1), wt, w_kr, gq.reshape(1, -1), gkv.reshape(1, -1), cos, sin)


def _rope_tables(seq, dim):
    half = dim // 2
    inv = 1.0 / (ROPE_THETA ** (np.arange(0, dim, 2, dtype=np.float32) / dim))
    ang = np.arange(seq, dtype=np.float32)[:, None] * inv.astype(np.float32)[None, :]
    cos = np.zeros((seq, LANES), np.float32)
    sin = np.zeros((seq, LANES), np.float32)
    cos[:, :half] = np.cos(ang)
    cos[:, LANES // 2:LANES // 2 + half] = np.cos(ang)
    sin[:, :half] = -np.sin(ang)
    sin[:, LANES // 2:LANES // 2 + half] = np.sin(ang)
    return jnp.asarray(cos), jnp.asarray(sin)


def _cmp_to_slc(seq):
    n_cmp = (seq - CMP_LEN) // CMP_STRIDE + 1
    n_slc = seq // SLC_LEN
    c0 = np.arange(n_cmp) * CMP_STRIDE
    c1 = c0 + CMP_LEN
    s0 = np.arange(n_slc) * SLC_LEN
    s1 = s0 + SLC_LEN
    ov = np.clip(np.minimum(c1[:, None], s1[None, :]) - np.maximum(c0[:, None], s0[None, :]), 0, None)
    m = np.zeros((n_slc, LANES), np.float32)
    m[:, :n_cmp] = (ov / CMP_LEN).T
    return jnp.asarray(m)


def _spread_rope_cols(w):
    half = MLA_ROPE // 2
    z = jnp.zeros(w.shape[:-1] + (LANES // 2 - half,), w.dtype)
    return jnp.concatenate([w[..., :half], z, w[..., half:], z], axis=-1)


def _hybrid_mixer(x, g, w_in, w_out, layer, cmp_pe, cmp_w1, cmp_w2, f_bias, *, batch, seq):
    n = batch * seq
    nq_w = NSA_HEADS * HEAD_DIM
    kv_w = NSA_GROUPS * HEAD_DIM
    fx_w = FOX_HEADS * HEAD_DIM
    c_cmp, c_slc, c_gate = nq_w, nq_w + 2 * kv_w, nq_w + 6 * kv_w
    c_fox = c_gate + 3 * NSA_HEADS
    c_f = c_fox + 3 * fx_w
    wt = jnp.swapaxes(w_in[layer], 0, 1)

    cos, sin = _rope_tables(seq, HEAD_DIM)
    tn = 4 * kv_w
    main_starts = (0, c_slc) + tuple(range(c_fox, c_f, tn))
    main = _mm([x], [_W(wt, row_starts=main_starts)], len(main_starts) * tn, out_dtype=BF, tm=1024, tn=tn, norm_g=g,
               rope=(cos, sin, (nq_w // tn, nq_w // tn + 1), (True, True, False, False) * 2), seq=seq,
               name="proj_main")
    f_lane0 = LANES - FOX_HEADS
    plane_starts = tuple(range(c_cmp, c_slc, LANES)) + (c_gate, c_f - f_lane0)
    gate_plane, f_plane = len(plane_starts) - 2, len(plane_starts) - 1
    planes = _mm([x], [_W(wt, row_starts=plane_starts)], len(plane_starts) * LANES, out_dtype=F32, tm=1024, tn=LANES,
                 norm_g=g, stacked_out=True, name="proj_planes")

    rows16 = seq // CMP_STRIDE
    kvc = _compress(planes.reshape(len(plane_starts), batch, rows16, CMP_STRIDE * HEAD_DIM),
                    cmp_pe.reshape(2, 2, CMP_STRIDE * HEAD_DIM), cmp_w1, cmp_w2, batch)

    main3 = main.reshape(batch, seq, -1)
    o_nsa = _nsa(main3, kvc, planes, gate_plane, cos, sin, _cmp_to_slc(seq), batch=batch, seq=seq)

    bias = jnp.zeros((1, LANES), F32).at[0, f_lane0:].set(f_bias)
    cum = _cumdecay(planes, f_plane, bias, batch=batch, seq=seq)[:, :, f_lane0:]
    cum_t = jnp.swapaxes(cum, 1, 2).reshape(batch * FOX_HEADS, seq)
    blk = lambda col: col // HEAD_DIM
    c_qf = nq_w + 4 * kv_w
    o_fox = _attention(main3, blk(c_qf), [main3], [blk(c_qf + fx_w)], [True], main3, blk(c_qf + 2 * fx_w),
                       batch=batch, seq=seq, heads=FOX_HEADS, scale=HEAD_DIM ** -0.5,
                       decay=cum_t[:, None, :], name="fox_attention")

    return _mm([o_nsa.reshape(n, nq_w), o_fox.reshape(n, fx_w)],
               [_W(w_out, layer=layer, rows=nq_w, row_blk=0), _W(w_out, layer=layer, rows=fx_w, row_blk=1)],
               D_MODEL, out_dtype=F32, tm=512, tn=D_MODEL, res=x, name="proj_out_hybrid")


def _mla_mixer(x, g, w_in, q_norm, kv_norm, w_uq, w_ukv, w_out, layer, *, batch, seq):
    n = batch * seq
    qk = MLA_NOPE + MLA_ROPE
    wt_lat = jnp.swapaxes(w_in, 0, 1)
    w_kr = jnp.swapaxes(_spread_rope_cols(w_in[:, MLA_Q_RANK + MLA_KV_RANK:]), 0, 1)
    wq = w_uq.reshape(MLA_Q_RANK, MLA_HEADS, qk)
    wq = jnp.concatenate([wq[..., :MLA_NOPE], _spread_rope_cols(wq[..., MLA_NOPE:])], axis=-1)
    wq = wq.reshape(MLA_Q_RANK, MLA_HEADS * 2 * HEAD_DIM)
    wkv = w_ukv.reshape(MLA_KV_RANK, MLA_HEADS, MLA_NOPE + MLA_V)
    wkv = jnp.concatenate([wkv[..., :MLA_NOPE].reshape(MLA_KV_RANK, -1),
                           wkv[..., MLA_NOPE:].reshape(MLA_KV_RANK, -1)], axis=1)

    cos, sin = _rope_tables(seq, MLA_ROPE)
    cq, ckv, kr = _mla_latent(x, g, wt_lat, w_kr, q_norm, kv_norm, cos, sin, seq=seq)
    tn = wq.shape[1]
    q = _mm([cq], [_W(wq)], wq.shape[1], out_dtype=BF, tm=512, tn=tn,
            rope=(cos, sin, (0, 1), (False, True) * (tn // (2 * LANES))), seq=seq, name="proj_mla_q")
    kv = _mm([ckv], [_W(wkv)], wkv.shape[1], out_dtype=BF, tm=512, tn=tn, name="proj_mla_kv")

    q3 = q.reshape(batch, seq, -1)
    kv3 = kv.reshape(batch, seq, -1)
    kr3 = kr.reshape(batch, seq, LANES)
    o = _attention(q3, 0, [kv3, kr3], [0, 0], [True, False], kv3, MLA_HEADS, batch=batch, seq=seq,
                   heads=MLA_HEADS, scale=qk ** -0.5, name="mla_attention")
    return _mm([o.reshape(n, MLA_HEADS * MLA_V)], [_W(w_out, layer=layer)], D_MODEL, out_dtype=F32, tm=512, tn=D_MODEL,
               res=x, name="proj_out_mla")


def _ffn(x, norms, w_in, w_out, layer):
    depth, d = norms.shape
    a, w_out_bf = _ffn_up(x, norms.reshape(depth, 1, d), w_in, w_out, layer)
    return _mm([a], [_W(w_out_bf)], d, out_dtype=F32, tm=1024, tn=512, res=x, name="ffn_down")


def kernel(x, ffn1_norm, ffn1_w_in, ffn1_w_out, mix_norm, ffn2_norm, ffn2_w_in, ffn2_w_out, hyb_w_in, hyb_w_out, nsa_cmp_pe, nsa_cmp_w1, nsa_cmp_w2, fox_f_bias, mla_w_in, mla_q_norm, mla_kv_norm, mla_w_uq, mla_w_ukv, mla_w_out, final_norm):
    batch, seq, d = x.shape
    depth = ffn1_norm.shape[0]
    x = x.reshape(batch * seq, d)
    for i in range(depth):
        x = _ffn(x, ffn1_norm, ffn1_w_in, ffn1_w_out, i)
        if i % 2 == 0:
            e = i // 2
            x = _hybrid_mixer(x, mix_norm[i], hyb_w_in, hyb_w_out, e, nsa_cmp_pe[e], nsa_cmp_w1[e], nsa_cmp_w2[e],
                              fox_f_bias[e], batch=batch, seq=seq)
        else:
            o = i // 2
            x = _mla_mixer(x, mix_norm[i], mla_w_in[o], mla_q_norm[o], mla_kv_norm[o], mla_w_uq[o], mla_w_ukv[o],
                           mla_w_out, o, batch=batch, seq=seq)
        x = _ffn(x, ffn2_norm, ffn2_w_in, ffn2_w_out, i)
    return _rmsnorm(x, final_norm).reshape(batch, seq, d)
```

```python
import functools

import numpy as np
import jax
import jax.numpy as jnp
from jax import lax
from jax.experimental import pallas as pl
from jax.experimental.pallas import tpu as pltpu

D_MODEL = 2048
HEAD_DIM = 128
ROPE_THETA = 10000.0
NORM_EPS = 1e-6
NEG = -1e30

NSA_HEADS = 8
NSA_GROUPS = 2
HEADS_PER_GROUP = NSA_HEADS // NSA_GROUPS
CMP_LEN = 32
CMP_STRIDE = 16
CMP_HIDDEN = 256
SLC_LEN = 64
SLC_TOP = 8
WIN = 512
FOX_HEADS = 8
MLA_HEADS = 16
MLA_Q_RANK = 512
MLA_KV_RANK = 512
MLA_NOPE = 128
MLA_ROPE = 64
MLA_V = 128
D_FF = 5632

LANES = 128
VMEM_LIMIT = 56 * 1024 * 1024

BF = jnp.bfloat16
F32 = jnp.float32


def _params(sem, vmem=VMEM_LIMIT):
    return pltpu.CompilerParams(dimension_semantics=sem, vmem_limit_bytes=vmem)


def _rope_lanes(x, cos, sin):
    return x * cos + pltpu.roll(x, LANES // 2, axis=1) * sin


def _dot_t(a, b):
    return lax.dot_general(a, b, (((1,), (1,)), ((), ())), preferred_element_type=F32)


def _softmax_rows(s):
    m = jnp.max(s, axis=-1, keepdims=True)
    e = jnp.exp(s - m)
    return e, jnp.sum(e, axis=-1, keepdims=True)


def _rms_normed(x, g):
    ms = jnp.mean(x * x, axis=-1, keepdims=True)
    return x * lax.rsqrt(ms + NORM_EPS) * g


def _ffn_up_kernel(x_ref, g_ref, wa_ref, wb_ref, w2_ref, o_ref, w2b_ref, xn_sc, *, n_cast):
    i, j = pl.program_id(0), pl.program_id(1)

    @pl.when(j == 0)
    def _():
        xn_sc[...] = _rms_normed(x_ref[...], g_ref[...]).astype(BF)

    @pl.when(i * pl.num_programs(1) + j < n_cast)
    def _():
        w2b_ref[...] = w2_ref[...].astype(BF)

    xn = xn_sc[...]
    h1 = jnp.dot(xn, wa_ref[...].astype(BF), preferred_element_type=F32)
    h2 = jnp.dot(xn, wb_ref[...].astype(BF), preferred_element_type=F32)
    o_ref[...] = (h1 * jax.nn.sigmoid(h1) * h2 * 0.5).astype(BF)


def _ffn_up(x, g, w_in, w_out, layer, *, tm=1024, tf=512, cast_rows=256):
    n, d = x.shape
    n_f = D_FF // tf
    n_cast = D_FF // cast_rows
    assert n_cast <= (n // tm) * n_f
    chunk = lambda i, j: jnp.minimum(i * n_f + j, n_cast - 1)
    return pl.pallas_call(
        functools.partial(_ffn_up_kernel, n_cast=n_cast),
        grid=(n // tm, n_f),
        in_specs=[
            pl.BlockSpec((tm, d), lambda i, j: (i, 0)),
            pl.BlockSpec((None, 1, d), lambda i, j: (layer, 0, 0)),
            pl.BlockSpec((None, d, tf), lambda i, j: (layer, 0, j)),
            pl.BlockSpec((None, d, tf), lambda i, j: (layer, 0, n_f + j)),
            pl.BlockSpec((None, cast_rows, d), lambda i, j: (layer, chunk(i, j), 0)),
        ],
        out_specs=[pl.BlockSpec((tm, tf), lambda i, j: (i, j)),
                   pl.BlockSpec((cast_rows, d), lambda i, j: (chunk(i, j), 0))],
        out_shape=[jax.ShapeDtypeStruct((n, D_FF), BF), jax.ShapeDtypeStruct((D_FF, d), BF)],
        scratch_shapes=[pltpu.VMEM((tm, d), BF)],
        compiler_params=_params(("arbitrary", "arbitrary")),
        name="ffn_up",
    )(x, g, w_in, w_in, w_out)


def _norm_kernel(x_ref, g_ref, o_ref):
    o_ref[...] = _rms_normed(x_ref[...], g_ref[...]).astype(o_ref.dtype)


def _rmsnorm(x, g, *, tm=512):
    n, d = x.shape
    return pl.pallas_call(
        _norm_kernel,
        grid=(n // tm,),
        in_specs=[pl.BlockSpec((tm, d), lambda i: (i, 0)), pl.BlockSpec((1, d), lambda i: (0, 0))],
        out_specs=pl.BlockSpec((tm, d), lambda i: (i, 0)),
        out_shape=jax.ShapeDtypeStruct((n, d), x.dtype),
        compiler_params=_params(("parallel",)),
        name="rmsnorm",
    )(x, g.reshape(1, d))


class _W:
    def __init__(self, arr, *, layer=None, rows=None, row_blk=0, col_blk0=0, row_starts=None):
        self.arr, self.layer, self.row_blk, self.col_blk0 = arr, layer, row_blk, col_blk0
        self.transposed = row_starts is not None
        self.row_starts = row_starts
        assert row_starts is None or all(r % 8 == 0 for r in row_starts)
        self.rows = rows if rows is not None else arr.shape[-2]

    def spec(self, tn, single_buffer=False):
        layer, row_blk, col_blk0 = self.layer, self.row_blk, self.col_blk0
        if self.transposed:
            starts = self.row_starts

            def start(j):
                off = starts[-1]
                for k in range(len(starts) - 2, -1, -1):
                    off = jnp.where(j == k, starts[k], off)
                return pl.multiple_of(off, 8)

            return pl.BlockSpec((pl.Element(tn), pl.Element(self.arr.shape[1])), lambda i, j: (start(j), 0))
        kw = dict(pipeline_mode=pl.Buffered(1)) if single_buffer else {}
        if self.arr.ndim == 3:
            return pl.BlockSpec((None, self.rows, tn), lambda i, j: (layer, row_blk, col_blk0 + j), **kw)
        return pl.BlockSpec((self.rows, tn), lambda i, j: (row_blk, col_blk0 + j), **kw)


def _mm_kernel(*refs, n_a, has_res, rope_j, rope_chunks):
    a_refs = refs[:n_a]
    w_refs = refs[n_a:2 * n_a]
    k = 2 * n_a
    if has_res:
        res_ref = refs[k]
        k += 1
    if rope_chunks is not None:
        cos_ref, sin_ref = refs[k], refs[k + 1]
        k += 2
    o_ref = refs[k]

    acc = jnp.dot(a_refs[0][...], w_refs[0][...].astype(BF), preferred_element_type=F32)
    for a_ref, w_ref in zip(a_refs[1:], w_refs[1:]):
        acc += jnp.dot(a_ref[...], w_ref[...].astype(BF), preferred_element_type=F32)
    if has_res:
        acc += res_ref[...]

    if rope_chunks is None:
        o_ref[...] = acc.astype(o_ref.dtype)
    else:
        j = pl.program_id(1)
        is_rope = (j >= rope_j[0]) & (j < rope_j[1])
        cos, sin = cos_ref[...], sin_ref[...]
        for c, on in enumerate(rope_chunks):
            blk = acc[:, c * LANES:(c + 1) * LANES]
            if on:
                blk = jnp.where(is_rope, _rope_lanes(blk, cos, sin), blk)
            o_ref[:, c * LANES:(c + 1) * LANES] = blk.astype(o_ref.dtype)


def _mm(a_list, w_list, n_out, *, out_dtype, tm, tn, res=None, rope=None, seq=None, name="mm"):
    n = a_list[0].shape[0]
    n_a = len(a_list)
    in_specs, args = [], []
    for a in a_list:
        in_specs.append(pl.BlockSpec((tm, a.shape[1]), lambda i, j: (i, 0)))
        args.append(a)
    for w in w_list:
        assert not w.transposed
        in_specs.append(w.spec(tn, single_buffer=tn == n_out))
        args.append(w.arr)
    if res is not None:
        in_specs.append(pl.BlockSpec((tm, tn), lambda i, j: (i, j)))
        args.append(res)
    rope_j = rope_chunks = None
    if rope is not None:
        cos, sin, rope_j, rope_chunks = rope
        n_seq = seq // tm
        for t in (cos, sin):
            in_specs.append(pl.BlockSpec((tm, LANES), lambda i, j: (i % n_seq, 0)))
            args.append(t)
    return pl.pallas_call(
        functools.partial(_mm_kernel, n_a=n_a, has_res=res is not None, rope_j=rope_j, rope_chunks=rope_chunks),
        grid=(n // tm, n_out // tn),
        in_specs=in_specs,
        out_specs=pl.BlockSpec((tm, tn), lambda i, j: (i, j)),
        out_shape=jax.ShapeDtypeStruct((n, n_out), out_dtype),
        compiler_params=_params(("parallel", "arbitrary")),
        name=name,
    )(*args)


def _hybrid_proj_kernel(x_ref, g_ref, w_ref, *rest, n_main, n_side, rope_j, rope_chunks):
    side_refs = rest[:n_side]
    cos_ref, sin_ref, o_ref, p_ref, xn_sc = rest[n_side:]
    j = pl.program_id(1)

    @pl.when(j == 0)
    def _():
        xn_sc[...] = _rms_normed(x_ref[...], g_ref[...]).astype(BF)

    @pl.when(j < n_main)
    def _():
        acc = _dot_t(xn_sc[...], w_ref[...].astype(BF))
        is_rope = (j >= rope_j[0]) & (j < rope_j[1])
        cos, sin = cos_ref[...], sin_ref[...]
        for c, on in enumerate(rope_chunks):
            blk = acc[:, c * LANES:(c + 1) * LANES]
            if on:
                blk = jnp.where(is_rope, _rope_lanes(blk, cos, sin), blk)
            o_ref[:, c * LANES:(c + 1) * LANES] = blk.astype(o_ref.dtype)

    @pl.when(j == n_main)
    def _():
        w_side = jnp.concatenate([w[...] for w in side_refs], axis=0).astype(BF)
        s = _dot_t(xn_sc[...], w_side)
        for p in range(s.shape[1] // LANES):
            p_ref[p] = s[:, p * LANES:(p + 1) * LANES]


def _hybrid_proj(x, g, wt, main_starts, side_windows, rope, *, seq, tm=1024, tn=512):
    n, d = x.shape
    cos, sin, rope_j, rope_chunks = rope
    n_main = len(main_starts)
    n_planes = sum(rows for _, rows in side_windows) // LANES
    n_seq = seq // tm
    last = n_main - 1
    side_specs = [pl.BlockSpec((pl.Element(rows), pl.Element(d)), lambda i, j, start=start: (start, 0),
                               pipeline_mode=pl.Buffered(1)) for start, rows in side_windows]
    return pl.pallas_call(
        functools.partial(_hybrid_proj_kernel, n_main=n_main, n_side=len(side_windows), rope_j=rope_j,
                          rope_chunks=rope_chunks),
        grid=(n // tm, n_main + 1),
        in_specs=[
            pl.BlockSpec((tm, d), lambda i, j: (i, 0)),
            pl.BlockSpec((1, d), lambda i, j: (0, 0)),
            _W(wt, row_starts=main_starts).spec(tn),
            *side_specs,
            pl.BlockSpec((tm, LANES), lambda i, j: (i % n_seq, 0)),
            pl.BlockSpec((tm, LANES), lambda i, j: (i % n_seq, 0)),
        ],
        out_specs=[pl.BlockSpec((tm, tn), lambda i, j: (i, jnp.minimum(j, last))),
                   pl.BlockSpec((n_planes, tm, LANES), lambda i, j: (0, i, 0))],
        out_shape=[jax.ShapeDtypeStruct((n, n_main * tn), BF), jax.ShapeDtypeStruct((n_planes, n, LANES), F32)],
        scratch_shapes=[pltpu.VMEM((tm, d), BF)],
        compiler_params=_params(("parallel", "arbitrary")),
        name="proj_hybrid",
    )(x, g.reshape(1, d), wt, *([wt] * len(side_windows)), cos, sin)


def _compress_kernel(x_ref, pe_ref, w1_ref, w2_ref, o_ref):
    n_blk = o_ref.shape[2]
    ha = jnp.zeros((n_blk, CMP_HIDDEN), F32)
    hb = jnp.zeros((n_blk, CMP_HIDDEN), F32)
    for l in range(CMP_STRIDE):
        xl = x_ref[0, pl.ds(l, n_blk, stride=CMP_STRIDE), :]
        wa = w1_ref[0, l * HEAD_DIM:(l + 1) * HEAD_DIM, :].astype(BF)
        wb = w1_ref[0, (CMP_STRIDE + l) * HEAD_DIM:(CMP_STRIDE + l + 1) * HEAD_DIM, :].astype(BF)
        ha += jnp.dot((xl + pe_ref[0, l:l + 1, :]).astype(BF), wa, preferred_element_type=F32)
        hb += jnp.dot((xl + pe_ref[0, CMP_STRIDE + l:CMP_STRIDE + l + 1, :]).astype(BF), wb,
                      preferred_element_type=F32)
    hid = ha + pltpu.roll(hb, n_blk - 1, axis=0)
    act = jax.nn.gelu(hid, approximate=True).astype(BF)
    o_ref[0, 0] = jnp.dot(act, w2_ref[0].astype(BF), preferred_element_type=F32)


def _compress(planes, pe, w1, w2, *, batch, seq):
    rows = seq // CMP_STRIDE
    return pl.pallas_call(
        _compress_kernel,
        grid=(2 * NSA_GROUPS, batch),
        in_specs=[
            pl.BlockSpec((1, seq, HEAD_DIM), lambda c, b: (c, b, 0)),
            pl.BlockSpec((1, CMP_LEN, HEAD_DIM), lambda c, b: (c // NSA_GROUPS, 0, 0)),
            pl.BlockSpec((1, CMP_LEN * HEAD_DIM, CMP_HIDDEN), lambda c, b: (c // NSA_GROUPS, 0, 0)),
            pl.BlockSpec((1, CMP_HIDDEN, HEAD_DIM), lambda c, b: (c // NSA_GROUPS, 0, 0)),
        ],
        out_specs=pl.BlockSpec((1, 1, rows, HEAD_DIM), lambda c, b: (c, b, 0, 0)),
        out_shape=jax.ShapeDtypeStruct((2 * NSA_GROUPS, batch, rows, HEAD_DIM), F32),
        compiler_params=_params(("arbitrary", "arbitrary")),
        name="nsa_compress",
    )(planes, pe, w1, w2)


def _stacked_softmax_pv(s_raw, bias, v, scale, n_stack, tq):
    exp2_scale = scale * float(np.log2(np.e))
    es, ls = [], []
    for h in range(n_stack):
        u = s_raw[h * tq:(h + 1) * tq] + bias
        e = jnp.exp2((u - jnp.max(u, axis=-1, keepdims=True)) * exp2_scale)
        ls.append(jnp.sum(e, axis=-1, keepdims=True))
        es.append(e.astype(BF))
    o = jnp.dot(jnp.concatenate(es, axis=0), v, preferred_element_type=F32)
    return o / jnp.concatenate(ls, axis=0)


def _nsa_kernel(q_ref, ks_ref, kw_ref, vs_ref, vw_ref, kvc_ref, gate_ref, cos_ref, sin_ref, m_ref, o_ref,
                *, tq, seq, key_step):
    qi = pl.program_id(1)
    hg = HEADS_PER_GROUP
    scale = HEAD_DIM ** -0.5
    n_cmp = (seq - CMP_LEN) // CMP_STRIDE + 1
    n_slc = seq // SLC_LEN
    band = WIN + tq
    neg_bf = jnp.asarray(NEG, BF)

    t_row = qi * tq + lax.broadcasted_iota(jnp.int32, (tq, 1), 0)
    cos = jnp.concatenate([cos_ref[...]] * hg, axis=0)
    sin = jnp.concatenate([sin_ref[...]] * hg, axis=0)
    gates = jax.nn.sigmoid(gate_ref[0])

    n_lane = lax.broadcasted_iota(jnp.int32, (1, LANES), 1)
    cmask = (n_lane * CMP_STRIDE + (CMP_LEN - 1) <= t_row) & (n_lane < n_cmp)
    cbias = jnp.where(cmask, 0.0, NEG)
    j_blk = lax.broadcasted_iota(jnp.int32, (n_slc, 1), 0)
    slc_shift = SLC_LEN.bit_length() - 1
    cur = jnp.right_shift(qi * tq + lax.broadcasted_iota(jnp.int32, (1, tq), 1), slc_shift)
    forced = (j_blk == 0) | (j_blk == cur) | (j_blk == cur - 1)
    w_start = pl.multiple_of(jnp.maximum(qi * tq - WIN, 0), tq)
    w_pos = w_start + lax.broadcasted_iota(jnp.int32, (1, band), 1)
    wbias = jnp.where((w_pos <= t_row) & (w_pos > t_row - WIN), 0.0, NEG)

    def group_queries(g):
        qg = q_ref[0, :, g * hg * HEAD_DIM:(g + 1) * hg * HEAD_DIM].astype(F32)
        qp = jnp.concatenate([qg[:, h * HEAD_DIM:(h + 1) * HEAD_DIM] for h in range(hg)], axis=0)
        return qp.astype(BF), _rope_lanes(qp, cos, sin).astype(BF)

    def compressed_branch(g, qp):
        kc = kvc_ref[g, 0].astype(BF)
        vc = kvc_ref[NSA_GROUPS + g, 0].astype(BF)
        s = _dot_t(qp, kc)
        ps = []
        for h in range(hg):
            u = s[h * tq:(h + 1) * tq] * scale + cbias
            e, l = _softmax_rows(u)
            ps.append(jnp.where(cmask, e / l, 0.0))
        o_cmp = jnp.dot(jnp.concatenate(ps, axis=0).astype(BF), vc, preferred_element_type=F32)

        p_sum = ps[0]
        for h in range(1, hg):
            p_sum = p_sum + ps[h]
        imp = lax.dot_general(m_ref[...], p_sum, (((1,), (1,)), ((), ())), preferred_element_type=F32,
                              precision=lax.Precision.HIGHEST)
        imp = jnp.where(j_blk > cur, -jnp.inf, jnp.where(forced, jnp.inf, imp))
        rank = jnp.zeros((n_slc, tq), jnp.int32)
        for i in range(n_slc):
            ci = imp[i:i + 1, :]
            beats = (ci > imp) | ((ci == imp) & (j_blk > i))
            rank = rank + beats.astype(jnp.int32)
        return o_cmp, jnp.where(rank < SLC_TOP, 0.0, NEG).astype(BF)

    qs = [group_queries(g) for g in range(NSA_GROUPS)]
    cmps = [compressed_branch(g, qs[g][0]) for g in range(NSA_GROUPS)]

    def attend(width):
        blocks = width // SLC_LEN
        expand = jnp.where(jnp.right_shift(lax.broadcasted_iota(jnp.int32, (blocks, width), 1), slc_shift)
                           == lax.broadcasted_iota(jnp.int32, (blocks, width), 0), 1.0, 0.0).astype(BF)
        s_pos = lax.broadcasted_iota(jnp.int32, (1, width), 1)
        lanes = [slice(g * HEAD_DIM, (g + 1) * HEAD_DIM) for g in range(NSA_GROUPS)]
        s_win = [_dot_t(qs[g][1], kw_ref[0, pl.ds(w_start, band), lanes[g]]) for g in range(NSA_GROUPS)]
        s_slc = [_dot_t(qs[g][1], ks_ref[0, :width, lanes[g]]) for g in range(NSA_GROUPS)]
        for g in range(NSA_GROUPS):
            o_win = _stacked_softmax_pv(s_win[g], wbias, vw_ref[0, pl.ds(w_start, band), lanes[g]], scale, hg, tq)
            o_cmp, sel_neg = cmps[g]
            bias = lax.dot_general(sel_neg[:blocks], expand, (((0,), (0,)), ((), ())),
                                   preferred_element_type=F32)
            bias = jnp.where(s_pos <= t_row, bias, NEG)
            o_slc = _stacked_softmax_pv(s_slc[g], bias, vs_ref[0, :width, lanes[g]], scale, hg, tq)
            for hh in range(hg):
                h = g * hg + hh
                rows = slice(hh * tq, (hh + 1) * tq)
                o = (o_cmp[rows] * gates[:, 3 * h:3 * h + 1]
                     + o_slc[rows] * gates[:, 3 * h + 1:3 * h + 2]
                     + o_win[rows] * gates[:, 3 * h + 2:3 * h + 3])
                o_ref[0, :, h * HEAD_DIM:(h + 1) * HEAD_DIM] = o.astype(o_ref.dtype)

    n_widths = seq // key_step
    super_row = qi // (key_step // tq)
    for k in range(n_widths):
        pl.when(super_row == k)(functools.partial(attend, (k + 1) * key_step))


def _nsa(main, kvc, planes, gate_plane, cos, sin, m_cs, *, batch, seq, tq=128, key_step=512):
    nq = seq // tq
    kv_w = NSA_GROUPS * HEAD_DIM
    kv0 = NSA_HEADS * HEAD_DIM // kv_w
    return pl.pallas_call(
        functools.partial(_nsa_kernel, tq=tq, seq=seq, key_step=key_step),
        grid=(batch, nq),
        in_specs=[
            pl.BlockSpec((1, tq, NSA_HEADS * HEAD_DIM), lambda b, i: (b, i, 0)),
            pl.BlockSpec((1, seq, kv_w), lambda b, i: (b, 0, kv0)),
            pl.BlockSpec((1, seq, kv_w), lambda b, i: (b, 0, kv0 + 2)),
            pl.BlockSpec((1, seq, kv_w), lambda b, i: (b, 0, kv0 + 1)),
            pl.BlockSpec((1, seq, kv_w), lambda b, i: (b, 0, kv0 + 3)),
            pl.BlockSpec((2 * NSA_GROUPS, 1, LANES, HEAD_DIM), lambda b, i: (0, b, 0, 0)),
            pl.BlockSpec((1, tq, LANES), lambda b, i: (gate_plane, b * nq + i, 0)),
            pl.BlockSpec((tq, LANES), lambda b, i: (i, 0)),
            pl.BlockSpec((tq, LANES), lambda b, i: (i, 0)),
            pl.BlockSpec(m_cs.shape, lambda b, i: (0, 0)),
        ],
        out_specs=pl.BlockSpec((1, tq, NSA_HEADS * HEAD_DIM), lambda b, i: (b, i, 0)),
        out_shape=jax.ShapeDtypeStruct((batch, seq, NSA_HEADS * HEAD_DIM), BF),
        compiler_params=_params(("parallel", "arbitrary")),
        name="nsa_attention",
    )(main, main, main, main, main, kvc, planes, cos, sin, m_cs)


def _cumdecay_kernel(x_ref, b_ref, o_ref):
    z = x_ref[0] + b_ref[...]
    c = jnp.minimum(z, 0.0) - jnp.log1p(jnp.exp(-jnp.abs(z)))
    rows = c.shape[0]
    r = lax.broadcasted_iota(jnp.int32, (rows, 1), 0)
    k = 1
    while k < rows:
        c = c + jnp.where(r >= k, pltpu.roll(c, k, axis=0), 0.0)
        k *= 2
    o_ref[0] = c


def _cumdecay(planes, plane, bias, *, batch, seq):
    return pl.pallas_call(
        _cumdecay_kernel,
        grid=(batch,),
        in_specs=[pl.BlockSpec((1, seq, LANES), lambda b: (plane, b, 0)),
                  pl.BlockSpec((1, LANES), lambda b: (0, 0))],
        out_specs=pl.BlockSpec((1, seq, LANES), lambda b: (b, 0, 0)),
        out_shape=jax.ShapeDtypeStruct((batch, seq, LANES), F32),
        compiler_params=_params(("arbitrary",)),
        name="fox_cumdecay",
    )(planes, bias)


def _attn_kernel(*refs, tq, seq, scale, n_qk, decay):
    q_ref = refs[0]
    k_refs = refs[1:1 + n_qk]
    v_ref = refs[1 + n_qk]
    k = 2 + n_qk
    if decay:
        ck_ref = refs[k]
        k += 1
    o_ref = refs[k]

    row = lax.broadcasted_iota(jnp.int32, (tq, 1), 0)
    col = lax.broadcasted_iota(jnp.int32, (1, tq), 1)
    tri = col <= row
    exp2_scale = scale * float(np.log2(np.e))

    k_all = k_refs[0][0] if n_qk == 1 else jnp.concatenate([k_ref[0] for k_ref in k_refs], axis=1)

    n_tiles = seq // tq
    scores = lambda i: _dot_t(q_ref[0, i * tq:(i + 1) * tq, :], k_all[:(i + 1) * tq])
    s_next = scores(n_tiles - 1)
    for i in reversed(range(n_tiles)):
        lo, hi = i * tq, (i + 1) * tq
        s = s_next
        if i > 0:
            s_next = scores(i - 1)
        if decay:
            ck = ck_ref[0, :, :hi]
            cq = jnp.sum(jnp.where(col == row, ck[:, lo:hi], 0.0), axis=1, keepdims=True)
            s = s * scale + (cq - ck)
            prob = lambda u, m: jnp.exp(u - m)
        else:
            prob = lambda u, m: jnp.exp2((u - m) * exp2_scale)
        s_diag = jnp.where(tri, s[:, lo:hi], NEG)
        m = jnp.max(s_diag, axis=-1, keepdims=True)
        if i > 0:
            m = jnp.maximum(m, jnp.max(s[:, :lo], axis=-1, keepdims=True))
        e_diag = prob(s_diag, m)
        l = jnp.sum(e_diag, axis=-1, keepdims=True)
        e = e_diag.astype(BF)
        if i > 0:
            e_off = prob(s[:, :lo], m)
            l = l + jnp.sum(e_off, axis=-1, keepdims=True)
            e = jnp.concatenate([e_off.astype(BF), e], axis=1)
        o = jnp.dot(e, v_ref[0, :hi, :], preferred_element_type=F32) / l
        o_ref[0, lo:hi, :] = o.astype(o_ref.dtype)


def _attention(q_arr, q_blk0, k_arrs, k_blk0s, k_per_head, v_arr, v_blk0, *, batch, seq, heads, scale,
               decay=None, tq=256, name="attention"):
    n_qk = len(k_arrs)
    in_specs = [pl.BlockSpec((1, seq, n_qk * HEAD_DIM), lambda b, h: (b, 0, q_blk0 + h))]
    args = [q_arr]
    for arr, blk0, per_head in zip(k_arrs, k_blk0s, k_per_head):
        if per_head:
            in_specs.append(pl.BlockSpec((1, seq, HEAD_DIM), lambda b, h, blk0=blk0: (b, 0, blk0 + h)))
        else:
            in_specs.append(pl.BlockSpec((1, seq, HEAD_DIM), lambda b, h, blk0=blk0: (b, 0, blk0)))
        args.append(arr)
    in_specs.append(pl.BlockSpec((1, seq, HEAD_DIM), lambda b, h: (b, 0, v_blk0 + h)))
    args.append(v_arr)
    if decay is not None:
        in_specs.append(pl.BlockSpec((1, 1, seq), lambda b, h: (b * heads + h, 0, 0)))
        args.append(decay)
    return pl.pallas_call(
        functools.partial(_attn_kernel, tq=tq, seq=seq, scale=scale, n_qk=n_qk, decay=decay is not None),
        grid=(batch, heads),
        in_specs=in_specs,
        out_specs=pl.BlockSpec((1, seq, HEAD_DIM), lambda b, h: (b, 0, h)),
        out_shape=jax.ShapeDtypeStruct((batch, seq, heads * HEAD_DIM), BF),
        compiler_params=_params(("parallel", "arbitrary")),
        name=name,
    )(*args)


def _mla_latent_kernel(x_ref, g_ref, wt_ref, wkr_ref, gq_ref, gkv_ref, cos_ref, sin_ref, cq_ref, ckv_ref, kr_ref):
    h = _rms_normed(x_ref[...], g_ref[...]).astype(BF)
    c = _dot_t(h, wt_ref[...].astype(BF))
    cq_ref[...] = _rms_normed(c[:, :MLA_Q_RANK], gq_ref[...]).astype(BF)
    ckv_ref[...] = _rms_normed(c[:, MLA_Q_RANK:], gkv_ref[...]).astype(BF)
    kr_ref[...] = _rope_lanes(_dot_t(h, wkr_ref[...].astype(BF)), cos_ref[...], sin_ref[...]).astype(BF)


def _mla_latent(x, g, wt, w_kr, gq, gkv, cos, sin, *, seq, tm=512):
    n = x.shape[0]
    n_seq = seq // tm
    return pl.pallas_call(
        _mla_latent_kernel,
        grid=(n // tm,),
        in_specs=[
            pl.BlockSpec((tm, x.shape[1]), lambda i: (i, 0)),
            pl.BlockSpec((1, x.shape[1]), lambda i: (0, 0)),
            pl.BlockSpec((MLA_Q_RANK + MLA_KV_RANK, wt.shape[1]), lambda i: (0, 0)),
            pl.BlockSpec(w_kr.shape, lambda i: (0, 0)),
            pl.BlockSpec((1, MLA_Q_RANK), lambda i: (0, 0)),
            pl.BlockSpec((1, MLA_KV_RANK), lambda i: (0, 0)),
            pl.BlockSpec((tm, LANES), lambda i: (i % n_seq, 0)),
            pl.BlockSpec((tm, LANES), lambda i: (i % n_seq, 0)),
        ],
        out_specs=[
            pl.BlockSpec((tm, MLA_Q_RANK), lambda i: (i, 0)),
            pl.BlockSpec((tm, MLA_KV_RANK), lambda i: (i, 0)),
            pl.BlockSpec((tm, LANES), lambda i: (i, 0)),
        ],
        out_shape=[
            jax.ShapeDtypeStruct((n, MLA_Q_RANK), BF),
            jax.ShapeDtypeStruct((n, MLA_KV_RANK), BF),
            jax.ShapeDtypeStruct((n, LANES), BF),
        ],
        compiler_params=_params(("parallel",)),
        name="mla_latent",
    )(x, g.reshape(1, -1), wt, w_kr, gq.reshape(1, -1), gkv.reshape(1, -1), cos, sin)


def _rope_tables(seq, dim):
    half = dim // 2
    inv = 1.0 / (ROPE_THETA ** (np.arange(0, dim, 2, dtype=np.float32) / dim))
    ang = np.arange(seq, dtype=np.float32)[:, None] * inv.astype(np.float32)[None, :]
    cos = np.zeros((seq, LANES), np.float32)
    sin = np.zeros((seq, LANES), np.float32)
    cos[:, :half] = np.cos(ang)
    cos[:, LANES // 2:LANES // 2 + half] = np.cos(ang)
    sin[:, :half] = -np.sin(ang)
    sin[:, LANES // 2:LANES // 2 + half] = np.sin(ang)
    return jnp.asarray(cos), jnp.asarray(sin)


def _cmp_to_slc(seq):
    n_cmp = (seq - CMP_LEN) // CMP_STRIDE + 1
    n_slc = seq // SLC_LEN
    c0 = np.arange(n_cmp) * CMP_STRIDE
    c1 = c0 + CMP_LEN
    s0 = np.arange(n_slc) * SLC_LEN
    s1 = s0 + SLC_LEN
    ov = np.clip(np.minimum(c1[:, None], s1[None, :]) - np.maximum(c0[:, None], s0[None, :]), 0, None)
    m = np.zeros((n_slc, LANES), np.float32)
    m[:, :n_cmp] = (ov / CMP_LEN).T
    return jnp.asarray(m)


def _spread_rope_cols(w):
    half = MLA_ROPE // 2
    z = jnp.zeros(w.shape[:-1] + (LANES // 2 - half,), w.dtype)
    return jnp.concatenate([w[..., :half], z, w[..., half:], z], axis=-1)


def _hybrid_mixer(x, g, w_in, w_out, layer, cmp_pe, cmp_w1, cmp_w2, f_bias, *, batch, seq):
    n = batch * seq
    nq_w = NSA_HEADS * HEAD_DIM
    kv_w = NSA_GROUPS * HEAD_DIM
    fx_w = FOX_HEADS * HEAD_DIM
    c_cmp, c_slc, c_gate = nq_w, nq_w + 2 * kv_w, nq_w + 6 * kv_w
    c_fox = c_gate + 3 * NSA_HEADS
    c_f = c_fox + 3 * fx_w
    wt = jnp.swapaxes(w_in[layer], 0, 1)

    cos, sin = _rope_tables(seq, HEAD_DIM)
    tn = 2 * kv_w
    main_starts = tuple(range(0, c_cmp, tn)) + tuple(range(c_slc, c_gate, tn)) + tuple(range(c_fox, c_f, tn))
    f_lane0 = LANES - FOX_HEADS
    side_windows = ((c_cmp, c_slc - c_cmp), (c_gate, LANES), (c_f - f_lane0, LANES))
    gate_plane, f_plane = (c_slc - c_cmp) // LANES, (c_slc - c_cmp) // LANES + 1
    main, planes = _hybrid_proj(x, g, wt, main_starts, side_windows,
                                (cos, sin, (nq_w // tn, nq_w // tn + 2), (True, True, False, False)), seq=seq, tn=tn)

    kvc = _compress(planes, cmp_pe, cmp_w1, cmp_w2, batch=batch, seq=seq)

    main3 = main.reshape(batch, seq, -1)
    o_nsa = _nsa(main3, kvc, planes, gate_plane, cos, sin, _cmp_to_slc(seq), batch=batch, seq=seq)

    bias = jnp.zeros((1, LANES), F32).at[0, f_lane0:].set(f_bias)
    cum = _cumdecay(planes, f_plane, bias, batch=batch, seq=seq)[:, :, f_lane0:]
    cum_t = jnp.swapaxes(cum, 1, 2).reshape(batch * FOX_HEADS, seq)
    blk = lambda col: col // HEAD_DIM
    c_qf = nq_w + 4 * kv_w
    o_fox = _attention(main3, blk(c_qf), [main3], [blk(c_qf + fx_w)], [True], main3, blk(c_qf + 2 * fx_w),
                       batch=batch, seq=seq, heads=FOX_HEADS, scale=HEAD_DIM ** -0.5,
                       decay=cum_t[:, None, :], name="fox_attention")

    return _mm([o_nsa.reshape(n, nq_w), o_fox.reshape(n, fx_w)],
               [_W(w_out, layer=layer, rows=nq_w, row_blk=0), _W(w_out, layer=layer, rows=fx_w, row_blk=1)],
               D_MODEL, out_dtype=F32, tm=512, tn=D_MODEL, res=x, name="proj_out_hybrid")


def _mla_mixer(x, g, w_in, q_norm, kv_norm, w_uq, w_ukv, w_out, layer, *, batch, seq):
    n = batch * seq
    qk = MLA_NOPE + MLA_ROPE
    wt_lat = jnp.swapaxes(w_in, 0, 1)
    w_kr = jnp.swapaxes(_spread_rope_cols(w_in[:, MLA_Q_RANK + MLA_KV_RANK:]), 0, 1)
    wq = w_uq.reshape(MLA_Q_RANK, MLA_HEADS, qk)
    wq = jnp.concatenate([wq[..., :MLA_NOPE], _spread_rope_cols(wq[..., MLA_NOPE:])], axis=-1)
    wq = wq.reshape(MLA_Q_RANK, MLA_HEADS * 2 * HEAD_DIM)
    wkv = w_ukv.reshape(MLA_KV_RANK, MLA_HEADS, MLA_NOPE + MLA_V)
    wkv = jnp.concatenate([wkv[..., :MLA_NOPE].reshape(MLA_KV_RANK, -1),
                           wkv[..., MLA_NOPE:].reshape(MLA_KV_RANK, -1)], axis=1)

    cos, sin = _rope_tables(seq, MLA_ROPE)
    cq, ckv, kr = _mla_latent(x, g, wt_lat, w_kr, q_norm, kv_norm, cos, sin, seq=seq)
    tn = wq.shape[1]
    q = _mm([cq], [_W(wq)], wq.shape[1], out_dtype=BF, tm=512, tn=tn,
            rope=(cos, sin, (0, 1), (False, True) * (tn // (2 * LANES))), seq=seq, name="proj_mla_q")
    kv = _mm([ckv], [_W(wkv)], wkv.shape[1], out_dtype=BF, tm=512, tn=tn, name="proj_mla_kv")

    q3 = q.reshape(batch, seq, -1)
    kv3 = kv.reshape(batch, seq, -1)
    kr3 = kr.reshape(batch, seq, LANES)
    o = _attention(q3, 0, [kv3, kr3], [0, 0], [True, False], kv3, MLA_HEADS, batch=batch, seq=seq,
                   heads=MLA_HEADS, scale=qk ** -0.5, name="mla_attention")
    return _mm([o.reshape(n, MLA_HEADS * MLA_V)], [_W(w_out, layer=layer)], D_MODEL, out_dtype=F32, tm=512, tn=D_MODEL,
               res=x, name="proj_out_mla")


def _ffn(x, norms, w_in, w_out, layer):
    depth, d = norms.shape
    a, w_out_bf = _ffn_up(x, norms.reshape(depth, 1, d), w_in, w_out, layer)
    return _mm([a], [_W(w_out_bf)], d, out_dtype=F32, tm=1024, tn=512, res=x, name="ffn_down")


def kernel(x, ffn1_norm, ffn1_w_in, ffn1_w_out, mix_norm, ffn2_norm, ffn2_w_in, ffn2_w_out, hyb_w_in, hyb_w_out, nsa_cmp_pe, nsa_cmp_w1, nsa_cmp_w2, fox_f_bias, mla_w_in, mla_q_norm, mla_kv_norm, mla_w_uq, mla_w_ukv, mla_w_out, final_norm):
    batch, seq, d = x.shape
    depth = ffn1_norm.shape[0]
    x = x.reshape(batch * seq, d)
    for i in range(depth):
        x = _ffn(x, ffn1_norm, ffn1_w_in, ffn1_w_out, i)
        if i % 2 == 0:
            e = i // 2
            x = _hybrid_mixer(x, mix_norm[i], hyb_w_in, hyb_w_out, e, nsa_cmp_pe[e], nsa_cmp_w1[e], nsa_cmp_w2[e],
                              fox_f_bias[e], batch=batch, seq=seq)
        else:
            o = i // 2
            x = _mla_mixer(x, mix_norm[i], mla_w_in[o], mla_q_norm[o], mla_kv_norm[o], mla_w_uq[o], mla_w_ukv[o],
                           mla_w_out, o, batch=batch, seq=seq)
        x = _ffn(x, ffn2_norm, ffn2_w_in, ffn2_w_out, i)
    return _rmsnorm(x, final_norm).reshape(batch, seq, d)
```

```python
import functools

import numpy as np
import jax
import jax.numpy as jnp
from jax import lax
from jax.experimental import pallas as pl
from jax.experimental.pallas import tpu as pltpu

D_MODEL = 2048
HEAD_DIM = 128
ROPE_THETA = 10000.0
NORM_EPS = 1e-6
NEG = -1e30

NSA_HEADS = 8
NSA_GROUPS = 2
HEADS_PER_GROUP = NSA_HEADS // NSA_GROUPS
CMP_LEN = 32
CMP_STRIDE = 16
CMP_HIDDEN = 256
SLC_LEN = 64
SLC_TOP = 8
WIN = 512
FOX_HEADS = 8
MLA_HEADS = 16
MLA_Q_RANK = 512
MLA_KV_RANK = 512
MLA_NOPE = 128
MLA_ROPE = 64
MLA_V = 128
D_FF = 5632

LANES = 128
VMEM_LIMIT = 56 * 1024 * 1024

BF = jnp.bfloat16
F32 = jnp.float32


def _params(sem, vmem=VMEM_LIMIT):
    return pltpu.CompilerParams(dimension_semantics=sem, vmem_limit_bytes=vmem)


def _rope_lanes(x, cos, sin):
    return x * cos + pltpu.roll(x, LANES // 2, axis=1) * sin


def _dot_t(a, b):
    return lax.dot_general(a, b, (((1,), (1,)), ((), ())), preferred_element_type=F32)


def _softmax_rows(s):
    m = jnp.max(s, axis=-1, keepdims=True)
    e = jnp.exp(s - m)
    return e, jnp.sum(e, axis=-1, keepdims=True)


def _rms_normed(x, g):
    ms = jnp.mean(x * x, axis=-1, keepdims=True)
    return x * lax.rsqrt(ms + NORM_EPS) * g


def _ffn_up_kernel(x_ref, g_ref, wa_ref, wb_ref, w2_ref, o_ref, w2b_ref, *scratch, n_cast):
    i, j = pl.program_id(0), pl.program_id(1)

    if scratch:
        xn_sc, = scratch

        @pl.when(j == 0)
        def _():
            xn_sc[...] = _rms_normed(x_ref[...], g_ref[...]).astype(BF)

        xn = xn_sc[...]
    else:
        xn = x_ref[...]

    @pl.when(i * pl.num_programs(1) + j < n_cast)
    def _():
        w2b_ref[...] = w2_ref[...].astype(BF)

    h1 = jnp.dot(xn, wa_ref[...].astype(BF), preferred_element_type=F32)
    h2 = jnp.dot(xn, wb_ref[...].astype(BF), preferred_element_type=F32)
    o_ref[...] = (h1 * jax.nn.sigmoid(h1) * h2 * 0.5).astype(BF)


def _ffn_up(x, g, w_in, w_out, layer, *, tm=1024, tf=512, cast_rows=256):
    n, d = x.shape
    n_f = D_FF // tf
    n_cast = D_FF // cast_rows
    assert n_cast <= (n // tm) * n_f
    chunk = lambda i, j: jnp.minimum(i * n_f + j, n_cast - 1)
    return pl.pallas_call(
        functools.partial(_ffn_up_kernel, n_cast=n_cast),
        grid=(n // tm, n_f),
        in_specs=[
            pl.BlockSpec((tm, d), lambda i, j: (i, 0)),
            pl.BlockSpec((None, 1, d), lambda i, j: (layer, 0, 0)),
            pl.BlockSpec((None, d, tf), lambda i, j: (layer, 0, j)),
            pl.BlockSpec((None, d, tf), lambda i, j: (layer, 0, n_f + j)),
            pl.BlockSpec((None, cast_rows, d), lambda i, j: (layer, chunk(i, j), 0)),
        ],
        out_specs=[pl.BlockSpec((tm, tf), lambda i, j: (i, j)),
                   pl.BlockSpec((cast_rows, d), lambda i, j: (chunk(i, j), 0))],
        out_shape=[jax.ShapeDtypeStruct((n, D_FF), BF), jax.ShapeDtypeStruct((D_FF, d), BF)],
        scratch_shapes=[pltpu.VMEM((tm, d), BF)] if x.dtype == F32 else [],
        compiler_params=_params(("arbitrary", "arbitrary")),
        name="ffn_up",
    )(x, g, w_in, w_in, w_out)


def _norm_kernel(x_ref, g_ref, o_ref):
    o_ref[...] = _rms_normed(x_ref[...], g_ref[...]).astype(o_ref.dtype)


def _rmsnorm(x, g, *, tm=512):
    n, d = x.shape
    return pl.pallas_call(
        _norm_kernel,
        grid=(n // tm,),
        in_specs=[pl.BlockSpec((tm, d), lambda i: (i, 0)), pl.BlockSpec((1, d), lambda i: (0, 0))],
        out_specs=pl.BlockSpec((tm, d), lambda i: (i, 0)),
        out_shape=jax.ShapeDtypeStruct((n, d), x.dtype),
        compiler_params=_params(("parallel",)),
        name="rmsnorm",
    )(x, g.reshape(1, d))


class _W:
    def __init__(self, arr, *, layer=None, rows=None, row_blk=0, col_blk0=0, row_starts=None):
        self.arr, self.layer, self.row_blk, self.col_blk0 = arr, layer, row_blk, col_blk0
        self.transposed = row_starts is not None
        self.row_starts = row_starts
        assert row_starts is None or all(r % 8 == 0 for r in row_starts)
        self.rows = rows if rows is not None else arr.shape[-2]

    def spec(self, tn, single_buffer=False):
        layer, row_blk, col_blk0 = self.layer, self.row_blk, self.col_blk0
        if self.transposed:
            starts = self.row_starts

            def start(j):
                off = starts[-1]
                for k in range(len(starts) - 2, -1, -1):
                    off = jnp.where(j == k, starts[k], off)
                return pl.multiple_of(off, 8)

            return pl.BlockSpec((pl.Element(tn), pl.Element(self.arr.shape[1])), lambda i, j: (start(j), 0))
        kw = dict(pipeline_mode=pl.Buffered(1)) if single_buffer else {}
        if self.arr.ndim == 3:
            return pl.BlockSpec((None, self.rows, tn), lambda i, j: (layer, row_blk, col_blk0 + j), **kw)
        return pl.BlockSpec((self.rows, tn), lambda i, j: (row_blk, col_blk0 + j), **kw)


def _mm_kernel(*refs, n_a, has_res, has_norm_out, rope_j, rope_chunks):
    a_refs = refs[:n_a]
    w_refs = refs[n_a:2 * n_a]
    k = 2 * n_a
    if has_res:
        res_ref = refs[k]
        k += 1
    if has_norm_out:
        g_ref = refs[k]
        k += 1
    if rope_chunks is not None:
        cos_ref, sin_ref = refs[k], refs[k + 1]
        k += 2
    o_ref = refs[k]

    acc = jnp.dot(a_refs[0][...], w_refs[0][...].astype(BF), preferred_element_type=F32)
    for a_ref, w_ref in zip(a_refs[1:], w_refs[1:]):
        acc += jnp.dot(a_ref[...], w_ref[...].astype(BF), preferred_element_type=F32)
    if has_res:
        acc += res_ref[...]

    if has_norm_out:
        refs[k + 1][...] = _rms_normed(acc, g_ref[...]).astype(BF)
    if rope_chunks is None:
        o_ref[...] = acc.astype(o_ref.dtype)
    else:
        j = pl.program_id(1)
        is_rope = (j >= rope_j[0]) & (j < rope_j[1])
        cos, sin = cos_ref[...], sin_ref[...]
        for c, on in enumerate(rope_chunks):
            blk = acc[:, c * LANES:(c + 1) * LANES]
            if on:
                blk = jnp.where(is_rope, _rope_lanes(blk, cos, sin), blk)
            o_ref[:, c * LANES:(c + 1) * LANES] = blk.astype(o_ref.dtype)


def _mm(a_list, w_list, n_out, *, out_dtype, tm, tn, res=None, norm_out=None, rope=None, seq=None, name="mm"):
    n = a_list[0].shape[0]
    n_a = len(a_list)
    in_specs, args = [], []
    for a in a_list:
        in_specs.append(pl.BlockSpec((tm, a.shape[1]), lambda i, j: (i, 0)))
        args.append(a)
    for w in w_list:
        assert not w.transposed
        in_specs.append(w.spec(tn, single_buffer=tn == n_out))
        args.append(w.arr)
    if res is not None:
        in_specs.append(pl.BlockSpec((tm, tn), lambda i, j: (i, j)))
        args.append(res)
    out_specs = [pl.BlockSpec((tm, tn), lambda i, j: (i, j))]
    out_shape = [jax.ShapeDtypeStruct((n, n_out), out_dtype)]
    if norm_out is not None:
        assert tn == n_out
        in_specs.append(pl.BlockSpec((1, n_out), lambda i, j: (0, 0)))
        args.append(norm_out.reshape(1, n_out))
        out_specs.append(pl.BlockSpec((tm, tn), lambda i, j: (i, j)))
        out_shape.append(jax.ShapeDtypeStruct((n, n_out), BF))
    rope_j = rope_chunks = None
    if rope is not None:
        cos, sin, rope_j, rope_chunks = rope
        n_seq = seq // tm
        for t in (cos, sin):
            in_specs.append(pl.BlockSpec((tm, LANES), lambda i, j: (i % n_seq, 0)))
            args.append(t)
    outs = pl.pallas_call(
        functools.partial(_mm_kernel, n_a=n_a, has_res=res is not None, has_norm_out=norm_out is not None,
                          rope_j=rope_j, rope_chunks=rope_chunks),
        grid=(n // tm, n_out // tn),
        in_specs=in_specs,
        out_specs=out_specs,
        out_shape=out_shape,
        compiler_params=_params(("parallel", "arbitrary")),
        name=name,
    )(*args)
    return outs if norm_out is not None else outs[0]


def _hybrid_proj_kernel(x_ref, g_ref, w_ref, *rest, n_main, n_side, rope_j, rope_chunks):
    side_refs = rest[:n_side]
    cos_ref, sin_ref, o_ref, p_ref, xn_sc = rest[n_side:]
    j = pl.program_id(1)

    @pl.when(j == 0)
    def _():
        xn_sc[...] = _rms_normed(x_ref[...], g_ref[...]).astype(BF)

    @pl.when(j < n_main)
    def _():
        acc = _dot_t(xn_sc[...], w_ref[...].astype(BF))
        is_rope = (j >= rope_j[0]) & (j < rope_j[1])
        cos, sin = cos_ref[...], sin_ref[...]
        for c, on in enumerate(rope_chunks):
            blk = acc[:, c * LANES:(c + 1) * LANES]
            if on:
                blk = jnp.where(is_rope, _rope_lanes(blk, cos, sin), blk)
            o_ref[:, c * LANES:(c + 1) * LANES] = blk.astype(o_ref.dtype)

    @pl.when(j == n_main)
    def _():
        w_side = jnp.concatenate([w[...] for w in side_refs], axis=0).astype(BF)
        s = _dot_t(xn_sc[...], w_side)
        for p in range(s.shape[1] // LANES):
            p_ref[p] = s[:, p * LANES:(p + 1) * LANES]


def _hybrid_proj(x, g, wt, main_starts, side_windows, rope, *, seq, tm=1024, tn=512):
    n, d = x.shape
    cos, sin, rope_j, rope_chunks = rope
    n_main = len(main_starts)
    n_planes = sum(rows for _, rows in side_windows) // LANES
    n_seq = seq // tm
    last = n_main - 1
    side_specs = [pl.BlockSpec((pl.Element(rows), pl.Element(d)), lambda i, j, start=start: (start, 0),
                               pipeline_mode=pl.Buffered(1)) for start, rows in side_windows]
    return pl.pallas_call(
        functools.partial(_hybrid_proj_kernel, n_main=n_main, n_side=len(side_windows), rope_j=rope_j,
                          rope_chunks=rope_chunks),
        grid=(n // tm, n_main + 1),
        in_specs=[
            pl.BlockSpec((tm, d), lambda i, j: (i, 0)),
            pl.BlockSpec((1, d), lambda i, j: (0, 0)),
            _W(wt, row_starts=main_starts).spec(tn),
            *side_specs,
            pl.BlockSpec((tm, LANES), lambda i, j: (i % n_seq, 0)),
            pl.BlockSpec((tm, LANES), lambda i, j: (i % n_seq, 0)),
        ],
        out_specs=[pl.BlockSpec((tm, tn), lambda i, j: (i, jnp.minimum(j, last))),
                   pl.BlockSpec((n_planes, tm, LANES), lambda i, j: (0, i, 0))],
        out_shape=[jax.ShapeDtypeStruct((n, n_main * tn), BF), jax.ShapeDtypeStruct((n_planes, n, LANES), F32)],
        scratch_shapes=[pltpu.VMEM((tm, d), BF)],
        compiler_params=_params(("parallel", "arbitrary")),
        name="proj_hybrid",
    )(x, g.reshape(1, d), wt, *([wt] * len(side_windows)), cos, sin)


def _compress_kernel(x_ref, pe_ref, w1_ref, w2_ref, o_ref):
    n_blk = o_ref.shape[2]
    ha = jnp.zeros((n_blk, CMP_HIDDEN), F32)
    hb = jnp.zeros((n_blk, CMP_HIDDEN), F32)
    for l in range(CMP_STRIDE):
        xl = x_ref[0, pl.ds(l, n_blk, stride=CMP_STRIDE), :]
        wa = w1_ref[0, l * HEAD_DIM:(l + 1) * HEAD_DIM, :].astype(BF)
        wb = w1_ref[0, (CMP_STRIDE + l) * HEAD_DIM:(CMP_STRIDE + l + 1) * HEAD_DIM, :].astype(BF)
        ha += jnp.dot((xl + pe_ref[0, l:l + 1, :]).astype(BF), wa, preferred_element_type=F32)
        hb += jnp.dot((xl + pe_ref[0, CMP_STRIDE + l:CMP_STRIDE + l + 1, :]).astype(BF), wb,
                      preferred_element_type=F32)
    hid = ha + pltpu.roll(hb, n_blk - 1, axis=0)
    act = jax.nn.gelu(hid, approximate=True).astype(BF)
    o_ref[0, 0] = jnp.dot(act, w2_ref[0].astype(BF), preferred_element_type=F32)


def _compress(planes, pe, w1, w2, *, batch, seq):
    rows = seq // CMP_STRIDE
    return pl.pallas_call(
        _compress_kernel,
        grid=(2 * NSA_GROUPS, batch),
        in_specs=[
            pl.BlockSpec((1, seq, HEAD_DIM), lambda c, b: (c, b, 0)),
            pl.BlockSpec((1, CMP_LEN, HEAD_DIM), lambda c, b: (c // NSA_GROUPS, 0, 0)),
            pl.BlockSpec((1, CMP_LEN * HEAD_DIM, CMP_HIDDEN), lambda c, b: (c // NSA_GROUPS, 0, 0)),
            pl.BlockSpec((1, CMP_HIDDEN, HEAD_DIM), lambda c, b: (c // NSA_GROUPS, 0, 0)),
        ],
        out_specs=pl.BlockSpec((1, 1, rows, HEAD_DIM), lambda c, b: (c, b, 0, 0)),
        out_shape=jax.ShapeDtypeStruct((2 * NSA_GROUPS, batch, rows, HEAD_DIM), F32),
        compiler_params=_params(("arbitrary", "arbitrary")),
        name="nsa_compress",
    )(planes, pe, w1, w2)


def _stacked_softmax_pv(s_raw, bias, v, scale, n_stack, tq):
    exp2_scale = scale * float(np.log2(np.e))
    es, ls = [], []
    for h in range(n_stack):
        u = s_raw[h * tq:(h + 1) * tq] + bias
        e = jnp.exp2((u - jnp.max(u, axis=-1, keepdims=True)) * exp2_scale)
        ls.append(jnp.sum(e, axis=-1, keepdims=True))
        es.append(e.astype(BF))
    o = jnp.dot(jnp.concatenate(es, axis=0), v, preferred_element_type=F32)
    return o / jnp.concatenate(ls, axis=0)


def _nsa_kernel(q_ref, ks_ref, kw_ref, vs_ref, vw_ref, kvc_ref, gate_ref, cos_ref, sin_ref, m_ref, o_ref,
                *, tq, seq, key_step):
    qi = pl.program_id(1)
    hg = HEADS_PER_GROUP
    scale = HEAD_DIM ** -0.5
    n_cmp = (seq - CMP_LEN) // CMP_STRIDE + 1
    n_slc = seq // SLC_LEN
    band = WIN + tq

    t_row = qi * tq + lax.broadcasted_iota(jnp.int32, (tq, 1), 0)
    cos = jnp.concatenate([cos_ref[...]] * hg, axis=0)
    sin = jnp.concatenate([sin_ref[...]] * hg, axis=0)
    gates = jax.nn.sigmoid(gate_ref[0])

    n_lane = lax.broadcasted_iota(jnp.int32, (1, LANES), 1)
    cmask = (n_lane * CMP_STRIDE + (CMP_LEN - 1) <= t_row) & (n_lane < n_cmp)
    cbias = jnp.where(cmask, 0.0, NEG)
    j_blk = lax.broadcasted_iota(jnp.int32, (n_slc, 1), 0)
    slc_shift = SLC_LEN.bit_length() - 1
    cur = jnp.right_shift(qi * tq + lax.broadcasted_iota(jnp.int32, (1, tq), 1), slc_shift)
    forced = (j_blk == 0) | (j_blk == cur) | (j_blk == cur - 1)
    w_start = pl.multiple_of(jnp.maximum(qi * tq - WIN, 0), tq)
    w_pos = w_start + lax.broadcasted_iota(jnp.int32, (1, band), 1)
    wbias = jnp.where((w_pos <= t_row) & (w_pos > t_row - WIN), 0.0, NEG)

    def group_queries(g):
        qg = q_ref[0, :, g * hg * HEAD_DIM:(g + 1) * hg * HEAD_DIM].astype(F32)
        qp = jnp.concatenate([qg[:, h * HEAD_DIM:(h + 1) * HEAD_DIM] for h in range(hg)], axis=0)
        return qp.astype(BF), _rope_lanes(qp, cos, sin).astype(BF)

    def compressed_branch(g, qp):
        kc = kvc_ref[g, 0].astype(BF)
        vc = kvc_ref[NSA_GROUPS + g, 0].astype(BF)
        s = _dot_t(qp, kc)
        ps = []
        for h in range(hg):
            u = s[h * tq:(h + 1) * tq] * scale + cbias
            e, l = _softmax_rows(u)
            ps.append(jnp.where(cmask, e / l, 0.0))
        o_cmp = jnp.dot(jnp.concatenate(ps, axis=0).astype(BF), vc, preferred_element_type=F32)

        p_sum = ps[0]
        for h in range(1, hg):
            p_sum = p_sum + ps[h]
        imp = lax.dot_general(m_ref[...], p_sum, (((1,), (1,)), ((), ())), preferred_element_type=F32,
                              precision=lax.Precision.HIGHEST)
        imp = jnp.where(j_blk > cur, -jnp.inf, jnp.where(forced, jnp.inf, imp))
        rank = jnp.zeros((n_slc, tq), jnp.int32)
        for i in range(n_slc):
            ci = imp[i:i + 1, :]
            beats = (ci > imp) | ((ci == imp) & (j_blk > i))
            rank = rank + beats.astype(jnp.int32)
        return o_cmp, jnp.where(rank < SLC_TOP, 0.0, NEG).astype(BF)

    qs = [group_queries(g) for g in range(NSA_GROUPS)]
    cmps = [compressed_branch(g, qs[g][0]) for g in range(NSA_GROUPS)]

    def attend(width):
        blocks = width // SLC_LEN
        expand = jnp.where(jnp.right_shift(lax.broadcasted_iota(jnp.int32, (blocks, width), 1), slc_shift)
                           == lax.broadcasted_iota(jnp.int32, (blocks, width), 0), 1.0, 0.0).astype(BF)
        s_pos = lax.broadcasted_iota(jnp.int32, (1, width), 1)
        lanes = [slice(g * HEAD_DIM, (g + 1) * HEAD_DIM) for g in range(NSA_GROUPS)]
        s_win = [_dot_t(qs[g][1], kw_ref[0, pl.ds(w_start, band), lanes[g]]) for g in range(NSA_GROUPS)]
        s_slc = [_dot_t(qs[g][1], ks_ref[0, :width, lanes[g]]) for g in range(NSA_GROUPS)]
        for g in range(NSA_GROUPS):
            o_win = _stacked_softmax_pv(s_win[g], wbias, vw_ref[0, pl.ds(w_start, band), lanes[g]], scale, hg, tq)
            o_cmp, sel_neg = cmps[g]
            bias = lax.dot_general(sel_neg[:blocks], expand, (((0,), (0,)), ((), ())),
                                   preferred_element_type=F32)
            bias = jnp.where(s_pos <= t_row, bias, NEG)
            o_slc = _stacked_softmax_pv(s_slc[g], bias, vs_ref[0, :width, lanes[g]], scale, hg, tq)
            for hh in range(hg):
                h = g * hg + hh
                rows = slice(hh * tq, (hh + 1) * tq)
                o = (o_cmp[rows] * gates[:, 3 * h:3 * h + 1]
                     + o_slc[rows] * gates[:, 3 * h + 1:3 * h + 2]
                     + o_win[rows] * gates[:, 3 * h + 2:3 * h + 3])
                o_ref[0, :, h * HEAD_DIM:(h + 1) * HEAD_DIM] = o.astype(o_ref.dtype)

    n_widths = seq // key_step
    super_row = qi // (key_step // tq)
    for k in range(n_widths):
        pl.when(super_row == k)(functools.partial(attend, (k + 1) * key_step))


def _nsa(main, kvc, planes, gate_plane, cos, sin, m_cs, *, batch, seq, tq=128, key_step=512):
    nq = seq // tq
    kv_w = NSA_GROUPS * HEAD_DIM
    kv0 = NSA_HEADS * HEAD_DIM // kv_w
    return pl.pallas_call(
        functools.partial(_nsa_kernel, tq=tq, seq=seq, key_step=key_step),
        grid=(batch, nq),
        in_specs=[
            pl.BlockSpec((1, tq, NSA_HEADS * HEAD_DIM), lambda b, i: (b, i, 0)),
            pl.BlockSpec((1, seq, kv_w), lambda b, i: (b, 0, kv0)),
            pl.BlockSpec((1, seq, kv_w), lambda b, i: (b, 0, kv0 + 2)),
            pl.BlockSpec((1, seq, kv_w), lambda b, i: (b, 0, kv0 + 1)),
            pl.BlockSpec((1, seq, kv_w), lambda b, i: (b, 0, kv0 + 3)),
            pl.BlockSpec((2 * NSA_GROUPS, 1, LANES, HEAD_DIM), lambda b, i: (0, b, 0, 0)),
            pl.BlockSpec((1, tq, LANES), lambda b, i: (gate_plane, b * nq + i, 0)),
            pl.BlockSpec((tq, LANES), lambda b, i: (i, 0)),
            pl.BlockSpec((tq, LANES), lambda b, i: (i, 0)),
            pl.BlockSpec(m_cs.shape, lambda b, i: (0, 0)),
        ],
        out_specs=pl.BlockSpec((1, tq, NSA_HEADS * HEAD_DIM), lambda b, i: (b, i, 0)),
        out_shape=jax.ShapeDtypeStruct((batch, seq, NSA_HEADS * HEAD_DIM), BF),
        compiler_params=_params(("parallel", "arbitrary")),
        name="nsa_attention",
    )(main, main, main, main, main, kvc, planes, cos, sin, m_cs)


def _cumdecay_kernel(x_ref, b_ref, o_ref):
    z = x_ref[0] + b_ref[...]
    c = jnp.minimum(z, 0.0) - jnp.log1p(jnp.exp(-jnp.abs(z)))
    rows = c.shape[0]
    r = lax.broadcasted_iota(jnp.int32, (rows, 1), 0)
    k = 1
    while k < rows:
        c = c + jnp.where(r >= k, pltpu.roll(c, k, axis=0), 0.0)
        k *= 2
    o_ref[0] = c


def _cumdecay(planes, plane, bias, *, batch, seq):
    return pl.pallas_call(
        _cumdecay_kernel,
        grid=(batch,),
        in_specs=[pl.BlockSpec((1, seq, LANES), lambda b: (plane, b, 0)),
                  pl.BlockSpec((1, LANES), lambda b: (0, 0))],
        out_specs=pl.BlockSpec((1, seq, LANES), lambda b: (b, 0, 0)),
        out_shape=jax.ShapeDtypeStruct((batch, seq, LANES), F32),
        compiler_params=_params(("arbitrary",)),
        name="fox_cumdecay",
    )(planes, bias)


def _attn_kernel(*refs, tq, seq, scale, k_per_head, hps, decay):
    n_qk = len(k_per_head)
    q_ref = refs[0]
    k_refs = refs[1:1 + n_qk]
    v_ref = refs[1 + n_qk]
    k = 2 + n_qk
    if decay:
        ck_ref = refs[k]
        k += 1
    o_ref = refs[k]

    row = lax.broadcasted_iota(jnp.int32, (tq, 1), 0)
    col = lax.broadcasted_iota(jnp.int32, (1, tq), 1)
    tri = col <= row
    exp2_scale = scale * float(np.log2(np.e))
    head_lanes = lambda hh, width=HEAD_DIM: slice(hh * width, (hh + 1) * width)

    def head_keys(hh):
        parts = [k_ref[0, :, head_lanes(hh)] if per_head else k_ref[0] for k_ref, per_head in zip(k_refs, k_per_head)]
        return parts[0] if n_qk == 1 else jnp.concatenate(parts, axis=1)

    k_all = [head_keys(hh) for hh in range(hps)]

    n_tiles = seq // tq

    def scores(i):
        return [_dot_t(q_ref[0, i * tq:(i + 1) * tq, head_lanes(hh, n_qk * HEAD_DIM)], k_all[hh][:(i + 1) * tq])
                for hh in range(hps)]

    s_next = scores(n_tiles - 1)
    for i in reversed(range(n_tiles)):
        lo, hi = i * tq, (i + 1) * tq
        s_cur = s_next
        if i > 0:
            s_next = scores(i - 1)
        for hh in range(hps):
            s = s_cur[hh]
            if decay:
                ck = ck_ref[hh, :, :hi]
                cq = jnp.sum(jnp.where(col == row, ck[:, lo:hi], 0.0), axis=1, keepdims=True)
                s = s * scale + (cq - ck)
                prob = lambda u, m: jnp.exp(u - m)
            else:
                prob = lambda u, m: jnp.exp2((u - m) * exp2_scale)
            s_diag = jnp.where(tri, s[:, lo:hi], NEG)
            m = jnp.max(s_diag, axis=-1, keepdims=True)
            if i > 0:
                m = jnp.maximum(m, jnp.max(s[:, :lo], axis=-1, keepdims=True))
            e_diag = prob(s_diag, m)
            l = jnp.sum(e_diag, axis=-1, keepdims=True)
            e = e_diag.astype(BF)
            if i > 0:
                e_off = prob(s[:, :lo], m)
                l = l + jnp.sum(e_off, axis=-1, keepdims=True)
                e = jnp.concatenate([e_off.astype(BF), e], axis=1)
            o = jnp.dot(e, v_ref[0, :hi, head_lanes(hh)], preferred_element_type=F32) / l
            o_ref[0, lo:hi, head_lanes(hh)] = o.astype(o_ref.dtype)


def _attention(q_arr, q_blk0, k_arrs, k_blk0s, k_per_head, v_arr, v_blk0, *, batch, seq, heads, scale,
               decay=None, tq=256, hps=2, name="attention"):
    n_qk = len(k_arrs)
    assert heads % hps == 0 and q_blk0 % hps == 0 and v_blk0 % hps == 0
    in_specs = [pl.BlockSpec((1, seq, hps * n_qk * HEAD_DIM), lambda b, h: (b, 0, q_blk0 // hps + h))]
    args = [q_arr]
    for arr, blk0, per_head in zip(k_arrs, k_blk0s, k_per_head):
        if per_head:
            assert blk0 % hps == 0
            in_specs.append(pl.BlockSpec((1, seq, hps * HEAD_DIM), lambda b, h, blk0=blk0: (b, 0, blk0 // hps + h)))
        else:
            in_specs.append(pl.BlockSpec((1, seq, HEAD_DIM), lambda b, h, blk0=blk0: (b, 0, blk0)))
        args.append(arr)
    in_specs.append(pl.BlockSpec((1, seq, hps * HEAD_DIM), lambda b, h: (b, 0, v_blk0 // hps + h)))
    args.append(v_arr)
    if decay is not None:
        in_specs.append(pl.BlockSpec((hps, 1, seq), lambda b, h: (b * (heads // hps) + h, 0, 0)))
        args.append(decay)
    return pl.pallas_call(
        functools.partial(_attn_kernel, tq=tq, seq=seq, scale=scale, k_per_head=tuple(k_per_head), hps=hps,
                          decay=decay is not None),
        grid=(batch, heads // hps),
        in_specs=in_specs,
        out_specs=pl.BlockSpec((1, seq, hps * HEAD_DIM), lambda b, h: (b, 0, h)),
        out_shape=jax.ShapeDtypeStruct((batch, seq, heads * HEAD_DIM), BF),
        compiler_params=_params(("parallel", "arbitrary")),
        name=name,
    )(*args)


def _mla_latent_kernel(x_ref, g_ref, wt_ref, wkr_ref, gq_ref, gkv_ref, cos_ref, sin_ref, cq_ref, ckv_ref, kr_ref):
    h = _rms_normed(x_ref[...], g_ref[...]).astype(BF)
    c = _dot_t(h, wt_ref[...].astype(BF))
    cq_ref[...] = _rms_normed(c[:, :MLA_Q_RANK], gq_ref[...]).astype(BF)
    ckv_ref[...] = _rms_normed(c[:, MLA_Q_RANK:], gkv_ref[...]).astype(BF)
    kr_ref[...] = _rope_lanes(_dot_t(h, wkr_ref[...].astype(BF)), cos_ref[...], sin_ref[...]).astype(BF)


def _mla_latent(x, g, wt, w_kr, gq, gkv, cos, sin, *, seq, tm=512):
    n = x.shape[0]
    n_seq = seq // tm
    return pl.pallas_call(
        _mla_latent_kernel,
        grid=(n // tm,),
        in_specs=[
            pl.BlockSpec((tm, x.shape[1]), lambda i: (i, 0)),
            pl.BlockSpec((1, x.shape[1]), lambda i: (0, 0)),
            pl.BlockSpec((MLA_Q_RANK + MLA_KV_RANK, wt.shape[1]), lambda i: (0, 0)),
            pl.BlockSpec(w_kr.shape, lambda i: (0, 0)),
            pl.BlockSpec((1, MLA_Q_RANK), lambda i: (0, 0)),
            pl.BlockSpec((1, MLA_KV_RANK), lambda i: (0, 0)),
            pl.BlockSpec((tm, LANES), lambda i: (i % n_seq, 0)),
            pl.BlockSpec((tm, LANES), lambda i: (i % n_seq, 0)),
        ],
        out_specs=[
            pl.BlockSpec((tm, MLA_Q_RANK), lambda i: (i, 0)),
            pl.BlockSpec((tm, MLA_KV_RANK), lambda i: (i, 0)),
            pl.BlockSpec((tm, LANES), lambda i: (i, 0)),
        ],
        out_shape=[
            jax.ShapeDtypeStruct((n, MLA_Q_RANK), BF),
            jax.ShapeDtypeStruct((n, MLA_KV_RANK), BF),
            jax.ShapeDtypeStruct((n, LANES), BF),
        ],
        compiler_params=_params(("parallel",)),
        name="mla_latent",
    )(x, g.reshape(1, -1), wt, w_kr, gq.reshape(1, -1), gkv.reshape(1, -1), cos, sin)


def _rope_tables(seq, dim):
    half = dim // 2
    inv = 1.0 / (ROPE_THETA ** (np.arange(0, dim, 2, dtype=np.float32) / dim))
    ang = np.arange(seq, dtype=np.float32)[:, None] * inv.astype(np.float32)[None, :]
    cos = np.zeros((seq, LANES), np.float32)
    sin = np.zeros((seq, LANES), np.float32)
    cos[:, :half] = np.cos(ang)
    cos[:, LANES // 2:LANES // 2 + half] = np.cos(ang)
    sin[:, :half] = -np.sin(ang)
    sin[:, LANES // 2:LANES // 2 + half] = np.sin(ang)
    return jnp.asarray(cos), jnp.asarray(sin)


def _cmp_to_slc(seq):
    n_cmp = (seq - CMP_LEN) // CMP_STRIDE + 1
    n_slc = seq // SLC_LEN
    c0 = np.arange(n_cmp) * CMP_STRIDE
    c1 = c0 + CMP_LEN
    s0 = np.arange(n_slc) * SLC_LEN
    s1 = s0 + SLC_LEN
    ov = np.clip(np.minimum(c1[:, None], s1[None, :]) - np.maximum(c0[:, None], s0[None, :]), 0, None)
    m = np.zeros((n_slc, LANES), np.float32)
    m[:, :n_cmp] = (ov / CMP_LEN).T
    return jnp.asarray(m)


def _spread_rope_cols(w):
    half = MLA_ROPE // 2
    z = jnp.zeros(w.shape[:-1] + (LANES // 2 - half,), w.dtype)
    return jnp.concatenate([w[..., :half], z, w[..., half:], z], axis=-1)


def _hybrid_mixer(x, g, g_next, w_in, w_out, layer, cmp_pe, cmp_w1, cmp_w2, f_bias, *, batch, seq):
    n = batch * seq
    nq_w = NSA_HEADS * HEAD_DIM
    kv_w = NSA_GROUPS * HEAD_DIM
    fx_w = FOX_HEADS * HEAD_DIM
    c_cmp, c_slc, c_gate = nq_w, nq_w + 2 * kv_w, nq_w + 6 * kv_w
    c_fox = c_gate + 3 * NSA_HEADS
    c_f = c_fox + 3 * fx_w
    wt = jnp.swapaxes(w_in[layer], 0, 1)

    cos, sin = _rope_tables(seq, HEAD_DIM)
    tn = 2 * kv_w
    main_starts = tuple(range(0, c_cmp, tn)) + tuple(range(c_slc, c_gate, tn)) + tuple(range(c_fox, c_f, tn))
    f_lane0 = LANES - FOX_HEADS
    side_windows = ((c_cmp, c_slc - c_cmp), (c_gate, LANES), (c_f - f_lane0, LANES))
    gate_plane, f_plane = (c_slc - c_cmp) // LANES, (c_slc - c_cmp) // LANES + 1
    main, planes = _hybrid_proj(x, g, wt, main_starts, side_windows,
                                (cos, sin, (nq_w // tn, nq_w // tn + 2), (True, True, False, False)), seq=seq, tn=tn)

    kvc = _compress(planes, cmp_pe, cmp_w1, cmp_w2, batch=batch, seq=seq)

    main3 = main.reshape(batch, seq, -1)
    o_nsa = _nsa(main3, kvc, planes, gate_plane, cos, sin, _cmp_to_slc(seq), batch=batch, seq=seq)

    bias = jnp.zeros((1, LANES), F32).at[0, f_lane0:].set(f_bias)
    cum = _cumdecay(planes, f_plane, bias, batch=batch, seq=seq)[:, :, f_lane0:]
    cum_t = jnp.swapaxes(cum, 1, 2).reshape(batch * FOX_HEADS, seq)
    blk = lambda col: col // HEAD_DIM
    c_qf = nq_w + 4 * kv_w
    o_fox = _attention(main3, blk(c_qf), [main3], [blk(c_qf + fx_w)], [True], main3, blk(c_qf + 2 * fx_w),
                       batch=batch, seq=seq, heads=FOX_HEADS, scale=HEAD_DIM ** -0.5,
                       decay=cum_t[:, None, :], name="fox_attention")

    return _mm([o_nsa.reshape(n, nq_w), o_fox.reshape(n, fx_w)],
               [_W(w_out, layer=layer, rows=nq_w, row_blk=0), _W(w_out, layer=layer, rows=fx_w, row_blk=1)],
               D_MODEL, out_dtype=F32, tm=512, tn=D_MODEL, res=x, norm_out=g_next, name="proj_out_hybrid")


def _mla_mixer(x, g, g_next, w_in, q_norm, kv_norm, w_uq, w_ukv, w_out, layer, *, batch, seq):
    n = batch * seq
    qk = MLA_NOPE + MLA_ROPE
    wt_lat = jnp.swapaxes(w_in, 0, 1)
    w_kr = jnp.swapaxes(_spread_rope_cols(w_in[:, MLA_Q_RANK + MLA_KV_RANK:]), 0, 1)
    wq = w_uq.reshape(MLA_Q_RANK, MLA_HEADS, qk)
    wq = jnp.concatenate([wq[..., :MLA_NOPE], _spread_rope_cols(wq[..., MLA_NOPE:])], axis=-1)
    wq = wq.reshape(MLA_Q_RANK, MLA_HEADS * 2 * HEAD_DIM)
    wkv = w_ukv.reshape(MLA_KV_RANK, MLA_HEADS, MLA_NOPE + MLA_V)
    wkv = jnp.concatenate([wkv[..., :MLA_NOPE].reshape(MLA_KV_RANK, -1),
                           wkv[..., MLA_NOPE:].reshape(MLA_KV_RANK, -1)], axis=1)

    cos, sin = _rope_tables(seq, MLA_ROPE)
    cq, ckv, kr = _mla_latent(x, g, wt_lat, w_kr, q_norm, kv_norm, cos, sin, seq=seq)
    tn = wq.shape[1]
    q = _mm([cq], [_W(wq)], wq.shape[1], out_dtype=BF, tm=512, tn=tn,
            rope=(cos, sin, (0, 1), (False, True) * (tn // (2 * LANES))), seq=seq, name="proj_mla_q")
    kv = _mm([ckv], [_W(wkv)], wkv.shape[1], out_dtype=BF, tm=512, tn=tn, name="proj_mla_kv")

    q3 = q.reshape(batch, seq, -1)
    kv3 = kv.reshape(batch, seq, -1)
    kr3 = kr.reshape(batch, seq, LANES)
    o = _attention(q3, 0, [kv3, kr3], [0, 0], [True, False], kv3, MLA_HEADS, batch=batch, seq=seq,
                   heads=MLA_HEADS, scale=qk ** -0.5, name="mla_attention")
    return _mm([o.reshape(n, MLA_HEADS * MLA_V)], [_W(w_out, layer=layer)], D_MODEL, out_dtype=F32, tm=512, tn=D_MODEL,
               res=x, norm_out=g_next, name="proj_out_mla")


def _ffn(x, norms, w_in, w_out, layer, xn=None):
    depth, d = norms.shape
    a, w_out_bf = _ffn_up(x if xn is None else xn, norms.reshape(depth, 1, d), w_in, w_out, layer)
    return _mm([a], [_W(w_out_bf)], d, out_dtype=F32, tm=1024, tn=512, res=x, name="ffn_down")


def kernel(x, ffn1_norm, ffn1_w_in, ffn1_w_out, mix_norm, ffn2_norm, ffn2_w_in, ffn2_w_out, hyb_w_in, hyb_w_out, nsa_cmp_pe, nsa_cmp_w1, nsa_cmp_w2, fox_f_bias, mla_w_in, mla_q_norm, mla_kv_norm, mla_w_uq, mla_w_ukv, mla_w_out, final_norm):
    batch, seq, d = x.shape
    depth = ffn1_norm.shape[0]
    x = x.reshape(batch * seq, d)
    for i in range(depth):
        x = _ffn(x, ffn1_norm, ffn1_w_in, ffn1_w_out, i)
        if i % 2 == 0:
            e = i // 2
            x, xn = _hybrid_mixer(x, mix_norm[i], ffn2_norm[i], hyb_w_in, hyb_w_out, e, nsa_cmp_pe[e], nsa_cmp_w1[e],
                                  nsa_cmp_w2[e], fox_f_bias[e], batch=batch, seq=seq)
        else:
            o = i // 2
            x, xn = _mla_mixer(x, mix_norm[i], ffn2_norm[i], mla_w_in[o], mla_q_norm[o], mla_kv_norm[o], mla_w_uq[o],
                               mla_w_ukv[o], mla_w_out, o, batch=batch, seq=seq)
        x = _ffn(x, ffn2_norm, ffn2_w_in, ffn2_w_out, i, xn=xn)
    return _rmsnorm(x, final_norm).reshape(batch, seq, d)
```

```python
import functools

import numpy as np
import jax
import jax.numpy as jnp
from jax import lax
from jax.experimental import pallas as pl
from jax.experimental.pallas import tpu as pltpu

D_MODEL = 2048
HEAD_DIM = 128
ROPE_THETA = 10000.0
NORM_EPS = 1e-6
NEG = -1e30

NSA_HEADS = 8
NSA_GROUPS = 2
HEADS_PER_GROUP = NSA_HEADS // NSA_GROUPS
CMP_LEN = 32
CMP_STRIDE = 16
CMP_HIDDEN = 256
SLC_LEN = 64
SLC_TOP = 8
WIN = 512
FOX_HEADS = 8
MLA_HEADS = 16
MLA_Q_RANK = 512
MLA_KV_RANK = 512
MLA_NOPE = 128
MLA_ROPE = 64
MLA_V = 128
D_FF = 5632

LANES = 128
VMEM_LIMIT = 56 * 1024 * 1024

BF = jnp.bfloat16
F32 = jnp.float32


def _params(sem, vmem=VMEM_LIMIT):
    return pltpu.CompilerParams(dimension_semantics=sem, vmem_limit_bytes=vmem)


def _rope_lanes(x, cos, sin):
    return x * cos + pltpu.roll(x, LANES // 2, axis=1) * sin


def _dot_t(a, b):
    return lax.dot_general(a, b, (((1,), (1,)), ((), ())), preferred_element_type=F32)


def _softmax_rows(s):
    m = jnp.max(s, axis=-1, keepdims=True)
    e = jnp.exp(s - m)
    return e, jnp.sum(e, axis=-1, keepdims=True)


def _rms_normed(x, g):
    ms = jnp.mean(x * x, axis=-1, keepdims=True)
    return x * lax.rsqrt(ms + NORM_EPS) * g


def _ffn_up_kernel(x_ref, g_ref, wa_ref, wb_ref, w2_ref, o_ref, w2b_ref, *scratch, n_cast):
    i, j = pl.program_id(0), pl.program_id(1)

    if scratch:
        xn_ref, = scratch

        @pl.when(j == 0)
        def _():
            xn_ref[...] = _rms_normed(x_ref[...], g_ref[...]).astype(BF)
    else:
        xn_ref = x_ref

    @pl.when(i * pl.num_programs(1) + j < n_cast)
    def _():
        w2b_ref[...] = w2_ref[...].astype(BF)

    xn = xn_ref[...]
    h1 = jnp.dot(xn, wa_ref[...].astype(BF), preferred_element_type=F32)
    h2 = jnp.dot(xn, wb_ref[...].astype(BF), preferred_element_type=F32)
    o_ref[...] = (h1 * jax.nn.sigmoid(h1) * h2 * 0.5).astype(BF)


def _ffn_up(x, g, w_in, w_out, layer, *, tm=1024, tf=512, cast_rows=256):
    n, d = x.shape
    n_f = D_FF // tf
    n_cast = D_FF // cast_rows
    assert n_cast <= (n // tm) * n_f
    chunk = lambda i, j: jnp.minimum(i * n_f + j, n_cast - 1)
    return pl.pallas_call(
        functools.partial(_ffn_up_kernel, n_cast=n_cast),
        grid=(n // tm, n_f),
        in_specs=[
            pl.BlockSpec((tm, d), lambda i, j: (i, 0)),
            pl.BlockSpec((None, 1, d), lambda i, j: (layer, 0, 0)),
            pl.BlockSpec((None, d, tf), lambda i, j: (layer, 0, j)),
            pl.BlockSpec((None, d, tf), lambda i, j: (layer, 0, n_f + j)),
            pl.BlockSpec((None, cast_rows, d), lambda i, j: (layer, chunk(i, j), 0)),
        ],
        out_specs=[pl.BlockSpec((tm, tf), lambda i, j: (i, j)),
                   pl.BlockSpec((cast_rows, d), lambda i, j: (chunk(i, j), 0))],
        out_shape=[jax.ShapeDtypeStruct((n, D_FF), BF), jax.ShapeDtypeStruct((D_FF, d), BF)],
        scratch_shapes=[pltpu.VMEM((tm, d), BF)] if x.dtype == F32 else [],
        compiler_params=_params(("arbitrary", "arbitrary")),
        name="ffn_up",
    )(x, g, w_in, w_in, w_out)


class _W:
    def __init__(self, arr, *, layer=None, rows=None, row_blk=0, col_blk0=0, row_starts=None):
        self.arr, self.layer, self.row_blk, self.col_blk0 = arr, layer, row_blk, col_blk0
        self.transposed = row_starts is not None
        self.row_starts = row_starts
        assert row_starts is None or all(r % 8 == 0 for r in row_starts)
        self.rows = rows if rows is not None else arr.shape[-2]

    def spec(self, tn, single_buffer=False):
        layer, row_blk, col_blk0 = self.layer, self.row_blk, self.col_blk0
        if self.transposed:
            starts = self.row_starts

            def start(j):
                off = starts[-1]
                for k in range(len(starts) - 2, -1, -1):
                    off = jnp.where(j == k, starts[k], off)
                return pl.multiple_of(off, 8)

            return pl.BlockSpec((pl.Element(tn), pl.Element(self.arr.shape[1])), lambda i, j: (start(j), 0))
        kw = dict(pipeline_mode=pl.Buffered(1)) if single_buffer else {}
        if self.arr.ndim == 3:
            return pl.BlockSpec((None, self.rows, tn), lambda i, j: (layer, row_blk, col_blk0 + j), **kw)
        return pl.BlockSpec((self.rows, tn), lambda i, j: (row_blk, col_blk0 + j), **kw)


def _mm_kernel(*refs, n_a, has_res, has_norm_out, keep_raw, rope_j, rope_chunks):
    a_refs = refs[:n_a]
    w_refs = refs[n_a:2 * n_a]
    k = 2 * n_a
    if has_res:
        res_ref = refs[k]
        k += 1
    if has_norm_out:
        g_ref = refs[k]
        k += 1
    if rope_chunks is not None:
        cos_ref, sin_ref = refs[k], refs[k + 1]
        k += 2
    out_refs = refs[k:]

    acc = jnp.dot(a_refs[0][...], w_refs[0][...].astype(BF), preferred_element_type=F32)
    for a_ref, w_ref in zip(a_refs[1:], w_refs[1:]):
        acc += jnp.dot(a_ref[...], w_ref[...].astype(BF), preferred_element_type=F32)
    if has_res:
        acc += res_ref[...]

    if has_norm_out:
        out_refs[-1][...] = _rms_normed(acc, g_ref[...]).astype(out_refs[-1].dtype)
    if not keep_raw:
        return
    o_ref = out_refs[0]
    if rope_chunks is None:
        o_ref[...] = acc.astype(o_ref.dtype)
    else:
        j = pl.program_id(1)
        is_rope = (j >= rope_j[0]) & (j < rope_j[1])
        cos, sin = cos_ref[...], sin_ref[...]
        for c, on in enumerate(rope_chunks):
            blk = acc[:, c * LANES:(c + 1) * LANES]
            if on:
                blk = jnp.where(is_rope, _rope_lanes(blk, cos, sin), blk)
            o_ref[:, c * LANES:(c + 1) * LANES] = blk.astype(o_ref.dtype)


def _mm(a_list, w_list, n_out, *, out_dtype, tm, tn, res=None, norm_out=None, keep_raw=True, rope=None, seq=None,
        name="mm"):
    n = a_list[0].shape[0]
    n_a = len(a_list)
    in_specs, args = [], []
    for a in a_list:
        in_specs.append(pl.BlockSpec((tm, a.shape[1]), lambda i, j: (i, 0)))
        args.append(a)
    for w in w_list:
        assert not w.transposed
        in_specs.append(w.spec(tn, single_buffer=tn == n_out))
        args.append(w.arr)
    if res is not None:
        in_specs.append(pl.BlockSpec((tm, tn), lambda i, j: (i, j)))
        args.append(res)
    out_specs, out_shape = [], []
    if keep_raw:
        out_specs.append(pl.BlockSpec((tm, tn), lambda i, j: (i, j)))
        out_shape.append(jax.ShapeDtypeStruct((n, n_out), out_dtype))
    if norm_out is not None:
        assert tn == n_out
        g, norm_dtype = norm_out
        in_specs.append(pl.BlockSpec((1, n_out), lambda i, j: (0, 0)))
        args.append(g.reshape(1, n_out))
        out_specs.append(pl.BlockSpec((tm, tn), lambda i, j: (i, j)))
        out_shape.append(jax.ShapeDtypeStruct((n, n_out), norm_dtype))
    rope_j = rope_chunks = None
    if rope is not None:
        cos, sin, rope_j, rope_chunks = rope
        n_seq = seq // tm
        for t in (cos, sin):
            in_specs.append(pl.BlockSpec((tm, LANES), lambda i, j: (i % n_seq, 0)))
            args.append(t)
    outs = pl.pallas_call(
        functools.partial(_mm_kernel, n_a=n_a, has_res=res is not None, has_norm_out=norm_out is not None,
                          keep_raw=keep_raw, rope_j=rope_j, rope_chunks=rope_chunks),
        grid=(n // tm, n_out // tn),
        in_specs=in_specs,
        out_specs=out_specs,
        out_shape=out_shape,
        compiler_params=_params(("parallel", "arbitrary")),
        name=name,
    )(*args)
    return outs if len(outs) > 1 else outs[0]


def _hybrid_proj_kernel(xn_ref, w_ref, *rest, n_main, n_side, rope_j, rope_chunks):
    side_refs = rest[:n_side]
    cos_ref, sin_ref, o_ref, p_ref = rest[n_side:]
    j = pl.program_id(1)

    @pl.when(j < n_main)
    def _():
        acc = _dot_t(xn_ref[...], w_ref[...].astype(BF))
        is_rope = (j >= rope_j[0]) & (j < rope_j[1])
        cos, sin = cos_ref[...], sin_ref[...]
        for c, on in enumerate(rope_chunks):
            blk = acc[:, c * LANES:(c + 1) * LANES]
            if on:
                blk = jnp.where(is_rope, _rope_lanes(blk, cos, sin), blk)
            o_ref[:, c * LANES:(c + 1) * LANES] = blk.astype(o_ref.dtype)

    @pl.when(j == n_main)
    def _():
        w_side = jnp.concatenate([w[...] for w in side_refs], axis=0).astype(BF)
        s = _dot_t(xn_ref[...], w_side)
        for p in range(s.shape[1] // LANES):
            p_ref[p] = s[:, p * LANES:(p + 1) * LANES]


def _hybrid_proj(xn, wt, main_starts, side_windows, rope, *, seq, tm=1024, tn=512):
    n, d = xn.shape
    cos, sin, rope_j, rope_chunks = rope
    n_main = len(main_starts)
    n_planes = sum(rows for _, rows in side_windows) // LANES
    n_seq = seq // tm
    last = n_main - 1
    side_specs = [pl.BlockSpec((pl.Element(rows), pl.Element(d)), lambda i, j, start=start: (start, 0),
                               pipeline_mode=pl.Buffered(1)) for start, rows in side_windows]
    return pl.pallas_call(
        functools.partial(_hybrid_proj_kernel, n_main=n_main, n_side=len(side_windows), rope_j=rope_j,
                          rope_chunks=rope_chunks),
        grid=(n // tm, n_main + 1),
        in_specs=[
            pl.BlockSpec((tm, d), lambda i, j: (i, 0)),
            _W(wt, row_starts=main_starts).spec(tn),
            *side_specs,
            pl.BlockSpec((tm, LANES), lambda i, j: (i % n_seq, 0)),
            pl.BlockSpec((tm, LANES), lambda i, j: (i % n_seq, 0)),
        ],
        out_specs=[pl.BlockSpec((tm, tn), lambda i, j: (i, jnp.minimum(j, last))),
                   pl.BlockSpec((n_planes, tm, LANES), lambda i, j: (0, i, 0))],
        out_shape=[jax.ShapeDtypeStruct((n, n_main * tn), BF), jax.ShapeDtypeStruct((n_planes, n, LANES), F32)],
        compiler_params=_params(("parallel", "arbitrary")),
        name="proj_hybrid",
    )(xn, wt, *([wt] * len(side_windows)), cos, sin)


def _compress_kernel(x_ref, pe_ref, w1_ref, w2_ref, o_ref):
    n_blk = o_ref.shape[2]
    ha = jnp.zeros((n_blk, CMP_HIDDEN), F32)
    hb = jnp.zeros((n_blk, CMP_HIDDEN), F32)
    for l in range(CMP_STRIDE):
        xl = x_ref[0, pl.ds(l, n_blk, stride=CMP_STRIDE), :]
        wa = w1_ref[0, l * HEAD_DIM:(l + 1) * HEAD_DIM, :].astype(BF)
        wb = w1_ref[0, (CMP_STRIDE + l) * HEAD_DIM:(CMP_STRIDE + l + 1) * HEAD_DIM, :].astype(BF)
        ha += jnp.dot((xl + pe_ref[0, l:l + 1, :]).astype(BF), wa, preferred_element_type=F32)
        hb += jnp.dot((xl + pe_ref[0, CMP_STRIDE + l:CMP_STRIDE + l + 1, :]).astype(BF), wb,
                      preferred_element_type=F32)
    hid = ha + pltpu.roll(hb, n_blk - 1, axis=0)
    act = jax.nn.gelu(hid, approximate=True).astype(BF)
    o_ref[0, 0] = jnp.dot(act, w2_ref[0].astype(BF), preferred_element_type=F32)


def _compress(planes, pe, w1, w2, *, batch, seq):
    rows = seq // CMP_STRIDE
    return pl.pallas_call(
        _compress_kernel,
        grid=(2 * NSA_GROUPS, batch),
        in_specs=[
            pl.BlockSpec((1, seq, HEAD_DIM), lambda c, b: (c, b, 0)),
            pl.BlockSpec((1, CMP_LEN, HEAD_DIM), lambda c, b: (c // NSA_GROUPS, 0, 0)),
            pl.BlockSpec((1, CMP_LEN * HEAD_DIM, CMP_HIDDEN), lambda c, b: (c // NSA_GROUPS, 0, 0)),
            pl.BlockSpec((1, CMP_HIDDEN, HEAD_DIM), lambda c, b: (c // NSA_GROUPS, 0, 0)),
        ],
        out_specs=pl.BlockSpec((1, 1, rows, HEAD_DIM), lambda c, b: (c, b, 0, 0)),
        out_shape=jax.ShapeDtypeStruct((2 * NSA_GROUPS, batch, rows, HEAD_DIM), F32),
        compiler_params=_params(("arbitrary", "arbitrary")),
        name="nsa_compress",
    )(planes, pe, w1, w2)


def _stacked_softmax_pv(s_raw, bias, v, scale, n_stack, tq):
    exp2_scale = scale * float(np.log2(np.e))
    es, ls = [], []
    for h in range(n_stack):
        u = s_raw[h * tq:(h + 1) * tq] + bias
        e = jnp.exp2((u - jnp.max(u, axis=-1, keepdims=True)) * exp2_scale)
        ls.append(jnp.sum(e, axis=-1, keepdims=True))
        es.append(e.astype(BF))
    o = jnp.dot(jnp.concatenate(es, axis=0), v, preferred_element_type=F32)
    return o / jnp.concatenate(ls, axis=0)


def _nsa_kernel(q_ref, ks_ref, kw_ref, vs_ref, vw_ref, kvc_ref, gate_ref, cos_ref, sin_ref, m_ref, o_ref,
                *, tq, seq, key_step):
    qi = pl.program_id(1)
    hg = HEADS_PER_GROUP
    scale = HEAD_DIM ** -0.5
    n_cmp = (seq - CMP_LEN) // CMP_STRIDE + 1
    n_slc = seq // SLC_LEN
    band = WIN + tq

    t_row = qi * tq + lax.broadcasted_iota(jnp.int32, (tq, 1), 0)
    cos = jnp.concatenate([cos_ref[...]] * hg, axis=0)
    sin = jnp.concatenate([sin_ref[...]] * hg, axis=0)
    gates = jax.nn.sigmoid(gate_ref[0])

    n_lane = lax.broadcasted_iota(jnp.int32, (1, LANES), 1)
    cmask = (n_lane * CMP_STRIDE + (CMP_LEN - 1) <= t_row) & (n_lane < n_cmp)
    cbias = jnp.where(cmask, 0.0, NEG)
    j_blk = lax.broadcasted_iota(jnp.int32, (n_slc, 1), 0)
    slc_shift = SLC_LEN.bit_length() - 1
    cur = jnp.right_shift(qi * tq + lax.broadcasted_iota(jnp.int32, (1, tq), 1), slc_shift)
    forced = (j_blk == 0) | (j_blk == cur) | (j_blk == cur - 1)
    w_start = pl.multiple_of(jnp.maximum(qi * tq - WIN, 0), tq)
    w_pos = w_start + lax.broadcasted_iota(jnp.int32, (1, band), 1)
    wbias = jnp.where((w_pos <= t_row) & (w_pos > t_row - WIN), 0.0, NEG)

    def group_queries(g):
        qg = q_ref[0, :, g * hg * HEAD_DIM:(g + 1) * hg * HEAD_DIM].astype(F32)
        qp = jnp.concatenate([qg[:, h * HEAD_DIM:(h + 1) * HEAD_DIM] for h in range(hg)], axis=0)
        return qp.astype(BF), _rope_lanes(qp, cos, sin).astype(BF)

    def compressed_branch(g, qp):
        kc = kvc_ref[g, 0].astype(BF)
        vc = kvc_ref[NSA_GROUPS + g, 0].astype(BF)
        s = _dot_t(qp, kc)
        ps = []
        for h in range(hg):
            u = s[h * tq:(h + 1) * tq] * scale + cbias
            e, l = _softmax_rows(u)
            ps.append(jnp.where(cmask, e / l, 0.0))
        o_cmp = jnp.dot(jnp.concatenate(ps, axis=0).astype(BF), vc, preferred_element_type=F32)

        p_sum = ps[0]
        for h in range(1, hg):
            p_sum = p_sum + ps[h]
        imp = lax.dot_general(m_ref[...], p_sum, (((1,), (1,)), ((), ())), preferred_element_type=F32,
                              precision=lax.Precision.HIGHEST)
        imp = jnp.where(j_blk > cur, -jnp.inf, jnp.where(forced, jnp.inf, imp))
        rank = jnp.zeros((n_slc, tq), jnp.int32)
        for i in range(n_slc):
            ci = imp[i:i + 1, :]
            beats = (ci > imp) | ((ci == imp) & (j_blk > i))
            rank = rank + beats.astype(jnp.int32)
        return o_cmp, jnp.where(rank < SLC_TOP, 0.0, NEG).astype(BF)

    qs = [group_queries(g) for g in range(NSA_GROUPS)]
    cmps = [compressed_branch(g, qs[g][0]) for g in range(NSA_GROUPS)]

    def attend(width):
        blocks = width // SLC_LEN
        expand = jnp.where(jnp.right_shift(lax.broadcasted_iota(jnp.int32, (blocks, width), 1), slc_shift)
                           == lax.broadcasted_iota(jnp.int32, (blocks, width), 0), 1.0, 0.0).astype(BF)
        s_pos = lax.broadcasted_iota(jnp.int32, (1, width), 1)
        lanes = [slice(g * HEAD_DIM, (g + 1) * HEAD_DIM) for g in range(NSA_GROUPS)]
        s_win = [_dot_t(qs[g][1], kw_ref[0, pl.ds(w_start, band), lanes[g]]) for g in range(NSA_GROUPS)]
        s_slc = [_dot_t(qs[g][1], ks_ref[0, :width, lanes[g]]) for g in range(NSA_GROUPS)]
        for g in range(NSA_GROUPS):
            o_win = _stacked_softmax_pv(s_win[g], wbias, vw_ref[0, pl.ds(w_start, band), lanes[g]], scale, hg, tq)
            o_cmp, sel_neg = cmps[g]
            bias = lax.dot_general(sel_neg[:blocks], expand, (((0,), (0,)), ((), ())),
                                   preferred_element_type=F32)
            bias = jnp.where(s_pos <= t_row, bias, NEG)
            o_slc = _stacked_softmax_pv(s_slc[g], bias, vs_ref[0, :width, lanes[g]], scale, hg, tq)
            for hh in range(hg):
                h = g * hg + hh
                rows = slice(hh * tq, (hh + 1) * tq)
                o = (o_cmp[rows] * gates[:, 3 * h:3 * h + 1]
                     + o_slc[rows] * gates[:, 3 * h + 1:3 * h + 2]
                     + o_win[rows] * gates[:, 3 * h + 2:3 * h + 3])
                o_ref[0, :, h * HEAD_DIM:(h + 1) * HEAD_DIM] = o.astype(o_ref.dtype)

    n_widths = seq // key_step
    super_row = qi // (key_step // tq)
    for k in range(n_widths):
        pl.when(super_row == k)(functools.partial(attend, (k + 1) * key_step))


def _nsa(main, kvc, planes, gate_plane, cos, sin, m_cs, *, batch, seq, tq=128, key_step=512):
    nq = seq // tq
    kv_w = NSA_GROUPS * HEAD_DIM
    kv0 = NSA_HEADS * HEAD_DIM // kv_w
    return pl.pallas_call(
        functools.partial(_nsa_kernel, tq=tq, seq=seq, key_step=key_step),
        grid=(batch, nq),
        in_specs=[
            pl.BlockSpec((1, tq, NSA_HEADS * HEAD_DIM), lambda b, i: (b, i, 0)),
            pl.BlockSpec((1, seq, kv_w), lambda b, i: (b, 0, kv0)),
            pl.BlockSpec((1, seq, kv_w), lambda b, i: (b, 0, kv0 + 2)),
            pl.BlockSpec((1, seq, kv_w), lambda b, i: (b, 0, kv0 + 1)),
            pl.BlockSpec((1, seq, kv_w), lambda b, i: (b, 0, kv0 + 3)),
            pl.BlockSpec((2 * NSA_GROUPS, 1, LANES, HEAD_DIM), lambda b, i: (0, b, 0, 0)),
            pl.BlockSpec((1, tq, LANES), lambda b, i: (gate_plane, b * nq + i, 0)),
            pl.BlockSpec((tq, LANES), lambda b, i: (i, 0)),
            pl.BlockSpec((tq, LANES), lambda b, i: (i, 0)),
            pl.BlockSpec(m_cs.shape, lambda b, i: (0, 0)),
        ],
        out_specs=pl.BlockSpec((1, tq, NSA_HEADS * HEAD_DIM), lambda b, i: (b, i, 0)),
        out_shape=jax.ShapeDtypeStruct((batch, seq, NSA_HEADS * HEAD_DIM), BF),
        compiler_params=_params(("parallel", "arbitrary")),
        name="nsa_attention",
    )(main, main, main, main, main, kvc, planes, cos, sin, m_cs)


def _cumdecay_kernel(x_ref, b_ref, o_ref):
    z = x_ref[0] + b_ref[...]
    c = jnp.minimum(z, 0.0) - jnp.log1p(jnp.exp(-jnp.abs(z)))
    rows = c.shape[0]
    r = lax.broadcasted_iota(jnp.int32, (rows, 1), 0)
    k = 1
    while k < rows:
        c = c + jnp.where(r >= k, pltpu.roll(c, k, axis=0), 0.0)
        k *= 2
    o_ref[0] = c


def _cumdecay(planes, plane, bias, *, batch, seq):
    return pl.pallas_call(
        _cumdecay_kernel,
        grid=(batch,),
        in_specs=[pl.BlockSpec((1, seq, LANES), lambda b: (plane, b, 0)),
                  pl.BlockSpec((1, LANES), lambda b: (0, 0))],
        out_specs=pl.BlockSpec((1, seq, LANES), lambda b: (b, 0, 0)),
        out_shape=jax.ShapeDtypeStruct((batch, seq, LANES), F32),
        compiler_params=_params(("arbitrary",)),
        name="fox_cumdecay",
    )(planes, bias)


def _attn_kernel(*refs, tq, seq, scale, n_qk, decay):
    q_ref = refs[0]
    k_refs = refs[1:1 + n_qk]
    v_ref = refs[1 + n_qk]
    k = 2 + n_qk
    if decay:
        ck_ref = refs[k]
        k += 1
    o_ref = refs[k]

    row = lax.broadcasted_iota(jnp.int32, (tq, 1), 0)
    col = lax.broadcasted_iota(jnp.int32, (1, tq), 1)
    tri = col <= row
    exp2_scale = scale * float(np.log2(np.e))

    k_all = k_refs[0][0] if n_qk == 1 else jnp.concatenate([k_ref[0] for k_ref in k_refs], axis=1)

    n_tiles = seq // tq
    scores = lambda i: _dot_t(q_ref[0, i * tq:(i + 1) * tq, :], k_all[:(i + 1) * tq])
    s_next = scores(n_tiles - 1)
    for i in reversed(range(n_tiles)):
        lo, hi = i * tq, (i + 1) * tq
        s = s_next
        if i > 0:
            s_next = scores(i - 1)
        if decay:
            ck = ck_ref[0, :, :hi]
            cq = jnp.sum(jnp.where(col == row, ck[:, lo:hi], 0.0), axis=1, keepdims=True)
            s = s * scale + (cq - ck)
            prob = lambda u, m: jnp.exp(u - m)
        else:
            prob = lambda u, m: jnp.exp2((u - m) * exp2_scale)
        s_diag = jnp.where(tri, s[:, lo:hi], NEG)
        m = jnp.max(s_diag, axis=-1, keepdims=True)
        if i > 0:
            m = jnp.maximum(m, jnp.max(s[:, :lo], axis=-1, keepdims=True))
        e_diag = prob(s_diag, m)
        l = jnp.sum(e_diag, axis=-1, keepdims=True)
        e = e_diag.astype(BF)
        if i > 0:
            e_off = prob(s[:, :lo], m)
            l = l + jnp.sum(e_off, axis=-1, keepdims=True)
            e = jnp.concatenate([e_off.astype(BF), e], axis=1)
        o = jnp.dot(e, v_ref[0, :hi, :], preferred_element_type=F32) / l
        o_ref[0, lo:hi, :] = o.astype(o_ref.dtype)


def _attention(q_arr, q_blk0, k_arrs, k_blk0s, k_per_head, v_arr, v_blk0, *, batch, seq, heads, scale,
               decay=None, tq=256, name="attention"):
    n_qk = len(k_arrs)
    in_specs = [pl.BlockSpec((1, seq, n_qk * HEAD_DIM), lambda b, h: (b, 0, q_blk0 + h))]
    args = [q_arr]
    for arr, blk0, per_head in zip(k_arrs, k_blk0s, k_per_head):
        if per_head:
            in_specs.append(pl.BlockSpec((1, seq, HEAD_DIM), lambda b, h, blk0=blk0: (b, 0, blk0 + h)))
        else:
            in_specs.append(pl.BlockSpec((1, seq, HEAD_DIM), lambda b, h, blk0=blk0: (b, 0, blk0)))
        args.append(arr)
    in_specs.append(pl.BlockSpec((1, seq, HEAD_DIM), lambda b, h: (b, 0, v_blk0 + h)))
    args.append(v_arr)
    if decay is not None:
        in_specs.append(pl.BlockSpec((1, 1, seq), lambda b, h: (b * heads + h, 0, 0)))
        args.append(decay)
    return pl.pallas_call(
        functools.partial(_attn_kernel, tq=tq, seq=seq, scale=scale, n_qk=n_qk, decay=decay is not None),
        grid=(batch, heads),
        in_specs=in_specs,
        out_specs=pl.BlockSpec((1, seq, HEAD_DIM), lambda b, h: (b, 0, h)),
        out_shape=jax.ShapeDtypeStruct((batch, seq, heads * HEAD_DIM), BF),
        compiler_params=_params(("parallel", "arbitrary")),
        name=name,
    )(*args)


def _mla_latent_kernel(xn_ref, wt_ref, wkr_ref, gq_ref, gkv_ref, cos_ref, sin_ref, cq_ref, ckv_ref, kr_ref):
    h = xn_ref[...]
    c = _dot_t(h, wt_ref[...].astype(BF))
    cq_ref[...] = _rms_normed(c[:, :MLA_Q_RANK], gq_ref[...]).astype(BF)
    ckv_ref[...] = _rms_normed(c[:, MLA_Q_RANK:], gkv_ref[...]).astype(BF)
    kr_ref[...] = _rope_lanes(_dot_t(h, wkr_ref[...].astype(BF)), cos_ref[...], sin_ref[...]).astype(BF)


def _mla_latent(xn, wt, w_kr, gq, gkv, cos, sin, *, seq, tm=512):
    n = xn.shape[0]
    n_seq = seq // tm
    return pl.pallas_call(
        _mla_latent_kernel,
        grid=(n // tm,),
        in_specs=[
            pl.BlockSpec((tm, xn.shape[1]), lambda i: (i, 0)),
            pl.BlockSpec((MLA_Q_RANK + MLA_KV_RANK, wt.shape[1]), lambda i: (0, 0)),
            pl.BlockSpec(w_kr.shape, lambda i: (0, 0)),
            pl.BlockSpec((1, MLA_Q_RANK), lambda i: (0, 0)),
            pl.BlockSpec((1, MLA_KV_RANK), lambda i: (0, 0)),
            pl.BlockSpec((tm, LANES), lambda i: (i % n_seq, 0)),
            pl.BlockSpec((tm, LANES), lambda i: (i % n_seq, 0)),
        ],
        out_specs=[
            pl.BlockSpec((tm, MLA_Q_RANK), lambda i: (i, 0)),
            pl.BlockSpec((tm, MLA_KV_RANK), lambda i: (i, 0)),
            pl.BlockSpec((tm, LANES), lambda i: (i, 0)),
        ],
        out_shape=[
            jax.ShapeDtypeStruct((n, MLA_Q_RANK), BF),
            jax.ShapeDtypeStruct((n, MLA_KV_RANK), BF),
            jax.ShapeDtypeStruct((n, LANES), BF),
        ],
        compiler_params=_params(("parallel",)),
        name="mla_latent",
    )(xn, wt, w_kr, gq.reshape(1, -1), gkv.reshape(1, -1), cos, sin)


def _rope_tables(seq, dim):
    half = dim // 2
    inv = 1.0 / (ROPE_THETA ** (np.arange(0, dim, 2, dtype=np.float32) / dim))
    ang = np.arange(seq, dtype=np.float32)[:, None] * inv.astype(np.float32)[None, :]
    cos = np.zeros((seq, LANES), np.float32)
    sin = np.zeros((seq, LANES), np.float32)
    cos[:, :half] = np.cos(ang)
    cos[:, LANES // 2:LANES // 2 + half] = np.cos(ang)
    sin[:, :half] = -np.sin(ang)
    sin[:, LANES // 2:LANES // 2 + half] = np.sin(ang)
    return jnp.asarray(cos), jnp.asarray(sin)


def _cmp_to_slc(seq):
    n_cmp = (seq - CMP_LEN) // CMP_STRIDE + 1
    n_slc = seq // SLC_LEN
    c0 = np.arange(n_cmp) * CMP_STRIDE
    c1 = c0 + CMP_LEN
    s0 = np.arange(n_slc) * SLC_LEN
    s1 = s0 + SLC_LEN
    ov = np.clip(np.minimum(c1[:, None], s1[None, :]) - np.maximum(c0[:, None], s0[None, :]), 0, None)
    m = np.zeros((n_slc, LANES), np.float32)
    m[:, :n_cmp] = (ov / CMP_LEN).T
    return jnp.asarray(m)


def _spread_rope_cols(w):
    half = MLA_ROPE // 2
    z = jnp.zeros(w.shape[:-1] + (LANES // 2 - half,), w.dtype)
    return jnp.concatenate([w[..., :half], z, w[..., half:], z], axis=-1)


def _hybrid_mixer(x, xn, g_next, w_in, w_out, layer, cmp_pe, cmp_w1, cmp_w2, f_bias, *, batch, seq):
    n = batch * seq
    nq_w = NSA_HEADS * HEAD_DIM
    kv_w = NSA_GROUPS * HEAD_DIM
    fx_w = FOX_HEADS * HEAD_DIM
    c_cmp, c_slc, c_gate = nq_w, nq_w + 2 * kv_w, nq_w + 6 * kv_w
    c_fox = c_gate + 3 * NSA_HEADS
    c_f = c_fox + 3 * fx_w
    wt = jnp.swapaxes(w_in[layer], 0, 1)

    cos, sin = _rope_tables(seq, HEAD_DIM)
    tn = 2 * kv_w
    main_starts = tuple(range(0, c_cmp, tn)) + tuple(range(c_slc, c_gate, tn)) + tuple(range(c_fox, c_f, tn))
    f_lane0 = LANES - FOX_HEADS
    side_windows = ((c_cmp, c_slc - c_cmp), (c_gate, LANES), (c_f - f_lane0, LANES))
    gate_plane, f_plane = (c_slc - c_cmp) // LANES, (c_slc - c_cmp) // LANES + 1
    main, planes = _hybrid_proj(xn, wt, main_starts, side_windows,
                                (cos, sin, (nq_w // tn, nq_w // tn + 2), (True, True, False, False)), seq=seq, tn=tn)

    kvc = _compress(planes, cmp_pe, cmp_w1, cmp_w2, batch=batch, seq=seq)

    main3 = main.reshape(batch, seq, -1)
    o_nsa = _nsa(main3, kvc, planes, gate_plane, cos, sin, _cmp_to_slc(seq), batch=batch, seq=seq)

    bias = jnp.zeros((1, LANES), F32).at[0, f_lane0:].set(f_bias)
    cum = _cumdecay(planes, f_plane, bias, batch=batch, seq=seq)[:, :, f_lane0:]
    cum_t = jnp.swapaxes(cum, 1, 2).reshape(batch * FOX_HEADS, seq)
    blk = lambda col: col // HEAD_DIM
    c_qf = nq_w + 4 * kv_w
    o_fox = _attention(main3, blk(c_qf), [main3], [blk(c_qf + fx_w)], [True], main3, blk(c_qf + 2 * fx_w),
                       batch=batch, seq=seq, heads=FOX_HEADS, scale=HEAD_DIM ** -0.5,
                       decay=cum_t[:, None, :], name="fox_attention")

    return _mm([o_nsa.reshape(n, nq_w), o_fox.reshape(n, fx_w)],
               [_W(w_out, layer=layer, rows=nq_w, row_blk=0), _W(w_out, layer=layer, rows=fx_w, row_blk=1)],
               D_MODEL, out_dtype=F32, tm=512, tn=D_MODEL, res=x, norm_out=(g_next, BF), name="proj_out_hybrid")


def _mla_mixer(x, xn, g_next, w_in, q_norm, kv_norm, w_uq, w_ukv, w_out, layer, *, batch, seq):
    n = batch * seq
    qk = MLA_NOPE + MLA_ROPE
    wt_lat = jnp.swapaxes(w_in, 0, 1)
    w_kr = jnp.swapaxes(_spread_rope_cols(w_in[:, MLA_Q_RANK + MLA_KV_RANK:]), 0, 1)
    wq = w_uq.reshape(MLA_Q_RANK, MLA_HEADS, qk)
    wq = jnp.concatenate([wq[..., :MLA_NOPE], _spread_rope_cols(wq[..., MLA_NOPE:])], axis=-1)
    wq = wq.reshape(MLA_Q_RANK, MLA_HEADS * 2 * HEAD_DIM)
    wkv = w_ukv.reshape(MLA_KV_RANK, MLA_HEADS, MLA_NOPE + MLA_V)
    wkv = jnp.concatenate([wkv[..., :MLA_NOPE].reshape(MLA_KV_RANK, -1),
                           wkv[..., MLA_NOPE:].reshape(MLA_KV_RANK, -1)], axis=1)

    cos, sin = _rope_tables(seq, MLA_ROPE)
    cq, ckv, kr = _mla_latent(xn, wt_lat, w_kr, q_norm, kv_norm, cos, sin, seq=seq)
    tn = wq.shape[1]
    q = _mm([cq], [_W(wq)], wq.shape[1], out_dtype=BF, tm=512, tn=tn,
            rope=(cos, sin, (0, 1), (False, True) * (tn // (2 * LANES))), seq=seq, name="proj_mla_q")
    kv = _mm([ckv], [_W(wkv)], wkv.shape[1], out_dtype=BF, tm=512, tn=tn, name="proj_mla_kv")

    q3 = q.reshape(batch, seq, -1)
    kv3 = kv.reshape(batch, seq, -1)
    kr3 = kr.reshape(batch, seq, LANES)
    o = _attention(q3, 0, [kv3, kr3], [0, 0], [True, False], kv3, MLA_HEADS, batch=batch, seq=seq,
                   heads=MLA_HEADS, scale=qk ** -0.5, name="mla_attention")
    return _mm([o.reshape(n, MLA_HEADS * MLA_V)], [_W(w_out, layer=layer)], D_MODEL, out_dtype=F32, tm=512, tn=D_MODEL,
               res=x, norm_out=(g_next, BF), name="proj_out_mla")


def _ffn(x, xn, norms, w_in, w_out, layer, norm_out, keep_raw=True):
    depth, d = norms.shape
    a, w_out_bf = _ffn_up(x if xn is None else xn, norms.reshape(depth, 1, d), w_in, w_out, layer)
    return _mm([a], [_W(w_out_bf)], d, out_dtype=F32, tm=512, tn=d, res=x, norm_out=norm_out, keep_raw=keep_raw,
               name="ffn_down")


def kernel(x, ffn1_norm, ffn1_w_in, ffn1_w_out, mix_norm, ffn2_norm, ffn2_w_in, ffn2_w_out, hyb_w_in, hyb_w_out, nsa_cmp_pe, nsa_cmp_w1, nsa_cmp_w2, fox_f_bias, mla_w_in, mla_q_norm, mla_kv_norm, mla_w_uq, mla_w_ukv, mla_w_out, final_norm):
    batch, seq, d = x.shape
    depth = ffn1_norm.shape[0]
    x = x.reshape(batch * seq, d)
    xn = None
    for i in range(depth):
        x, xn = _ffn(x, xn, ffn1_norm, ffn1_w_in, ffn1_w_out, i, (mix_norm[i], BF))
        if i % 2 == 0:
            e = i // 2
            x, xn = _hybrid_mixer(x, xn, ffn2_norm[i], hyb_w_in, hyb_w_out, e, nsa_cmp_pe[e], nsa_cmp_w1[e],
                                  nsa_cmp_w2[e], fox_f_bias[e], batch=batch, seq=seq)
        else:
            o = i // 2
            x, xn = _mla_mixer(x, xn, ffn2_norm[i], mla_w_in[o], mla_q_norm[o], mla_kv_norm[o], mla_w_uq[o],
                               mla_w_ukv[o], mla_w_out, o, batch=batch, seq=seq)
        if i + 1 < depth:
            x, xn = _ffn(x, xn, ffn2_norm, ffn2_w_in, ffn2_w_out, i, (ffn1_norm[i + 1], BF))
        else:
            out = _ffn(x, xn, ffn2_norm, ffn2_w_in, ffn2_w_out, i, (final_norm, F32), keep_raw=False)
    return out.reshape(batch, seq, d)
```

```python
import functools

import numpy as np
import jax
import jax.numpy as jnp
from jax import lax
from jax.experimental import pallas as pl
from jax.experimental.pallas import tpu as pltpu

D_MODEL = 2048
HEAD_DIM = 128
ROPE_THETA = 10000.0
NORM_EPS = 1e-6
NEG = -1e30

NSA_HEADS = 8
NSA_GROUPS = 2
HEADS_PER_GROUP = NSA_HEADS // NSA_GROUPS
CMP_LEN = 32
CMP_STRIDE = 16
CMP_HIDDEN = 256
SLC_LEN = 64
SLC_TOP = 8
WIN = 512
FOX_HEADS = 8
MLA_HEADS = 16
MLA_Q_RANK = 512
MLA_KV_RANK = 512
MLA_NOPE = 128
MLA_ROPE = 64
MLA_V = 128
D_FF = 5632

LANES = 128
VMEM_LIMIT = 56 * 1024 * 1024

BF = jnp.bfloat16
F32 = jnp.float32


def _params(sem, vmem=VMEM_LIMIT):
    return pltpu.CompilerParams(dimension_semantics=sem, vmem_limit_bytes=vmem)


def _rope_lanes(x, cos, sin):
    return x * cos + pltpu.roll(x, LANES // 2, axis=1) * sin


def _dot_t(a, b):
    return lax.dot_general(a, b, (((1,), (1,)), ((), ())), preferred_element_type=F32)


def _softmax_rows(s):
    m = jnp.max(s, axis=-1, keepdims=True)
    e = jnp.exp(s - m)
    return e, jnp.sum(e, axis=-1, keepdims=True)


def _rms_normed(x, g):
    ms = jnp.mean(x * x, axis=-1, keepdims=True)
    return x * lax.rsqrt(ms + NORM_EPS) * g


def _ffn_up_kernel(x_ref, g_ref, wa_ref, wb_ref, w2_ref, o_ref, w2b_ref, *scratch, n_cast):
    i, j = pl.program_id(0), pl.program_id(1)

    if scratch:
        xn_ref, = scratch

        @pl.when(j == 0)
        def _():
            xn_ref[...] = _rms_normed(x_ref[...], g_ref[...]).astype(BF)
    else:
        xn_ref = x_ref

    @pl.when(i * pl.num_programs(1) + j < n_cast)
    def _():
        w2b_ref[...] = w2_ref[...].astype(BF)

    xn = xn_ref[...]
    half = o_ref.shape[1] // 2
    cols = [slice(c * half, (c + 1) * half) for c in range(2)]
    hs = [(jnp.dot(xn, wa_ref[:, cs].astype(BF), preferred_element_type=F32),
           jnp.dot(xn, wb_ref[:, cs].astype(BF), preferred_element_type=F32)) for cs in cols]
    for cs, (h1, h2) in zip(cols, hs):
        o_ref[:, cs] = (h1 * jax.nn.sigmoid(h1) * h2 * 0.5).astype(BF)


def _ffn_up(x, g, w_in, w_out, layer, *, tm=1024, tf=512, cast_rows=256):
    n, d = x.shape
    n_f = D_FF // tf
    n_cast = D_FF // cast_rows
    assert n_cast <= (n // tm) * n_f
    chunk = lambda i, j: jnp.minimum(i * n_f + j, n_cast - 1)
    return pl.pallas_call(
        functools.partial(_ffn_up_kernel, n_cast=n_cast),
        grid=(n // tm, n_f),
        in_specs=[
            pl.BlockSpec((tm, d), lambda i, j: (i, 0)),
            pl.BlockSpec((None, 1, d), lambda i, j: (layer, 0, 0)),
            pl.BlockSpec((None, d, tf), lambda i, j: (layer, 0, j)),
            pl.BlockSpec((None, d, tf), lambda i, j: (layer, 0, n_f + j)),
            pl.BlockSpec((None, cast_rows, d), lambda i, j: (layer, chunk(i, j), 0)),
        ],
        out_specs=[pl.BlockSpec((tm, tf), lambda i, j: (i, j)),
                   pl.BlockSpec((cast_rows, d), lambda i, j: (chunk(i, j), 0))],
        out_shape=[jax.ShapeDtypeStruct((n, D_FF), BF), jax.ShapeDtypeStruct((D_FF, d), BF)],
        scratch_shapes=[pltpu.VMEM((tm, d), BF)] if x.dtype == F32 else [],
        compiler_params=_params(("arbitrary", "arbitrary")),
        name="ffn_up",
    )(x, g, w_in, w_in, w_out)


class _W:
    def __init__(self, arr, *, layer=None, rows=None, row_blk=0, col_blk0=0, row_starts=None):
        self.arr, self.layer, self.row_blk, self.col_blk0 = arr, layer, row_blk, col_blk0
        self.transposed = row_starts is not None
        self.row_starts = row_starts
        assert row_starts is None or all(r % 8 == 0 for r in row_starts)
        self.rows = rows if rows is not None else arr.shape[-2]

    def spec(self, tn, single_buffer=False):
        layer, row_blk, col_blk0 = self.layer, self.row_blk, self.col_blk0
        if self.transposed:
            starts = self.row_starts

            def start(j):
                off = starts[-1]
                for k in range(len(starts) - 2, -1, -1):
                    off = jnp.where(j == k, starts[k], off)
                return pl.multiple_of(off, 8)

            return pl.BlockSpec((pl.Element(tn), pl.Element(self.arr.shape[1])), lambda i, j: (start(j), 0))
        kw = dict(pipeline_mode=pl.Buffered(1)) if single_buffer else {}
        if self.arr.ndim == 3:
            return pl.BlockSpec((None, self.rows, tn), lambda i, j: (layer, row_blk, col_blk0 + j), **kw)
        return pl.BlockSpec((self.rows, tn), lambda i, j: (row_blk, col_blk0 + j), **kw)


def _mm_kernel(*refs, n_a, has_res, has_norm_out, keep_raw, rope_j, rope_chunks):
    a_refs = refs[:n_a]
    w_refs = refs[n_a:2 * n_a]
    k = 2 * n_a
    if has_res:
        res_ref = refs[k]
        k += 1
    if has_norm_out:
        g_ref = refs[k]
        k += 1
    if rope_chunks is not None:
        cos_ref, sin_ref = refs[k], refs[k + 1]
        k += 2
    out_refs = refs[k:]

    acc = jnp.dot(a_refs[0][...], w_refs[0][...].astype(BF), preferred_element_type=F32)
    for a_ref, w_ref in zip(a_refs[1:], w_refs[1:]):
        acc += jnp.dot(a_ref[...], w_ref[...].astype(BF), preferred_element_type=F32)
    if has_res:
        acc += res_ref[...]

    if has_norm_out:
        out_refs[-1][...] = _rms_normed(acc, g_ref[...]).astype(out_refs[-1].dtype)
    if not keep_raw:
        return
    o_ref = out_refs[0]
    if rope_chunks is None:
        o_ref[...] = acc.astype(o_ref.dtype)
    else:
        j = pl.program_id(1)
        is_rope = (j >= rope_j[0]) & (j < rope_j[1])
        cos, sin = cos_ref[...], sin_ref[...]
        for c, on in enumerate(rope_chunks):
            blk = acc[:, c * LANES:(c + 1) * LANES]
            if on:
                blk = jnp.where(is_rope, _rope_lanes(blk, cos, sin), blk)
            o_ref[:, c * LANES:(c + 1) * LANES] = blk.astype(o_ref.dtype)


def _mm(a_list, w_list, n_out, *, out_dtype, tm, tn, res=None, norm_out=None, keep_raw=True, rope=None, seq=None,
        name="mm"):
    n = a_list[0].shape[0]
    n_a = len(a_list)
    in_specs, args = [], []
    for a in a_list:
        in_specs.append(pl.BlockSpec((tm, a.shape[1]), lambda i, j: (i, 0)))
        args.append(a)
    for w in w_list:
        assert not w.transposed
        in_specs.append(w.spec(tn, single_buffer=tn == n_out))
        args.append(w.arr)
    if res is not None:
        in_specs.append(pl.BlockSpec((tm, tn), lambda i, j: (i, j)))
        args.append(res)
    out_specs, out_shape = [], []
    if keep_raw:
        out_specs.append(pl.BlockSpec((tm, tn), lambda i, j: (i, j)))
        out_shape.append(jax.ShapeDtypeStruct((n, n_out), out_dtype))
    if norm_out is not None:
        assert tn == n_out
        g, norm_dtype = norm_out
        in_specs.append(pl.BlockSpec((1, n_out), lambda i, j: (0, 0)))
        args.append(g.reshape(1, n_out))
        out_specs.append(pl.BlockSpec((tm, tn), lambda i, j: (i, j)))
        out_shape.append(jax.ShapeDtypeStruct((n, n_out), norm_dtype))
    rope_j = rope_chunks = None
    if rope is not None:
        cos, sin, rope_j, rope_chunks = rope
        n_seq = seq // tm
        for t in (cos, sin):
            in_specs.append(pl.BlockSpec((tm, LANES), lambda i, j: (i % n_seq, 0)))
            args.append(t)
    outs = pl.pallas_call(
        functools.partial(_mm_kernel, n_a=n_a, has_res=res is not None, has_norm_out=norm_out is not None,
                          keep_raw=keep_raw, rope_j=rope_j, rope_chunks=rope_chunks),
        grid=(n // tm, n_out // tn),
        in_specs=in_specs,
        out_specs=out_specs,
        out_shape=out_shape,
        compiler_params=_params(("parallel", "arbitrary")),
        name=name,
    )(*args)
    return outs if len(outs) > 1 else outs[0]


def _hybrid_proj_kernel(xn_ref, w_ref, *rest, n_main, n_side, rope_j, rope_chunks):
    side_refs = rest[:n_side]
    cos_ref, sin_ref, o_ref, p_ref = rest[n_side:]
    j = pl.program_id(1)

    @pl.when(j < n_main)
    def _():
        acc = _dot_t(xn_ref[...], w_ref[...].astype(BF))
        is_rope = (j >= rope_j[0]) & (j < rope_j[1])
        cos, sin = cos_ref[...], sin_ref[...]
        for c, on in enumerate(rope_chunks):
            blk = acc[:, c * LANES:(c + 1) * LANES]
            if on:
                blk = jnp.where(is_rope, _rope_lanes(blk, cos, sin), blk)
            o_ref[:, c * LANES:(c + 1) * LANES] = blk.astype(o_ref.dtype)

    @pl.when(j == n_main)
    def _():
        w_side = jnp.concatenate([w[...] for w in side_refs], axis=0).astype(BF)
        s = _dot_t(xn_ref[...], w_side)
        for p in range(s.shape[1] // LANES):
            p_ref[p] = s[:, p * LANES:(p + 1) * LANES]


def _hybrid_proj(xn, wt, main_starts, side_windows, rope, *, seq, tm=1024, tn=512):
    n, d = xn.shape
    cos, sin, rope_j, rope_chunks = rope
    n_main = len(main_starts)
    n_planes = sum(rows for _, rows in side_windows) // LANES
    n_seq = seq // tm
    last = n_main - 1
    side_specs = [pl.BlockSpec((pl.Element(rows), pl.Element(d)), lambda i, j, start=start: (start, 0),
                               pipeline_mode=pl.Buffered(1)) for start, rows in side_windows]
    return pl.pallas_call(
        functools.partial(_hybrid_proj_kernel, n_main=n_main, n_side=len(side_windows), rope_j=rope_j,
                          rope_chunks=rope_chunks),
        grid=(n // tm, n_main + 1),
        in_specs=[
            pl.BlockSpec((tm, d), lambda i, j: (i, 0)),
            _W(wt, row_starts=main_starts).spec(tn),
            *side_specs,
            pl.BlockSpec((tm, LANES), lambda i, j: (i % n_seq, 0)),
            pl.BlockSpec((tm, LANES), lambda i, j: (i % n_seq, 0)),
        ],
        out_specs=[pl.BlockSpec((tm, tn), lambda i, j: (i, jnp.minimum(j, last))),
                   pl.BlockSpec((n_planes, tm, LANES), lambda i, j: (0, i, 0))],
        out_shape=[jax.ShapeDtypeStruct((n, n_main * tn), BF), jax.ShapeDtypeStruct((n_planes, n, LANES), F32)],
        compiler_params=_params(("parallel", "arbitrary")),
        name="proj_hybrid",
    )(xn, wt, *([wt] * len(side_windows)), cos, sin)


def _compress_kernel(x_ref, pe_ref, w1_ref, w2_ref, o_ref):
    n_blk = o_ref.shape[2]
    ha = jnp.zeros((n_blk, CMP_HIDDEN), F32)
    hb = jnp.zeros((n_blk, CMP_HIDDEN), F32)
    for l in range(CMP_STRIDE):
        xl = x_ref[0, pl.ds(l, n_blk, stride=CMP_STRIDE), :]
        wa = w1_ref[0, l * HEAD_DIM:(l + 1) * HEAD_DIM, :].astype(BF)
        wb = w1_ref[0, (CMP_STRIDE + l) * HEAD_DIM:(CMP_STRIDE + l + 1) * HEAD_DIM, :].astype(BF)
        ha += jnp.dot((xl + pe_ref[0, l:l + 1, :]).astype(BF), wa, preferred_element_type=F32)
        hb += jnp.dot((xl + pe_ref[0, CMP_STRIDE + l:CMP_STRIDE + l + 1, :]).astype(BF), wb,
                      preferred_element_type=F32)
    hid = ha + pltpu.roll(hb, n_blk - 1, axis=0)
    act = jax.nn.gelu(hid, approximate=True).astype(BF)
    o_ref[0, 0] = jnp.dot(act, w2_ref[0].astype(BF), preferred_element_type=F32)


def _compress(planes, pe, w1, w2, *, batch, seq):
    rows = seq // CMP_STRIDE
    return pl.pallas_call(
        _compress_kernel,
        grid=(2 * NSA_GROUPS, batch),
        in_specs=[
            pl.BlockSpec((1, seq, HEAD_DIM), lambda c, b: (c, b, 0)),
            pl.BlockSpec((1, CMP_LEN, HEAD_DIM), lambda c, b: (c // NSA_GROUPS, 0, 0)),
            pl.BlockSpec((1, CMP_LEN * HEAD_DIM, CMP_HIDDEN), lambda c, b: (c // NSA_GROUPS, 0, 0)),
            pl.BlockSpec((1, CMP_HIDDEN, HEAD_DIM), lambda c, b: (c // NSA_GROUPS, 0, 0)),
        ],
        out_specs=pl.BlockSpec((1, 1, rows, HEAD_DIM), lambda c, b: (c, b, 0, 0)),
        out_shape=jax.ShapeDtypeStruct((2 * NSA_GROUPS, batch, rows, HEAD_DIM), F32),
        compiler_params=_params(("arbitrary", "arbitrary")),
        name="nsa_compress",
    )(planes, pe, w1, w2)


def _stacked_softmax_pv(s_raw, bias, v, scale, n_stack, tq):
    exp2_scale = scale * float(np.log2(np.e))
    es = []
    for h in range(n_stack):
        u = s_raw[h * tq:(h + 1) * tq] + bias
        es.append(jnp.exp2((u - jnp.max(u, axis=-1, keepdims=True)) * exp2_scale).astype(BF))
    v_ones = jnp.concatenate([v, jnp.ones_like(v)], axis=1)
    o = jnp.dot(jnp.concatenate(es, axis=0), v_ones, preferred_element_type=F32)
    d = v.shape[1]
    return o[:, :d] / o[:, d:d + 1]


def _nsa_kernel(q_ref, ks_ref, kw_ref, vs_ref, vw_ref, kvc_ref, gate_ref, cos_ref, sin_ref, m_ref, o_ref,
                *, tq, seq, key_step):
    qi = pl.program_id(1)
    hg = HEADS_PER_GROUP
    scale = HEAD_DIM ** -0.5
    n_cmp = (seq - CMP_LEN) // CMP_STRIDE + 1
    n_slc = seq // SLC_LEN
    band = WIN + tq

    t_row = qi * tq + lax.broadcasted_iota(jnp.int32, (tq, 1), 0)
    cos = jnp.concatenate([cos_ref[...]] * hg, axis=0)
    sin = jnp.concatenate([sin_ref[...]] * hg, axis=0)
    gates = jax.nn.sigmoid(gate_ref[0])

    n_lane = lax.broadcasted_iota(jnp.int32, (1, LANES), 1)
    cmask = (n_lane * CMP_STRIDE + (CMP_LEN - 1) <= t_row) & (n_lane < n_cmp)
    cbias = jnp.where(cmask, 0.0, NEG)
    j_blk = lax.broadcasted_iota(jnp.int32, (n_slc, 1), 0)
    slc_shift = SLC_LEN.bit_length() - 1
    cur = jnp.right_shift(qi * tq + lax.broadcasted_iota(jnp.int32, (1, tq), 1), slc_shift)
    forced = (j_blk == 0) | (j_blk == cur) | (j_blk == cur - 1)
    w_start = pl.multiple_of(jnp.maximum(qi * tq - WIN, 0), tq)
    w_pos = w_start + lax.broadcasted_iota(jnp.int32, (1, band), 1)
    wbias = jnp.where((w_pos <= t_row) & (w_pos > t_row - WIN), 0.0, NEG)

    def group_queries(g):
        qg = q_ref[0, :, g * hg * HEAD_DIM:(g + 1) * hg * HEAD_DIM].astype(F32)
        qp = jnp.concatenate([qg[:, h * HEAD_DIM:(h + 1) * HEAD_DIM] for h in range(hg)], axis=0)
        return qp.astype(BF), _rope_lanes(qp, cos, sin).astype(BF)

    def compressed_branch(g, qp):
        kc = kvc_ref[g, 0].astype(BF)
        vc = kvc_ref[NSA_GROUPS + g, 0].astype(BF)
        s = _dot_t(qp, kc)
        ps = []
        for h in range(hg):
            u = s[h * tq:(h + 1) * tq] * scale + cbias
            e, l = _softmax_rows(u)
            ps.append(jnp.where(cmask, e / l, 0.0))
        o_cmp = jnp.dot(jnp.concatenate(ps, axis=0).astype(BF), vc, preferred_element_type=F32)

        p_sum = ps[0]
        for h in range(1, hg):
            p_sum = p_sum + ps[h]
        imp = lax.dot_general(m_ref[...], p_sum, (((1,), (1,)), ((), ())), preferred_element_type=F32,
                              precision=lax.Precision.HIGHEST)
        imp = jnp.where(j_blk > cur, -jnp.inf, jnp.where(forced, jnp.inf, imp))
        rank = jnp.zeros((n_slc, tq), jnp.int32)
        for i in range(n_slc):
            ci = imp[i:i + 1, :]
            beats = (ci > imp) | ((ci == imp) & (j_blk > i))
            rank = rank + beats.astype(jnp.int32)
        return o_cmp, jnp.where(rank < SLC_TOP, 0.0, NEG).astype(BF)

    qs = [group_queries(g) for g in range(NSA_GROUPS)]
    cmps = [compressed_branch(g, qs[g][0]) for g in range(NSA_GROUPS)]

    def attend(width):
        blocks = width // SLC_LEN
        expand = jnp.where(jnp.right_shift(lax.broadcasted_iota(jnp.int32, (blocks, width), 1), slc_shift)
                           == lax.broadcasted_iota(jnp.int32, (blocks, width), 0), 1.0, 0.0).astype(BF)
        s_pos = lax.broadcasted_iota(jnp.int32, (1, width), 1)
        lanes = [slice(g * HEAD_DIM, (g + 1) * HEAD_DIM) for g in range(NSA_GROUPS)]
        s_win = [_dot_t(qs[g][1], kw_ref[0, pl.ds(w_start, band), lanes[g]]) for g in range(NSA_GROUPS)]
        s_slc = [_dot_t(qs[g][1], ks_ref[0, :width, lanes[g]]) for g in range(NSA_GROUPS)]
        for g in range(NSA_GROUPS):
            o_win = _stacked_softmax_pv(s_win[g], wbias, vw_ref[0, pl.ds(w_start, band), lanes[g]], scale, hg, tq)
            o_cmp, sel_neg = cmps[g]
            bias = lax.dot_general(sel_neg[:blocks], expand, (((0,), (0,)), ((), ())),
                                   preferred_element_type=F32)
            bias = jnp.where(s_pos <= t_row, bias, NEG)
            o_slc = _stacked_softmax_pv(s_slc[g], bias, vs_ref[0, :width, lanes[g]], scale, hg, tq)
            for hh in range(hg):
                h = g * hg + hh
                rows = slice(hh * tq, (hh + 1) * tq)
                o = (o_cmp[rows] * gates[:, 3 * h:3 * h + 1]
                     + o_slc[rows] * gates[:, 3 * h + 1:3 * h + 2]
                     + o_win[rows] * gates[:, 3 * h + 2:3 * h + 3])
                o_ref[0, :, h * HEAD_DIM:(h + 1) * HEAD_DIM] = o.astype(o_ref.dtype)

    n_widths = seq // key_step
    super_row = qi // (key_step // tq)
    for k in range(n_widths):
        pl.when(super_row == k)(functools.partial(attend, (k + 1) * key_step))


def _nsa(main, kvc, planes, gate_plane, cos, sin, m_cs, *, batch, seq, tq=128, key_step=512):
    nq = seq // tq
    kv_w = NSA_GROUPS * HEAD_DIM
    kv0 = NSA_HEADS * HEAD_DIM // kv_w
    return pl.pallas_call(
        functools.partial(_nsa_kernel, tq=tq, seq=seq, key_step=key_step),
        grid=(batch, nq),
        in_specs=[
            pl.BlockSpec((1, tq, NSA_HEADS * HEAD_DIM), lambda b, i: (b, i, 0)),
            pl.BlockSpec((1, seq, kv_w), lambda b, i: (b, 0, kv0)),
            pl.BlockSpec((1, seq, kv_w), lambda b, i: (b, 0, kv0 + 2)),
            pl.BlockSpec((1, seq, kv_w), lambda b, i: (b, 0, kv0 + 1)),
            pl.BlockSpec((1, seq, kv_w), lambda b, i: (b, 0, kv0 + 3)),
            pl.BlockSpec((2 * NSA_GROUPS, 1, LANES, HEAD_DIM), lambda b, i: (0, b, 0, 0)),
            pl.BlockSpec((1, tq, LANES), lambda b, i: (gate_plane, b * nq + i, 0)),
            pl.BlockSpec((tq, LANES), lambda b, i: (i, 0)),
            pl.BlockSpec((tq, LANES), lambda b, i: (i, 0)),
            pl.BlockSpec(m_cs.shape, lambda b, i: (0, 0)),
        ],
        out_specs=pl.BlockSpec((1, tq, NSA_HEADS * HEAD_DIM), lambda b, i: (b, i, 0)),
        out_shape=jax.ShapeDtypeStruct((batch, seq, NSA_HEADS * HEAD_DIM), BF),
        compiler_params=_params(("parallel", "arbitrary")),
        name="nsa_attention",
    )(main, main, main, main, main, kvc, planes, cos, sin, m_cs)


def _cumdecay_kernel(x_ref, b_ref, o_ref):
    z = x_ref[0] + b_ref[...]
    c = jnp.minimum(z, 0.0) - jnp.log1p(jnp.exp(-jnp.abs(z)))
    rows = c.shape[0]
    r = lax.broadcasted_iota(jnp.int32, (rows, 1), 0)
    k = 1
    while k < rows:
        c = c + jnp.where(r >= k, pltpu.roll(c, k, axis=0), 0.0)
        k *= 2
    o_ref[0] = c


def _cumdecay(planes, plane, bias, *, batch, seq):
    return pl.pallas_call(
        _cumdecay_kernel,
        grid=(batch,),
        in_specs=[pl.BlockSpec((1, seq, LANES), lambda b: (plane, b, 0)),
                  pl.BlockSpec((1, LANES), lambda b: (0, 0))],
        out_specs=pl.BlockSpec((1, seq, LANES), lambda b: (b, 0, 0)),
        out_shape=jax.ShapeDtypeStruct((batch, seq, LANES), F32),
        compiler_params=_params(("arbitrary",)),
        name="fox_cumdecay",
    )(planes, bias)


def _attn_kernel(*refs, tq, seq, scale, n_qk, decay):
    q_ref = refs[0]
    k_refs = refs[1:1 + n_qk]
    v_ref = refs[1 + n_qk]
    k = 2 + n_qk
    if decay:
        ck_ref = refs[k]
        k += 1
    o_ref = refs[k]

    row = lax.broadcasted_iota(jnp.int32, (tq, 1), 0)
    col = lax.broadcasted_iota(jnp.int32, (1, tq), 1)
    tri = col <= row
    exp2_scale = scale * float(np.log2(np.e))

    k_all = k_refs[0][0] if n_qk == 1 else jnp.concatenate([k_ref[0] for k_ref in k_refs], axis=1)

    v_ones = jnp.concatenate([v_ref[0], jnp.ones((seq, HEAD_DIM), BF)], axis=1)

    n_tiles = seq // tq
    scores = lambda i: _dot_t(q_ref[0, i * tq:(i + 1) * tq, :], k_all[:(i + 1) * tq])
    s_next = scores(n_tiles - 1)
    for i in reversed(range(n_tiles)):
        lo, hi = i * tq, (i + 1) * tq
        s = s_next
        if i > 0:
            s_next = scores(i - 1)
        if decay:
            ck = ck_ref[0, :, :hi]
            cq = jnp.sum(jnp.where(col == row, ck[:, lo:hi], 0.0), axis=1, keepdims=True)
            s = s * scale + (cq - ck)
            prob = lambda u, m: jnp.exp(u - m)
        else:
            prob = lambda u, m: jnp.exp2((u - m) * exp2_scale)
        s_diag = jnp.where(tri, s[:, lo:hi], NEG)
        m = jnp.max(s_diag, axis=-1, keepdims=True)
        if i > 0:
            m = jnp.maximum(m, jnp.max(s[:, :lo], axis=-1, keepdims=True))
        e = prob(s_diag, m).astype(BF)
        if i > 0:
            e = jnp.concatenate([prob(s[:, :lo], m).astype(BF), e], axis=1)
        o = jnp.dot(e, v_ones[:hi], preferred_element_type=F32)
        o_ref[0, lo:hi, :] = (o[:, :HEAD_DIM] / o[:, HEAD_DIM:HEAD_DIM + 1]).astype(o_ref.dtype)


def _attention(q_arr, q_blk0, k_arrs, k_blk0s, k_per_head, v_arr, v_blk0, *, batch, seq, heads, scale,
               decay=None, tq=256, name="attention"):
    n_qk = len(k_arrs)
    in_specs = [pl.BlockSpec((1, seq, n_qk * HEAD_DIM), lambda b, h: (b, 0, q_blk0 + h))]
    args = [q_arr]
    for arr, blk0, per_head in zip(k_arrs, k_blk0s, k_per_head):
        if per_head:
            in_specs.append(pl.BlockSpec((1, seq, HEAD_DIM), lambda b, h, blk0=blk0: (b, 0, blk0 + h)))
        else:
            in_specs.append(pl.BlockSpec((1, seq, HEAD_DIM), lambda b, h, blk0=blk0: (b, 0, blk0)))
        args.append(arr)
    in_specs.append(pl.BlockSpec((1, seq, HEAD_DIM), lambda b, h: (b, 0, v_blk0 + h)))
    args.append(v_arr)
    if decay is not None:
        in_specs.append(pl.BlockSpec((1, 1, seq), lambda b, h: (b * heads + h, 0, 0)))
        args.append(decay)
    return pl.pallas_call(
        functools.partial(_attn_kernel, tq=tq, seq=seq, scale=scale, n_qk=n_qk, decay=decay is not None),
        grid=(batch, heads),
        in_specs=in_specs,
        out_specs=pl.BlockSpec((1, seq, HEAD_DIM), lambda b, h: (b, 0, h)),
        out_shape=jax.ShapeDtypeStruct((batch, seq, heads * HEAD_DIM), BF),
        compiler_params=_params(("parallel", "arbitrary")),
        name=name,
    )(*args)


def _mla_latent_kernel(xn_ref, wt_ref, wkr_ref, gq_ref, gkv_ref, cos_ref, sin_ref, cq_ref, ckv_ref, kr_ref):
    h = xn_ref[...]
    c = _dot_t(h, wt_ref[...].astype(BF))
    cq_ref[...] = _rms_normed(c[:, :MLA_Q_RANK], gq_ref[...]).astype(BF)
    ckv_ref[...] = _rms_normed(c[:, MLA_Q_RANK:], gkv_ref[...]).astype(BF)
    kr_ref[...] = _rope_lanes(_dot_t(h, wkr_ref[...].astype(BF)), cos_ref[...], sin_ref[...]).astype(BF)


def _mla_latent(xn, wt, w_kr, gq, gkv, cos, sin, *, seq, tm=512):
    n = xn.shape[0]
    n_seq = seq // tm
    return pl.pallas_call(
        _mla_latent_kernel,
        grid=(n // tm,),
        in_specs=[
            pl.BlockSpec((tm, xn.shape[1]), lambda i: (i, 0)),
            pl.BlockSpec((MLA_Q_RANK + MLA_KV_RANK, wt.shape[1]), lambda i: (0, 0)),
            pl.BlockSpec(w_kr.shape, lambda i: (0, 0)),
            pl.BlockSpec((1, MLA_Q_RANK), lambda i: (0, 0)),
            pl.BlockSpec((1, MLA_KV_RANK), lambda i: (0, 0)),
            pl.BlockSpec((tm, LANES), lambda i: (i % n_seq, 0)),
            pl.BlockSpec((tm, LANES), lambda i: (i % n_seq, 0)),
        ],
        out_specs=[
            pl.BlockSpec((tm, MLA_Q_RANK), lambda i: (i, 0)),
            pl.BlockSpec((tm, MLA_KV_RANK), lambda i: (i, 0)),
            pl.BlockSpec((tm, LANES), lambda i: (i, 0)),
        ],
        out_shape=[
            jax.ShapeDtypeStruct((n, MLA_Q_RANK), BF),
            jax.ShapeDtypeStruct((n, MLA_KV_RANK), BF),
            jax.ShapeDtypeStruct((n, LANES), BF),
        ],
        compiler_params=_params(("parallel",)),
        name="mla_latent",
    )(xn, wt, w_kr, gq.reshape(1, -1), gkv.reshape(1, -1), cos, sin)


def _rope_tables(seq, dim):
    half = dim // 2
    inv = 1.0 / (ROPE_THETA ** (np.arange(0, dim, 2, dtype=np.float32) / dim))
    ang = np.arange(seq, dtype=np.float32)[:, None] * inv.astype(np.float32)[None, :]
    cos = np.zeros((seq, LANES), np.float32)
    sin = np.zeros((seq, LANES), np.float32)
    cos[:, :half] = np.cos(ang)
    cos[:, LANES // 2:LANES // 2 + half] = np.cos(ang)
    sin[:, :half] = -np.sin(ang)
    sin[:, LANES // 2:LANES // 2 + half] = np.sin(ang)
    return jnp.asarray(cos), jnp.asarray(sin)


def _cmp_to_slc(seq):
    n_cmp = (seq - CMP_LEN) // CMP_STRIDE + 1
    n_slc = seq // SLC_LEN
    c0 = np.arange(n_cmp) * CMP_STRIDE
    c1 = c0 + CMP_LEN
    s0 = np.arange(n_slc) * SLC_LEN
    s1 = s0 + SLC_LEN
    ov = np.clip(np.minimum(c1[:, None], s1[None, :]) - np.maximum(c0[:, None], s0[None, :]), 0, None)
    m = np.zeros((n_slc, LANES), np.float32)
    m[:, :n_cmp] = (ov / CMP_LEN).T
    return jnp.asarray(m)


def _spread_rope_cols(w):
    half = MLA_ROPE // 2
    z = jnp.zeros(w.shape[:-1] + (LANES // 2 - half,), w.dtype)
    return jnp.concatenate([w[..., :half], z, w[..., half:], z], axis=-1)


def _hybrid_mixer(x, xn, g_next, w_in, w_out, layer, cmp_pe, cmp_w1, cmp_w2, f_bias, *, batch, seq):
    n = batch * seq
    nq_w = NSA_HEADS * HEAD_DIM
    kv_w = NSA_GROUPS * HEAD_DIM
    fx_w = FOX_HEADS * HEAD_DIM
    c_cmp, c_slc, c_gate = nq_w, nq_w + 2 * kv_w, nq_w + 6 * kv_w
    c_fox = c_gate + 3 * NSA_HEADS
    c_f = c_fox + 3 * fx_w
    wt = jnp.swapaxes(w_in[layer], 0, 1)

    cos, sin = _rope_tables(seq, HEAD_DIM)
    tn = 2 * kv_w
    main_starts = tuple(range(0, c_cmp, tn)) + tuple(range(c_slc, c_gate, tn)) + tuple(range(c_fox, c_f, tn))
    f_lane0 = LANES - FOX_HEADS
    side_windows = ((c_cmp, c_slc - c_cmp), (c_gate, LANES), (c_f - f_lane0, LANES))
    gate_plane, f_plane = (c_slc - c_cmp) // LANES, (c_slc - c_cmp) // LANES + 1
    main, planes = _hybrid_proj(xn, wt, main_starts, side_windows,
                                (cos, sin, (nq_w // tn, nq_w // tn + 2), (True, True, False, False)), seq=seq, tn=tn)

    kvc = _compress(planes, cmp_pe, cmp_w1, cmp_w2, batch=batch, seq=seq)

    main3 = main.reshape(batch, seq, -1)
    o_nsa = _nsa(main3, kvc, planes, gate_plane, cos, sin, _cmp_to_slc(seq), batch=batch, seq=seq)

    bias = jnp.zeros((1, LANES), F32).at[0, f_lane0:].set(f_bias)
    cum = _cumdecay(planes, f_plane, bias, batch=batch, seq=seq)[:, :, f_lane0:]
    cum_t = jnp.swapaxes(cum, 1, 2).reshape(batch * FOX_HEADS, seq)
    blk = lambda col: col // HEAD_DIM
    c_qf = nq_w + 4 * kv_w
    o_fox = _attention(main3, blk(c_qf), [main3], [blk(c_qf + fx_w)], [True], main3, blk(c_qf + 2 * fx_w),
                       batch=batch, seq=seq, heads=FOX_HEADS, scale=HEAD_DIM ** -0.5,
                       decay=cum_t[:, None, :], name="fox_attention")

    return _mm([o_nsa.reshape(n, nq_w), o_fox.reshape(n, fx_w)],
               [_W(w_out, layer=layer, rows=nq_w, row_blk=0), _W(w_out, layer=layer, rows=fx_w, row_blk=1)],
               D_MODEL, out_dtype=F32, tm=512, tn=D_MODEL, res=x, norm_out=(g_next, BF), name="proj_out_hybrid")


def _mla_mixer(x, xn, g_next, w_in, q_norm, kv_norm, w_uq, w_ukv, w_out, layer, *, batch, seq):
    n = batch * seq
    qk = MLA_NOPE + MLA_ROPE
    wt_lat = jnp.swapaxes(w_in, 0, 1)
    w_kr = jnp.swapaxes(_spread_rope_cols(w_in[:, MLA_Q_RANK + MLA_KV_RANK:]), 0, 1)
    wq = w_uq.reshape(MLA_Q_RANK, MLA_HEADS, qk)
    wq = jnp.concatenate([wq[..., :MLA_NOPE], _spread_rope_cols(wq[..., MLA_NOPE:])], axis=-1)
    wq = wq.reshape(MLA_Q_RANK, MLA_HEADS * 2 * HEAD_DIM)
    wkv = w_ukv.reshape(MLA_KV_RANK, MLA_HEADS, MLA_NOPE + MLA_V)
    wkv = jnp.concatenate([wkv[..., :MLA_NOPE].reshape(MLA_KV_RANK, -1),
                           wkv[..., MLA_NOPE:].reshape(MLA_KV_RANK, -1)], axis=1)

    cos, sin = _rope_tables(seq, MLA_ROPE)
    cq, ckv, kr = _mla_latent(xn, wt_lat, w_kr, q_norm, kv_norm, cos, sin, seq=seq)
    tn = wq.shape[1]
    q = _mm([cq], [_W(wq)], wq.shape[1], out_dtype=BF, tm=512, tn=tn,
            rope=(cos, sin, (0, 1), (False, True) * (tn // (2 * LANES))), seq=seq, name="proj_mla_q")
    kv = _mm([ckv], [_W(wkv)], wkv.shape[1], out_dtype=BF, tm=512, tn=tn, name="proj_mla_kv")

    q3 = q.reshape(batch, seq, -1)
    kv3 = kv.reshape(batch, seq, -1)
    kr3 = kr.reshape(batch, seq, LANES)
    o = _attention(q3, 0, [kv3, kr3], [0, 0], [True, False], kv3, MLA_HEADS, batch=batch, seq=seq,
                   heads=MLA_HEADS, scale=qk ** -0.5, name="mla_attention")
    return _mm([o.reshape(n, MLA_HEADS * MLA_V)], [_W(w_out, layer=layer)], D_MODEL, out_dtype=F32, tm=512, tn=D_MODEL,
               res=x, norm_out=(g_next, BF), name="proj_out_mla")


def _ffn(x, xn, norms, w_in, w_out, layer, norm_out, keep_raw=True):
    depth, d = norms.shape
    a, w_out_bf = _ffn_up(x if xn is None else xn, norms.reshape(depth, 1, d), w_in, w_out, layer)
    return _mm([a], [_W(w_out_bf)], d, out_dtype=F32, tm=512, tn=d, res=x, norm_out=norm_out, keep_raw=keep_raw,
               name="ffn_down")


def kernel(x, ffn1_norm, ffn1_w_in, ffn1_w_out, mix_norm, ffn2_norm, ffn2_w_in, ffn2_w_out, hyb_w_in, hyb_w_out, nsa_cmp_pe, nsa_cmp_w1, nsa_cmp_w2, fox_f_bias, mla_w_in, mla_q_norm, mla_kv_norm, mla_w_uq, mla_w_ukv, mla_w_out, final_norm):
    batch, seq, d = x.shape
    depth = ffn1_norm.shape[0]
    x = x.reshape(batch * seq, d)
    xn = None
    for i in range(depth):
        x, xn = _ffn(x, xn, ffn1_norm, ffn1_w_in, ffn1_w_out, i, (mix_norm[i], BF))
        if i % 2 == 0:
            e = i // 2
            x, xn = _hybrid_mixer(x, xn, ffn2_norm[i], hyb_w_in, hyb_w_out, e, nsa_cmp_pe[e], nsa_cmp_w1[e],
                                  nsa_cmp_w2[e], fox_f_bias[e], batch=batch, seq=seq)
        else:
            o = i // 2
            x, xn = _mla_mixer(x, xn, ffn2_norm[i], mla_w_in[o], mla_q_norm[o], mla_kv_norm[o], mla_w_uq[o],
                               mla_w_ukv[o], mla_w_out, o, batch=batch, seq=seq)
        if i + 1 < depth:
            x, xn = _ffn(x, xn, ffn2_norm, ffn2_w_in, ffn2_w_out, i, (ffn1_norm[i + 1], BF))
        else:
            out = _ffn(x, xn, ffn2_norm, ffn2_w_in, ffn2_w_out, i, (final_norm, F32), keep_raw=False)
    return out.reshape(batch, seq, d)
```

```python
import functools

import numpy as np
import jax
import jax.numpy as jnp
from jax import lax
from jax.experimental import pallas as pl
from jax.experimental.pallas import tpu as pltpu

D_MODEL = 2048
HEAD_DIM = 128
ROPE_THETA = 10000.0
NORM_EPS = 1e-6
NEG = -1e30

NSA_HEADS = 8
NSA_GROUPS = 2
HEADS_PER_GROUP = NSA_HEADS // NSA_GROUPS
CMP_LEN = 32
CMP_STRIDE = 16
CMP_HIDDEN = 256
SLC_LEN = 64
SLC_TOP = 8
WIN = 512
FOX_HEADS = 8
MLA_HEADS = 16
MLA_Q_RANK = 512
MLA_KV_RANK = 512
MLA_NOPE = 128
MLA_ROPE = 64
MLA_V = 128
D_FF = 5632

LANES = 128
VMEM_LIMIT = 56 * 1024 * 1024

BF = jnp.bfloat16
F32 = jnp.float32


def _params(sem, vmem=VMEM_LIMIT):
    return pltpu.CompilerParams(dimension_semantics=sem, vmem_limit_bytes=vmem)


def _rope_lanes(x, cos, sin):
    return x * cos + pltpu.roll(x, LANES // 2, axis=1) * sin


def _dot_t(a, b):
    return lax.dot_general(a, b, (((1,), (1,)), ((), ())), preferred_element_type=F32)


def _softmax_rows(s):
    m = jnp.max(s, axis=-1, keepdims=True)
    e = jnp.exp(s - m)
    return e, jnp.sum(e, axis=-1, keepdims=True)


def _rms_normed(x, g):
    ms = jnp.mean(x * x, axis=-1, keepdims=True)
    return x * lax.rsqrt(ms + NORM_EPS) * g


def _ffn_up_kernel(x_ref, g_ref, wa_ref, wb_ref, w2_ref, o_ref, w2b_ref, *scratch, n_cast):
    i, j = pl.program_id(0), pl.program_id(1)

    if scratch:
        xn_ref, = scratch

        @pl.when(j == 0)
        def _():
            xn_ref[...] = _rms_normed(x_ref[...], g_ref[...]).astype(BF)
    else:
        xn_ref = x_ref

    @pl.when(i * pl.num_programs(1) + j < n_cast)
    def _():
        w2b_ref[...] = w2_ref[...].astype(BF)

    xn = xn_ref[...]
    half = o_ref.shape[1] // 2
    cols = [slice(c * half, (c + 1) * half) for c in range(2)]
    hs = [(jnp.dot(xn, wa_ref[:, cs].astype(BF), preferred_element_type=F32),
           jnp.dot(xn, wb_ref[:, cs].astype(BF), preferred_element_type=F32)) for cs in cols]
    for cs, (h1, h2) in zip(cols, hs):
        o_ref[:, cs] = (h1 * jax.nn.sigmoid(h1) * h2 * 0.5).astype(BF)


def _ffn_up(x, g, w_in, w_out, layer, *, tm=1024, tf=512, cast_rows=256):
    n, d = x.shape
    n_f = D_FF // tf
    n_cast = D_FF // cast_rows
    assert n_cast <= (n // tm) * n_f
    chunk = lambda i, j: jnp.minimum(i * n_f + j, n_cast - 1)
    return pl.pallas_call(
        functools.partial(_ffn_up_kernel, n_cast=n_cast),
        grid=(n // tm, n_f),
        in_specs=[
            pl.BlockSpec((tm, d), lambda i, j: (i, 0)),
            pl.BlockSpec((None, 1, d), lambda i, j: (layer, 0, 0)),
            pl.BlockSpec((None, d, tf), lambda i, j: (layer, 0, j)),
            pl.BlockSpec((None, d, tf), lambda i, j: (layer, 0, n_f + j)),
            pl.BlockSpec((None, cast_rows, d), lambda i, j: (layer, chunk(i, j), 0)),
        ],
        out_specs=[pl.BlockSpec((tm, tf), lambda i, j: (i, j)),
                   pl.BlockSpec((cast_rows, d), lambda i, j: (chunk(i, j), 0))],
        out_shape=[jax.ShapeDtypeStruct((n, D_FF), BF), jax.ShapeDtypeStruct((D_FF, d), BF)],
        scratch_shapes=[pltpu.VMEM((tm, d), BF)] if x.dtype == F32 else [],
        compiler_params=_params(("arbitrary", "arbitrary")),
        name="ffn_up",
    )(x, g, w_in, w_in, w_out)


class _W:
    def __init__(self, arr, *, layer=None, rows=None, row_blk=0, col_blk0=0, row_starts=None):
        self.arr, self.layer, self.row_blk, self.col_blk0 = arr, layer, row_blk, col_blk0
        self.transposed = row_starts is not None
        self.row_starts = row_starts
        assert row_starts is None or all(r % 8 == 0 for r in row_starts)
        self.rows = rows if rows is not None else arr.shape[-2]

    def spec(self, tn, single_buffer=False):
        layer, row_blk, col_blk0 = self.layer, self.row_blk, self.col_blk0
        if self.transposed:
            starts = self.row_starts

            def start(j):
                off = starts[-1]
                for k in range(len(starts) - 2, -1, -1):
                    off = jnp.where(j == k, starts[k], off)
                return pl.multiple_of(off, 8)

            return pl.BlockSpec((pl.Element(tn), pl.Element(self.arr.shape[1])), lambda i, j: (start(j), 0))
        kw = dict(pipeline_mode=pl.Buffered(1)) if single_buffer else {}
        if self.arr.ndim == 3:
            return pl.BlockSpec((None, self.rows, tn), lambda i, j: (layer, row_blk, col_blk0 + j), **kw)
        return pl.BlockSpec((self.rows, tn), lambda i, j: (row_blk, col_blk0 + j), **kw)


def _mm_kernel(*refs, n_a, has_res, has_norm_out, keep_raw, rope_j, rope_chunks):
    a_refs = refs[:n_a]
    w_refs = refs[n_a:2 * n_a]
    k = 2 * n_a
    if has_res:
        res_ref = refs[k]
        k += 1
    if has_norm_out:
        g_ref = refs[k]
        k += 1
    if rope_chunks is not None:
        cos_ref, sin_ref = refs[k], refs[k + 1]
        k += 2
    out_refs = refs[k:]

    acc = jnp.dot(a_refs[0][...], w_refs[0][...].astype(BF), preferred_element_type=F32)
    for a_ref, w_ref in zip(a_refs[1:], w_refs[1:]):
        acc += jnp.dot(a_ref[...], w_ref[...].astype(BF), preferred_element_type=F32)
    if has_res:
        acc += res_ref[...]

    if has_norm_out:
        out_refs[-1][...] = _rms_normed(acc, g_ref[...]).astype(out_refs[-1].dtype)
    if not keep_raw:
        return
    o_ref = out_refs[0]
    if rope_chunks is None:
        o_ref[...] = acc.astype(o_ref.dtype)
    else:
        j = pl.program_id(1)
        is_rope = (j >= rope_j[0]) & (j < rope_j[1])
        cos, sin = cos_ref[...], sin_ref[...]
        for c, on in enumerate(rope_chunks):
            blk = acc[:, c * LANES:(c + 1) * LANES]
            if on:
                blk = jnp.where(is_rope, _rope_lanes(blk, cos, sin), blk)
            o_ref[:, c * LANES:(c + 1) * LANES] = blk.astype(o_ref.dtype)


def _mm(a_list, w_list, n_out, *, out_dtype, tm, tn, res=None, norm_out=None, keep_raw=True, rope=None, seq=None,
        name="mm"):
    n = a_list[0].shape[0]
    n_a = len(a_list)
    in_specs, args = [], []
    for a in a_list:
        in_specs.append(pl.BlockSpec((tm, a.shape[1]), lambda i, j: (i, 0)))
        args.append(a)
    for w in w_list:
        assert not w.transposed
        in_specs.append(w.spec(tn, single_buffer=tn == n_out))
        args.append(w.arr)
    if res is not None:
        in_specs.append(pl.BlockSpec((tm, tn), lambda i, j: (i, j)))
        args.append(res)
    out_specs, out_shape = [], []
    if keep_raw:
        out_specs.append(pl.BlockSpec((tm, tn), lambda i, j: (i, j)))
        out_shape.append(jax.ShapeDtypeStruct((n, n_out), out_dtype))
    if norm_out is not None:
        assert tn == n_out
        g, norm_dtype = norm_out
        in_specs.append(pl.BlockSpec((1, n_out), lambda i, j: (0, 0)))
        args.append(g.reshape(1, n_out))
        out_specs.append(pl.BlockSpec((tm, tn), lambda i, j: (i, j)))
        out_shape.append(jax.ShapeDtypeStruct((n, n_out), norm_dtype))
    rope_j = rope_chunks = None
    if rope is not None:
        cos, sin, rope_j, rope_chunks = rope
        n_seq = seq // tm
        for t in (cos, sin):
            in_specs.append(pl.BlockSpec((tm, LANES), lambda i, j: (i % n_seq, 0)))
            args.append(t)
    outs = pl.pallas_call(
        functools.partial(_mm_kernel, n_a=n_a, has_res=res is not None, has_norm_out=norm_out is not None,
                          keep_raw=keep_raw, rope_j=rope_j, rope_chunks=rope_chunks),
        grid=(n // tm, n_out // tn),
        in_specs=in_specs,
        out_specs=out_specs,
        out_shape=out_shape,
        compiler_params=_params(("parallel", "arbitrary")),
        name=name,
    )(*args)
    return outs if len(outs) > 1 else outs[0]


def _hybrid_proj_kernel(xn_ref, w_ref, *rest, n_main, n_side, rope_j, rope_chunks):
    side_refs = rest[:n_side]
    cos_ref, sin_ref, o_ref, p_ref = rest[n_side:]
    j = pl.program_id(1)

    @pl.when(j < n_main)
    def _():
        acc = _dot_t(xn_ref[...], w_ref[...].astype(BF))
        is_rope = (j >= rope_j[0]) & (j < rope_j[1])
        cos, sin = cos_ref[...], sin_ref[...]
        for c, on in enumerate(rope_chunks):
            blk = acc[:, c * LANES:(c + 1) * LANES]
            if on:
                blk = jnp.where(is_rope, _rope_lanes(blk, cos, sin), blk)
            o_ref[:, c * LANES:(c + 1) * LANES] = blk.astype(o_ref.dtype)

    @pl.when(j == n_main)
    def _():
        w_side = jnp.concatenate([w[...] for w in side_refs], axis=0).astype(BF)
        s = _dot_t(xn_ref[...], w_side)
        for p in range(s.shape[1] // LANES):
            p_ref[p] = s[:, p * LANES:(p + 1) * LANES]


def _hybrid_proj(xn, wt, main_starts, side_windows, rope, *, seq, tm=1024, tn=512):
    n, d = xn.shape
    cos, sin, rope_j, rope_chunks = rope
    n_main = len(main_starts)
    n_planes = sum(rows for _, rows in side_windows) // LANES
    n_seq = seq // tm
    last = n_main - 1
    side_specs = [pl.BlockSpec((pl.Element(rows), pl.Element(d)), lambda i, j, start=start: (start, 0),
                               pipeline_mode=pl.Buffered(1)) for start, rows in side_windows]
    return pl.pallas_call(
        functools.partial(_hybrid_proj_kernel, n_main=n_main, n_side=len(side_windows), rope_j=rope_j,
                          rope_chunks=rope_chunks),
        grid=(n // tm, n_main + 1),
        in_specs=[
            pl.BlockSpec((tm, d), lambda i, j: (i, 0)),
            _W(wt, row_starts=main_starts).spec(tn),
            *side_specs,
            pl.BlockSpec((tm, LANES), lambda i, j: (i % n_seq, 0)),
            pl.BlockSpec((tm, LANES), lambda i, j: (i % n_seq, 0)),
        ],
        out_specs=[pl.BlockSpec((tm, tn), lambda i, j: (i, jnp.minimum(j, last))),
                   pl.BlockSpec((n_planes, tm, LANES), lambda i, j: (0, i, 0))],
        out_shape=[jax.ShapeDtypeStruct((n, n_main * tn), BF), jax.ShapeDtypeStruct((n_planes, n, LANES), F32)],
        compiler_params=_params(("parallel", "arbitrary")),
        name="proj_hybrid",
    )(xn, wt, *([wt] * len(side_windows)), cos, sin)


def _compress_kernel(x_ref, pe_ref, w1_ref, w2_ref, o_ref):
    n_blk = o_ref.shape[2]
    ha = jnp.zeros((n_blk, CMP_HIDDEN), F32)
    hb = jnp.zeros((n_blk, CMP_HIDDEN), F32)
    for l in range(CMP_STRIDE):
        xl = x_ref[0, pl.ds(l, n_blk, stride=CMP_STRIDE), :]
        wa = w1_ref[0, l * HEAD_DIM:(l + 1) * HEAD_DIM, :].astype(BF)
        wb = w1_ref[0, (CMP_STRIDE + l) * HEAD_DIM:(CMP_STRIDE + l + 1) * HEAD_DIM, :].astype(BF)
        ha += jnp.dot((xl + pe_ref[0, l:l + 1, :]).astype(BF), wa, preferred_element_type=F32)
        hb += jnp.dot((xl + pe_ref[0, CMP_STRIDE + l:CMP_STRIDE + l + 1, :]).astype(BF), wb,
                      preferred_element_type=F32)
    hid = ha + pltpu.roll(hb, n_blk - 1, axis=0)
    act = jax.nn.gelu(hid, approximate=True).astype(BF)
    o_ref[0, 0] = jnp.dot(act, w2_ref[0].astype(BF), preferred_element_type=F32)


def _compress(planes, pe, w1, w2, *, batch, seq):
    rows = seq // CMP_STRIDE
    return pl.pallas_call(
        _compress_kernel,
        grid=(2 * NSA_GROUPS, batch),
        in_specs=[
            pl.BlockSpec((1, seq, HEAD_DIM), lambda c, b: (c, b, 0)),
            pl.BlockSpec((1, CMP_LEN, HEAD_DIM), lambda c, b: (c // NSA_GROUPS, 0, 0)),
            pl.BlockSpec((1, CMP_LEN * HEAD_DIM, CMP_HIDDEN), lambda c, b: (c // NSA_GROUPS, 0, 0)),
            pl.BlockSpec((1, CMP_HIDDEN, HEAD_DIM), lambda c, b: (c // NSA_GROUPS, 0, 0)),
        ],
        out_specs=pl.BlockSpec((1, 1, rows, HEAD_DIM), lambda c, b: (c, b, 0, 0)),
        out_shape=jax.ShapeDtypeStruct((2 * NSA_GROUPS, batch, rows, HEAD_DIM), F32),
        compiler_params=_params(("arbitrary", "arbitrary")),
        name="nsa_compress",
    )(planes, pe, w1, w2)


def _stacked_softmax_pv(s_raw, bias, v, scale, n_stack, tq):
    exp2_scale = scale * float(np.log2(np.e))
    es = []
    for h in range(n_stack):
        u = s_raw[h * tq:(h + 1) * tq] + bias
        es.append(jnp.exp2((u - jnp.max(u, axis=-1, keepdims=True)) * exp2_scale).astype(BF))
    v_ones = jnp.concatenate([v, jnp.ones_like(v)], axis=1)
    o = jnp.dot(jnp.concatenate(es, axis=0), v_ones, preferred_element_type=F32)
    d = v.shape[1]
    return o[:, :d] / o[:, d:d + 1]


def _nsa_kernel(q_ref, ks_ref, kw_ref, vs_ref, vw_ref, kvc_ref, gate_ref, cos_ref, sin_ref, m_ref, o_ref,
                *, tq, seq, key_step):
    qi = pl.program_id(1)
    hg = HEADS_PER_GROUP
    scale = HEAD_DIM ** -0.5
    n_cmp = (seq - CMP_LEN) // CMP_STRIDE + 1
    n_slc = seq // SLC_LEN
    band = WIN + tq

    t_row = qi * tq + lax.broadcasted_iota(jnp.int32, (tq, 1), 0)
    cos = jnp.concatenate([cos_ref[...]] * hg, axis=0)
    sin = jnp.concatenate([sin_ref[...]] * hg, axis=0)
    gates = jax.nn.sigmoid(gate_ref[0])

    n_lane = lax.broadcasted_iota(jnp.int32, (1, LANES), 1)
    cmask = (n_lane * CMP_STRIDE + (CMP_LEN - 1) <= t_row) & (n_lane < n_cmp)
    cbias = jnp.where(cmask, 0.0, NEG)
    j_blk = lax.broadcasted_iota(jnp.int32, (n_slc, 1), 0)
    slc_shift = SLC_LEN.bit_length() - 1
    cur = jnp.right_shift(qi * tq + lax.broadcasted_iota(jnp.int32, (1, tq), 1), slc_shift)
    forced = (j_blk == 0) | (j_blk == cur) | (j_blk == cur - 1)
    w_start = pl.multiple_of(jnp.maximum(qi * tq - WIN, 0), tq)
    w_pos = w_start + lax.broadcasted_iota(jnp.int32, (1, band), 1)
    wbias = jnp.where((w_pos <= t_row) & (w_pos > t_row - WIN), 0.0, NEG)

    def group_queries(g):
        qg = q_ref[0, :, g * hg * HEAD_DIM:(g + 1) * hg * HEAD_DIM].astype(F32)
        qp = jnp.concatenate([qg[:, h * HEAD_DIM:(h + 1) * HEAD_DIM] for h in range(hg)], axis=0)
        return qp.astype(BF), _rope_lanes(qp, cos, sin).astype(BF)

    def compressed_branch(g, qp):
        kc = kvc_ref[g, 0].astype(BF)
        vc = kvc_ref[NSA_GROUPS + g, 0].astype(BF)
        s = _dot_t(qp, kc)
        ps = []
        for h in range(hg):
            u = s[h * tq:(h + 1) * tq] * scale + cbias
            e, l = _softmax_rows(u)
            ps.append(jnp.where(cmask, e / l, 0.0))
        o_cmp = jnp.dot(jnp.concatenate(ps, axis=0).astype(BF), vc, preferred_element_type=F32)

        p_sum = ps[0]
        for h in range(1, hg):
            p_sum = p_sum + ps[h]
        imp = lax.dot_general(m_ref[...], p_sum, (((1,), (1,)), ((), ())), preferred_element_type=F32,
                              precision=lax.Precision.HIGHEST)
        imp = jnp.where(j_blk > cur, -jnp.inf, jnp.where(forced, jnp.inf, imp))
        rank = jnp.zeros((n_slc, tq), jnp.int32)
        for i in range(n_slc):
            ci = imp[i:i + 1, :]
            beats = (ci > imp) | ((ci == imp) & (j_blk > i))
            rank = rank + beats.astype(jnp.int32)
        return o_cmp, jnp.where(rank < SLC_TOP, 0.0, NEG).astype(BF)

    qs = [group_queries(g) for g in range(NSA_GROUPS)]
    cmps = [compressed_branch(g, qs[g][0]) for g in range(NSA_GROUPS)]

    def attend(width):
        blocks = width // SLC_LEN
        expand = jnp.where(jnp.right_shift(lax.broadcasted_iota(jnp.int32, (blocks, width), 1), slc_shift)
                           == lax.broadcasted_iota(jnp.int32, (blocks, width), 0), 1.0, 0.0).astype(BF)
        s_pos = lax.broadcasted_iota(jnp.int32, (1, width), 1)
        lanes = [slice(g * HEAD_DIM, (g + 1) * HEAD_DIM) for g in range(NSA_GROUPS)]
        s_win = [_dot_t(qs[g][1], kw_ref[0, pl.ds(w_start, band), lanes[g]]) for g in range(NSA_GROUPS)]
        s_slc = [_dot_t(qs[g][1], ks_ref[0, :width, lanes[g]]) for g in range(NSA_GROUPS)]
        for g in range(NSA_GROUPS):
            o_win = _stacked_softmax_pv(s_win[g], wbias, vw_ref[0, pl.ds(w_start, band), lanes[g]], scale, hg, tq)
            o_cmp, sel_neg = cmps[g]
            bias = lax.dot_general(sel_neg[:blocks], expand, (((0,), (0,)), ((), ())),
                                   preferred_element_type=F32)
            bias = jnp.where(s_pos <= t_row, bias, NEG)
            o_slc = _stacked_softmax_pv(s_slc[g], bias, vs_ref[0, :width, lanes[g]], scale, hg, tq)
            for hh in range(hg):
                h = g * hg + hh
                rows = slice(hh * tq, (hh + 1) * tq)
                o = (o_cmp[rows] * gates[:, 3 * h:3 * h + 1]
                     + o_slc[rows] * gates[:, 3 * h + 1:3 * h + 2]
                     + o_win[rows] * gates[:, 3 * h + 2:3 * h + 3])
                o_ref[0, :, h * HEAD_DIM:(h + 1) * HEAD_DIM] = o.astype(o_ref.dtype)

    n_widths = seq // key_step
    super_row = qi // (key_step // tq)
    for k in range(n_widths):
        pl.when(super_row == k)(functools.partial(attend, (k + 1) * key_step))


def _nsa(main, kvc, planes, gate_plane, cos, sin, m_cs, *, batch, seq, tq=128, key_step=512):
    nq = seq // tq
    kv_w = NSA_GROUPS * HEAD_DIM
    kv0 = NSA_HEADS * HEAD_DIM // kv_w
    return pl.pallas_call(
        functools.partial(_nsa_kernel, tq=tq, seq=seq, key_step=key_step),
        grid=(batch, nq),
        in_specs=[
            pl.BlockSpec((1, tq, NSA_HEADS * HEAD_DIM), lambda b, i: (b, i, 0)),
            pl.BlockSpec((1, seq, kv_w), lambda b, i: (b, 0, kv0)),
            pl.BlockSpec((1, seq, kv_w), lambda b, i: (b, 0, kv0 + 2)),
            pl.BlockSpec((1, seq, kv_w), lambda b, i: (b, 0, kv0 + 1)),
            pl.BlockSpec((1, seq, kv_w), lambda b, i: (b, 0, kv0 + 3)),
            pl.BlockSpec((2 * NSA_GROUPS, 1, LANES, HEAD_DIM), lambda b, i: (0, b, 0, 0)),
            pl.BlockSpec((1, tq, LANES), lambda b, i: (gate_plane, b * nq + i, 0)),
            pl.BlockSpec((tq, LANES), lambda b, i: (i, 0)),
            pl.BlockSpec((tq, LANES), lambda b, i: (i, 0)),
            pl.BlockSpec(m_cs.shape, lambda b, i: (0, 0)),
        ],
        out_specs=pl.BlockSpec((1, tq, NSA_HEADS * HEAD_DIM), lambda b, i: (b, i, 0)),
        out_shape=jax.ShapeDtypeStruct((batch, seq, NSA_HEADS * HEAD_DIM), BF),
        compiler_params=_params(("parallel", "arbitrary")),
        name="nsa_attention",
    )(main, main, main, main, main, kvc, planes, cos, sin, m_cs)


def _cumdecay_kernel(x_ref, b_ref, o_ref):
    z = x_ref[0] + b_ref[...]
    c = jnp.minimum(z, 0.0) - jnp.log1p(jnp.exp(-jnp.abs(z)))
    rows = c.shape[0]
    r = lax.broadcasted_iota(jnp.int32, (rows, 1), 0)
    k = 1
    while k < rows:
        c = c + jnp.where(r >= k, pltpu.roll(c, k, axis=0), 0.0)
        k *= 2
    o_ref[0] = c.T


def _cumdecay(planes, plane, bias, *, batch, seq):
    return pl.pallas_call(
        _cumdecay_kernel,
        grid=(batch,),
        in_specs=[pl.BlockSpec((1, seq, LANES), lambda b: (plane, b, 0)),
                  pl.BlockSpec((1, LANES), lambda b: (0, 0))],
        out_specs=pl.BlockSpec((1, LANES, seq), lambda b: (b, 0, 0)),
        out_shape=jax.ShapeDtypeStruct((batch, LANES, seq), F32),
        compiler_params=_params(("arbitrary",)),
        name="fox_cumdecay",
    )(planes, bias)


def _attn_kernel(*refs, tq, seq, scale, n_qk, decay):
    q_ref = refs[0]
    k_refs = refs[1:1 + n_qk]
    v_ref = refs[1 + n_qk]
    k = 2 + n_qk
    if decay:
        ck_ref = refs[k]
        k += 1
    o_ref = refs[k]

    row = lax.broadcasted_iota(jnp.int32, (tq, 1), 0)
    col = lax.broadcasted_iota(jnp.int32, (1, tq), 1)
    tri = col <= row
    exp2_scale = scale * float(np.log2(np.e))

    k_all = k_refs[0][0] if n_qk == 1 else jnp.concatenate([k_ref[0] for k_ref in k_refs], axis=1)

    v_ones = jnp.concatenate([v_ref[0], jnp.ones((seq, HEAD_DIM), BF)], axis=1)
    if decay:
        head_row = lax.broadcasted_iota(jnp.int32, (ck_ref.shape[1], 1), 0) == pl.program_id(1)
        ck_head = jnp.sum(jnp.where(head_row, ck_ref[0], 0.0), axis=0, keepdims=True)

    n_tiles = seq // tq
    scores = lambda i: _dot_t(q_ref[0, i * tq:(i + 1) * tq, :], k_all[:(i + 1) * tq])
    s_next = scores(n_tiles - 1)
    for i in reversed(range(n_tiles)):
        lo, hi = i * tq, (i + 1) * tq
        s = s_next
        if i > 0:
            s_next = scores(i - 1)
        if decay:
            ck = ck_head[:, :hi]
            cq = jnp.sum(jnp.where(col == row, ck[:, lo:hi], 0.0), axis=1, keepdims=True)
            s = s * scale + (cq - ck)
            prob = lambda u, m: jnp.exp(u - m)
        else:
            prob = lambda u, m: jnp.exp2((u - m) * exp2_scale)
        s_diag = jnp.where(tri, s[:, lo:hi], NEG)
        m = jnp.max(s_diag, axis=-1, keepdims=True)
        if i > 0:
            m = jnp.maximum(m, jnp.max(s[:, :lo], axis=-1, keepdims=True))
        e = prob(s_diag, m).astype(BF)
        if i > 0:
            e = jnp.concatenate([prob(s[:, :lo], m).astype(BF), e], axis=1)
        o = jnp.dot(e, v_ones[:hi], preferred_element_type=F32)
        o_ref[0, lo:hi, :] = (o[:, :HEAD_DIM] / o[:, HEAD_DIM:HEAD_DIM + 1]).astype(o_ref.dtype)


def _attention(q_arr, q_blk0, k_arrs, k_blk0s, k_per_head, v_arr, v_blk0, *, batch, seq, heads, scale,
               decay=None, tq=256, name="attention"):
    n_qk = len(k_arrs)
    in_specs = [pl.BlockSpec((1, seq, n_qk * HEAD_DIM), lambda b, h: (b, 0, q_blk0 + h))]
    args = [q_arr]
    for arr, blk0, per_head in zip(k_arrs, k_blk0s, k_per_head):
        if per_head:
            in_specs.append(pl.BlockSpec((1, seq, HEAD_DIM), lambda b, h, blk0=blk0: (b, 0, blk0 + h)))
        else:
            in_specs.append(pl.BlockSpec((1, seq, HEAD_DIM), lambda b, h, blk0=blk0: (b, 0, blk0)))
        args.append(arr)
    in_specs.append(pl.BlockSpec((1, seq, HEAD_DIM), lambda b, h: (b, 0, v_blk0 + h)))
    args.append(v_arr)
    if decay is not None:
        cum, row0 = decay
        assert row0 % heads == 0 and heads % 8 == 0
        in_specs.append(pl.BlockSpec((1, heads, seq), lambda b, h: (b, row0 // heads, 0)))
        args.append(cum)
    return pl.pallas_call(
        functools.partial(_attn_kernel, tq=tq, seq=seq, scale=scale, n_qk=n_qk, decay=decay is not None),
        grid=(batch, heads),
        in_specs=in_specs,
        out_specs=pl.BlockSpec((1, seq, HEAD_DIM), lambda b, h: (b, 0, h)),
        out_shape=jax.ShapeDtypeStruct((batch, seq, heads * HEAD_DIM), BF),
        compiler_params=_params(("parallel", "arbitrary")),
        name=name,
    )(*args)


def _mla_latent_kernel(xn_ref, wt_ref, wkr_ref, gq_ref, gkv_ref, cos_ref, sin_ref, cq_ref, ckv_ref, kr_ref):
    h = xn_ref[...]
    c = _dot_t(h, wt_ref[...].astype(BF))
    cq_ref[...] = _rms_normed(c[:, :MLA_Q_RANK], gq_ref[...]).astype(BF)
    ckv_ref[...] = _rms_normed(c[:, MLA_Q_RANK:], gkv_ref[...]).astype(BF)
    kr_ref[...] = _rope_lanes(_dot_t(h, wkr_ref[...].astype(BF)), cos_ref[...], sin_ref[...]).astype(BF)


def _mla_latent(xn, wt, w_kr, gq, gkv, cos, sin, *, seq, tm=512):
    n = xn.shape[0]
    n_seq = seq // tm
    return pl.pallas_call(
        _mla_latent_kernel,
        grid=(n // tm,),
        in_specs=[
            pl.BlockSpec((tm, xn.shape[1]), lambda i: (i, 0)),
            pl.BlockSpec((MLA_Q_RANK + MLA_KV_RANK, wt.shape[1]), lambda i: (0, 0)),
            pl.BlockSpec(w_kr.shape, lambda i: (0, 0)),
            pl.BlockSpec((1, MLA_Q_RANK), lambda i: (0, 0)),
            pl.BlockSpec((1, MLA_KV_RANK), lambda i: (0, 0)),
            pl.BlockSpec((tm, LANES), lambda i: (i % n_seq, 0)),
            pl.BlockSpec((tm, LANES), lambda i: (i % n_seq, 0)),
        ],
        out_specs=[
            pl.BlockSpec((tm, MLA_Q_RANK), lambda i: (i, 0)),
            pl.BlockSpec((tm, MLA_KV_RANK), lambda i: (i, 0)),
            pl.BlockSpec((tm, LANES), lambda i: (i, 0)),
        ],
        out_shape=[
            jax.ShapeDtypeStruct((n, MLA_Q_RANK), BF),
            jax.ShapeDtypeStruct((n, MLA_KV_RANK), BF),
            jax.ShapeDtypeStruct((n, LANES), BF),
        ],
        compiler_params=_params(("parallel",)),
        name="mla_latent",
    )(xn, wt, w_kr, gq.reshape(1, -1), gkv.reshape(1, -1), cos, sin)


def _rope_tables(seq, dim):
    half = dim // 2
    inv = 1.0 / (ROPE_THETA ** (np.arange(0, dim, 2, dtype=np.float32) / dim))
    ang = np.arange(seq, dtype=np.float32)[:, None] * inv.astype(np.float32)[None, :]
    cos = np.zeros((seq, LANES), np.float32)
    sin = np.zeros((seq, LANES), np.float32)
    cos[:, :half] = np.cos(ang)
    cos[:, LANES // 2:LANES // 2 + half] = np.cos(ang)
    sin[:, :half] = -np.sin(ang)
    sin[:, LANES // 2:LANES // 2 + half] = np.sin(ang)
    return jnp.asarray(cos), jnp.asarray(sin)


def _cmp_to_slc(seq):
    n_cmp = (seq - CMP_LEN) // CMP_STRIDE + 1
    n_slc = seq // SLC_LEN
    c0 = np.arange(n_cmp) * CMP_STRIDE
    c1 = c0 + CMP_LEN
    s0 = np.arange(n_slc) * SLC_LEN
    s1 = s0 + SLC_LEN
    ov = np.clip(np.minimum(c1[:, None], s1[None, :]) - np.maximum(c0[:, None], s0[None, :]), 0, None)
    m = np.zeros((n_slc, LANES), np.float32)
    m[:, :n_cmp] = (ov / CMP_LEN).T
    return jnp.asarray(m)


def _spread_rope_cols(w):
    half = MLA_ROPE // 2
    z = jnp.zeros(w.shape[:-1] + (LANES // 2 - half,), w.dtype)
    return jnp.concatenate([w[..., :half], z, w[..., half:], z], axis=-1)


def _hybrid_mixer(x, xn, g_next, w_in, w_out, layer, cmp_pe, cmp_w1, cmp_w2, f_bias, *, batch, seq):
    n = batch * seq
    nq_w = NSA_HEADS * HEAD_DIM
    kv_w = NSA_GROUPS * HEAD_DIM
    fx_w = FOX_HEADS * HEAD_DIM
    c_cmp, c_slc, c_gate = nq_w, nq_w + 2 * kv_w, nq_w + 6 * kv_w
    c_fox = c_gate + 3 * NSA_HEADS
    c_f = c_fox + 3 * fx_w
    wt = jnp.swapaxes(w_in[layer], 0, 1)

    cos, sin = _rope_tables(seq, HEAD_DIM)
    tn = 4 * kv_w
    main_starts = tuple(range(0, c_cmp, tn)) + tuple(range(c_slc, c_gate, tn)) + tuple(range(c_fox, c_f, tn))
    f_lane0 = LANES - FOX_HEADS
    side_windows = ((c_cmp, c_slc - c_cmp), (c_gate, LANES), (c_f - f_lane0, LANES))
    gate_plane, f_plane = (c_slc - c_cmp) // LANES, (c_slc - c_cmp) // LANES + 1
    main, planes = _hybrid_proj(xn, wt, main_starts, side_windows,
                                (cos, sin, (nq_w // tn, nq_w // tn + 1), (True, True, False, False) * 2), seq=seq,
                                tn=tn)

    kvc = _compress(planes, cmp_pe, cmp_w1, cmp_w2, batch=batch, seq=seq)

    main3 = main.reshape(batch, seq, -1)
    o_nsa = _nsa(main3, kvc, planes, gate_plane, cos, sin, _cmp_to_slc(seq), batch=batch, seq=seq)

    bias = jnp.zeros((1, LANES), F32).at[0, f_lane0:].set(f_bias)
    cum = _cumdecay(planes, f_plane, bias, batch=batch, seq=seq)
    blk = lambda col: col // HEAD_DIM
    c_qf = nq_w + 4 * kv_w
    o_fox = _attention(main3, blk(c_qf), [main3], [blk(c_qf + fx_w)], [True], main3, blk(c_qf + 2 * fx_w),
                       batch=batch, seq=seq, heads=FOX_HEADS, scale=HEAD_DIM ** -0.5,
                       decay=(cum, f_lane0), name="fox_attention")

    return _mm([o_nsa.reshape(n, nq_w), o_fox.reshape(n, fx_w)],
               [_W(w_out, layer=layer, rows=nq_w, row_blk=0), _W(w_out, layer=layer, rows=fx_w, row_blk=1)],
               D_MODEL, out_dtype=F32, tm=512, tn=D_MODEL, res=x, norm_out=(g_next, BF), name="proj_out_hybrid")


def _mla_mixer(x, xn, g_next, w_in, q_norm, kv_norm, w_uq, w_ukv, w_out, layer, *, batch, seq):
    n = batch * seq
    qk = MLA_NOPE + MLA_ROPE
    wt_lat = jnp.swapaxes(w_in, 0, 1)
    w_kr = jnp.swapaxes(_spread_rope_cols(w_in[:, MLA_Q_RANK + MLA_KV_RANK:]), 0, 1)
    wq = w_uq.reshape(MLA_Q_RANK, MLA_HEADS, qk)
    wq = jnp.concatenate([wq[..., :MLA_NOPE], _spread_rope_cols(wq[..., MLA_NOPE:])], axis=-1)
    wq = wq.reshape(MLA_Q_RANK, MLA_HEADS * 2 * HEAD_DIM)
    wkv = w_ukv.reshape(MLA_KV_RANK, MLA_HEADS, MLA_NOPE + MLA_V)
    wkv = jnp.concatenate([wkv[..., :MLA_NOPE].reshape(MLA_KV_RANK, -1),
                           wkv[..., MLA_NOPE:].reshape(MLA_KV_RANK, -1)], axis=1)

    cos, sin = _rope_tables(seq, MLA_ROPE)
    cq, ckv, kr = _mla_latent(xn, wt_lat, w_kr, q_norm, kv_norm, cos, sin, seq=seq)
    tn = wq.shape[1]
    q = _mm([cq], [_W(wq)], wq.shape[1], out_dtype=BF, tm=512, tn=tn,
            rope=(cos, sin, (0, 1), (False, True) * (tn // (2 * LANES))), seq=seq, name="proj_mla_q")
    kv = _mm([ckv], [_W(wkv)], wkv.shape[1], out_dtype=BF, tm=512, tn=tn, name="proj_mla_kv")

    q3 = q.reshape(batch, seq, -1)
    kv3 = kv.reshape(batch, seq, -1)
    kr3 = kr.reshape(batch, seq, LANES)
    o = _attention(q3, 0, [kv3, kr3], [0, 0], [True, False], kv3, MLA_HEADS, batch=batch, seq=seq,
                   heads=MLA_HEADS, scale=qk ** -0.5, name="mla_attention")
    return _mm([o.reshape(n, MLA_HEADS * MLA_V)], [_W(w_out, layer=layer)], D_MODEL, out_dtype=F32, tm=512, tn=D_MODEL,
               res=x, norm_out=(g_next, BF), name="proj_out_mla")


def _ffn(x, xn, norms, w_in, w_out, layer, norm_out, keep_raw=True):
    depth, d = norms.shape
    a, w_out_bf = _ffn_up(x if xn is None else xn, norms.reshape(depth, 1, d), w_in, w_out, layer)
    return _mm([a], [_W(w_out_bf)], d, out_dtype=F32, tm=512, tn=d, res=x, norm_out=norm_out, keep_raw=keep_raw,
               name="ffn_down")


def kernel(x, ffn1_norm, ffn1_w_in, ffn1_w_out, mix_norm, ffn2_norm, ffn2_w_in, ffn2_w_out, hyb_w_in, hyb_w_out, nsa_cmp_pe, nsa_cmp_w1, nsa_cmp_w2, fox_f_bias, mla_w_in, mla_q_norm, mla_kv_norm, mla_w_uq, mla_w_ukv, mla_w_out, final_norm):
    batch, seq, d = x.shape
    depth = ffn1_norm.shape[0]
    x = x.reshape(batch * seq, d)
    xn = None
    for i in range(depth):
        x, xn = _ffn(x, xn, ffn1_norm, ffn1_w_in, ffn1_w_out, i, (mix_norm[i], BF))
        if i % 2 == 0:
            e = i // 2
            x, xn = _hybrid_mixer(x, xn, ffn2_norm[i], hyb_w_in, hyb_w_out, e, nsa_cmp_pe[e], nsa_cmp_w1[e],
                                  nsa_cmp_w2[e], fox_f_bias[e], batch=batch, seq=seq)
        else:
            o = i // 2
            x, xn = _mla_mixer(x, xn, ffn2_norm[i], mla_w_in[o], mla_q_norm[o], mla_kv_norm[o], mla_w_uq[o],
                               mla_w_ukv[o], mla_w_out, o, batch=batch, seq=seq)
        if i + 1 < depth:
            x, xn = _ffn(x, xn, ffn2_norm, ffn2_w_in, ffn2_w_out, i, (ffn1_norm[i + 1], BF))
        else:
            out = _ffn(x, xn, ffn2_norm, ffn2_w_in, ffn2_w_out, i, (final_norm, F32), keep_raw=False)
    return out.reshape(batch, seq, d)
```

```python
import functools

import numpy as np
import jax
import jax.numpy as jnp
from jax import lax
from jax.experimental import pallas as pl
from jax.experimental.pallas import tpu as pltpu

D_MODEL = 2048
HEAD_DIM = 128
ROPE_THETA = 10000.0
NORM_EPS = 1e-6
NEG = -1e30

NSA_HEADS = 8
NSA_GROUPS = 2
HEADS_PER_GROUP = NSA_HEADS // NSA_GROUPS
CMP_LEN = 32
CMP_STRIDE = 16
CMP_HIDDEN = 256
SLC_LEN = 64
SLC_TOP = 8
WIN = 512
FOX_HEADS = 8
MLA_HEADS = 16
MLA_Q_RANK = 512
MLA_KV_RANK = 512
MLA_NOPE = 128
MLA_ROPE = 64
MLA_V = 128
D_FF = 5632

LANES = 128
VMEM_LIMIT = 56 * 1024 * 1024

ROWS_STREAMED = 1024
ROWS_RESIDENT = 512
FFN_COLS = 512
FFN_CAST_ROWS = 256
QUERY_TILE = 256
NSA_KEY_STEP = 512

BF = jnp.bfloat16
F32 = jnp.float32


def _params(sem, vmem=VMEM_LIMIT):
    return pltpu.CompilerParams(dimension_semantics=sem, vmem_limit_bytes=vmem)


def _rope_lanes(x, cos, sin):
    return x * cos + pltpu.roll(x, LANES // 2, axis=1) * sin


def _dot_t(a, b):
    return lax.dot_general(a, b, (((1,), (1,)), ((), ())), preferred_element_type=F32)


def _softmax_rows(s):
    m = jnp.max(s, axis=-1, keepdims=True)
    e = jnp.exp(s - m)
    return e, jnp.sum(e, axis=-1, keepdims=True)


def _rms_normed(x, g):
    ms = jnp.mean(x * x, axis=-1, keepdims=True)
    return x * lax.rsqrt(ms + NORM_EPS) * g


def _ffn_up_kernel(x_ref, g_ref, wa_ref, wb_ref, w2_ref, o_ref, w2b_ref, *scratch, n_cast):
    i, j = pl.program_id(0), pl.program_id(1)

    if scratch:
        xn_ref, = scratch

        @pl.when(j == 0)
        def _():
            xn_ref[...] = _rms_normed(x_ref[...], g_ref[...]).astype(BF)
    else:
        xn_ref = x_ref

    @pl.when(i * pl.num_programs(1) + j < n_cast)
    def _():
        w2b_ref[...] = w2_ref[...].astype(BF)

    xn = xn_ref[...]
    half = o_ref.shape[1] // 2
    cols = [slice(c * half, (c + 1) * half) for c in range(2)]
    hs = [(jnp.dot(xn, wa_ref[:, cs].astype(BF), preferred_element_type=F32),
           jnp.dot(xn, wb_ref[:, cs].astype(BF), preferred_element_type=F32)) for cs in cols]
    for cs, (h1, h2) in zip(cols, hs):
        o_ref[:, cs] = (h1 * jax.nn.sigmoid(h1) * h2 * 0.5).astype(BF)


def _ffn_up(x, g, w_in, w_out, layer, *, tm=ROWS_STREAMED, tf=FFN_COLS, cast_rows=FFN_CAST_ROWS):
    n, d = x.shape
    n_f = D_FF // tf
    n_cast = D_FF // cast_rows
    assert n_cast <= (n // tm) * n_f
    chunk = lambda i, j: jnp.minimum(i * n_f + j, n_cast - 1)
    return pl.pallas_call(
        functools.partial(_ffn_up_kernel, n_cast=n_cast),
        grid=(n // tm, n_f),
        in_specs=[
            pl.BlockSpec((tm, d), lambda i, j: (i, 0)),
            pl.BlockSpec((None, 1, d), lambda i, j: (layer, 0, 0)),
            pl.BlockSpec((None, d, tf), lambda i, j: (layer, 0, j)),
            pl.BlockSpec((None, d, tf), lambda i, j: (layer, 0, n_f + j)),
            pl.BlockSpec((None, cast_rows, d), lambda i, j: (layer, chunk(i, j), 0)),
        ],
        out_specs=[pl.BlockSpec((tm, tf), lambda i, j: (i, j)),
                   pl.BlockSpec((cast_rows, d), lambda i, j: (chunk(i, j), 0))],
        out_shape=[jax.ShapeDtypeStruct((n, D_FF), BF), jax.ShapeDtypeStruct((D_FF, d), BF)],
        scratch_shapes=[pltpu.VMEM((tm, d), BF)] if x.dtype == F32 else [],
        compiler_params=_params(("arbitrary", "arbitrary")),
        name="ffn_up",
    )(x, g, w_in, w_in, w_out)


class _W:
    def __init__(self, arr, *, layer=None, rows=None, row_blk=0, col_blk0=0, row_starts=None):
        self.arr, self.layer, self.row_blk, self.col_blk0 = arr, layer, row_blk, col_blk0
        self.transposed = row_starts is not None
        self.row_starts = row_starts
        assert row_starts is None or all(r % 8 == 0 for r in row_starts)
        self.rows = rows if rows is not None else arr.shape[-2]

    def spec(self, tn, single_buffer=False):
        layer, row_blk, col_blk0 = self.layer, self.row_blk, self.col_blk0
        if self.transposed:
            starts = self.row_starts

            def start(j):
                off = starts[-1]
                for k in range(len(starts) - 2, -1, -1):
                    off = jnp.where(j == k, starts[k], off)
                return pl.multiple_of(off, 8)

            return pl.BlockSpec((pl.Element(tn), pl.Element(self.arr.shape[1])), lambda i, j: (start(j), 0))
        kw = dict(pipeline_mode=pl.Buffered(1)) if single_buffer else {}
        if self.arr.ndim == 3:
            return pl.BlockSpec((None, self.rows, tn), lambda i, j: (layer, row_blk, col_blk0 + j), **kw)
        return pl.BlockSpec((self.rows, tn), lambda i, j: (row_blk, col_blk0 + j), **kw)


def _mm_kernel(*refs, n_a, has_res, has_norm_out, keep_raw, rope_j, rope_chunks):
    a_refs = refs[:n_a]
    w_refs = refs[n_a:2 * n_a]
    k = 2 * n_a
    if has_res:
        res_ref = refs[k]
        k += 1
    if has_norm_out:
        g_ref = refs[k]
        k += 1
    if rope_chunks is not None:
        cos_ref, sin_ref = refs[k], refs[k + 1]
        k += 2
    out_refs = refs[k:]

    acc = jnp.dot(a_refs[0][...], w_refs[0][...].astype(BF), preferred_element_type=F32)
    for a_ref, w_ref in zip(a_refs[1:], w_refs[1:]):
        acc += jnp.dot(a_ref[...], w_ref[...].astype(BF), preferred_element_type=F32)
    if has_res:
        acc += res_ref[...]

    if has_norm_out:
        out_refs[-1][...] = _rms_normed(acc, g_ref[...]).astype(out_refs[-1].dtype)
    if not keep_raw:
        return
    o_ref = out_refs[0]
    if rope_chunks is None:
        o_ref[...] = acc.astype(o_ref.dtype)
    else:
        j = pl.program_id(1)
        is_rope = (j >= rope_j[0]) & (j < rope_j[1])
        cos, sin = cos_ref[...], sin_ref[...]
        for c, on in enumerate(rope_chunks):
            blk = acc[:, c * LANES:(c + 1) * LANES]
            if on:
                blk = jnp.where(is_rope, _rope_lanes(blk, cos, sin), blk)
            o_ref[:, c * LANES:(c + 1) * LANES] = blk.astype(o_ref.dtype)


def _mm(a_list, w_list, n_out, *, out_dtype, tm, tn, res=None, norm_out=None, keep_raw=True, rope=None, seq=None,
        name="mm"):
    n = a_list[0].shape[0]
    n_a = len(a_list)
    in_specs, args = [], []
    for a in a_list:
        in_specs.append(pl.BlockSpec((tm, a.shape[1]), lambda i, j: (i, 0)))
        args.append(a)
    for w in w_list:
        assert not w.transposed
        in_specs.append(w.spec(tn, single_buffer=tn == n_out))
        args.append(w.arr)
    if res is not None:
        in_specs.append(pl.BlockSpec((tm, tn), lambda i, j: (i, j)))
        args.append(res)
    out_specs, out_shape = [], []
    if keep_raw:
        out_specs.append(pl.BlockSpec((tm, tn), lambda i, j: (i, j)))
        out_shape.append(jax.ShapeDtypeStruct((n, n_out), out_dtype))
    if norm_out is not None:
        assert tn == n_out
        g, norm_dtype = norm_out
        in_specs.append(pl.BlockSpec((1, n_out), lambda i, j: (0, 0)))
        args.append(g.reshape(1, n_out))
        out_specs.append(pl.BlockSpec((tm, tn), lambda i, j: (i, j)))
        out_shape.append(jax.ShapeDtypeStruct((n, n_out), norm_dtype))
    rope_j = rope_chunks = None
    if rope is not None:
        cos, sin, rope_j, rope_chunks = rope
        n_seq = seq // tm
        for t in (cos, sin):
            in_specs.append(pl.BlockSpec((tm, LANES), lambda i, j: (i % n_seq, 0)))
            args.append(t)
    outs = pl.pallas_call(
        functools.partial(_mm_kernel, n_a=n_a, has_res=res is not None, has_norm_out=norm_out is not None,
                          keep_raw=keep_raw, rope_j=rope_j, rope_chunks=rope_chunks),
        grid=(n // tm, n_out // tn),
        in_specs=in_specs,
        out_specs=out_specs,
        out_shape=out_shape,
        compiler_params=_params(("parallel", "arbitrary")),
        name=name,
    )(*args)
    return outs if len(outs) > 1 else outs[0]


def _hybrid_proj_kernel(xn_ref, w_ref, *rest, n_main, n_side, rope_j, rope_chunks):
    side_refs = rest[:n_side]
    cos_ref, sin_ref, o_ref, p_ref = rest[n_side:]
    j = pl.program_id(1)

    @pl.when(j < n_main)
    def _():
        acc = _dot_t(xn_ref[...], w_ref[...].astype(BF))
        is_rope = (j >= rope_j[0]) & (j < rope_j[1])
        cos, sin = cos_ref[...], sin_ref[...]
        for c, on in enumerate(rope_chunks):
            blk = acc[:, c * LANES:(c + 1) * LANES]
            if on:
                blk = jnp.where(is_rope, _rope_lanes(blk, cos, sin), blk)
            o_ref[:, c * LANES:(c + 1) * LANES] = blk.astype(o_ref.dtype)

    @pl.when(j == n_main)
    def _():
        w_side = jnp.concatenate([w[...] for w in side_refs], axis=0).astype(BF)
        s = _dot_t(xn_ref[...], w_side)
        for p in range(s.shape[1] // LANES):
            p_ref[p] = s[:, p * LANES:(p + 1) * LANES]


def _hybrid_proj(xn, wt, main_starts, side_windows, rope, *, seq, tn, tm=ROWS_STREAMED):
    n, d = xn.shape
    cos, sin, rope_j, rope_chunks = rope
    n_main = len(main_starts)
    n_planes = sum(rows for _, rows in side_windows) // LANES
    n_seq = seq // tm
    last = n_main - 1
    side_specs = [pl.BlockSpec((pl.Element(rows), pl.Element(d)), lambda i, j, start=start: (start, 0),
                               pipeline_mode=pl.Buffered(1)) for start, rows in side_windows]
    return pl.pallas_call(
        functools.partial(_hybrid_proj_kernel, n_main=n_main, n_side=len(side_windows), rope_j=rope_j,
                          rope_chunks=rope_chunks),
        grid=(n // tm, n_main + 1),
        in_specs=[
            pl.BlockSpec((tm, d), lambda i, j: (i, 0)),
            _W(wt, row_starts=main_starts).spec(tn),
            *side_specs,
            pl.BlockSpec((tm, LANES), lambda i, j: (i % n_seq, 0)),
            pl.BlockSpec((tm, LANES), lambda i, j: (i % n_seq, 0)),
        ],
        out_specs=[pl.BlockSpec((tm, tn), lambda i, j: (i, jnp.minimum(j, last))),
                   pl.BlockSpec((n_planes, tm, LANES), lambda i, j: (0, i, 0))],
        out_shape=[jax.ShapeDtypeStruct((n, n_main * tn), BF), jax.ShapeDtypeStruct((n_planes, n, LANES), F32)],
        compiler_params=_params(("parallel", "arbitrary")),
        name="proj_hybrid",
    )(xn, wt, *([wt] * len(side_windows)), cos, sin)


def _compress_kernel(x_ref, pe_ref, w1_ref, w2_ref, o_ref):
    n_blk = o_ref.shape[2]
    ha = jnp.zeros((n_blk, CMP_HIDDEN), F32)
    hb = jnp.zeros((n_blk, CMP_HIDDEN), F32)
    for l in range(CMP_STRIDE):
        xl = x_ref[0, pl.ds(l, n_blk, stride=CMP_STRIDE), :]
        wa = w1_ref[0, l * HEAD_DIM:(l + 1) * HEAD_DIM, :].astype(BF)
        wb = w1_ref[0, (CMP_STRIDE + l) * HEAD_DIM:(CMP_STRIDE + l + 1) * HEAD_DIM, :].astype(BF)
        ha += jnp.dot((xl + pe_ref[0, l:l + 1, :]).astype(BF), wa, preferred_element_type=F32)
        hb += jnp.dot((xl + pe_ref[0, CMP_STRIDE + l:CMP_STRIDE + l + 1, :]).astype(BF), wb,
                      preferred_element_type=F32)
    hid = ha + pltpu.roll(hb, n_blk - 1, axis=0)
    act = jax.nn.gelu(hid, approximate=True).astype(BF)
    o_ref[0, 0] = jnp.dot(act, w2_ref[0].astype(BF), preferred_element_type=F32)


def _compress(planes, pe, w1, w2, *, batch, seq):
    rows = seq // CMP_STRIDE
    return pl.pallas_call(
        _compress_kernel,
        grid=(2 * NSA_GROUPS, batch),
        in_specs=[
            pl.BlockSpec((1, seq, HEAD_DIM), lambda c, b: (c, b, 0)),
            pl.BlockSpec((1, CMP_LEN, HEAD_DIM), lambda c, b: (c // NSA_GROUPS, 0, 0)),
            pl.BlockSpec((1, CMP_LEN * HEAD_DIM, CMP_HIDDEN), lambda c, b: (c // NSA_GROUPS, 0, 0)),
            pl.BlockSpec((1, CMP_HIDDEN, HEAD_DIM), lambda c, b: (c // NSA_GROUPS, 0, 0)),
        ],
        out_specs=pl.BlockSpec((1, 1, rows, HEAD_DIM), lambda c, b: (c, b, 0, 0)),
        out_shape=jax.ShapeDtypeStruct((2 * NSA_GROUPS, batch, rows, HEAD_DIM), F32),
        compiler_params=_params(("arbitrary", "arbitrary")),
        name="nsa_compress",
    )(planes, pe, w1, w2)


def _stacked_softmax_pv(s_raw, bias, v, scale, n_stack, tq):
    exp2_scale = scale * float(np.log2(np.e))
    es = []
    for h in range(n_stack):
        u = s_raw[h * tq:(h + 1) * tq] + bias
        es.append(jnp.exp2((u - jnp.max(u, axis=-1, keepdims=True)) * exp2_scale).astype(BF))
    v_ones = jnp.concatenate([v, jnp.ones_like(v)], axis=1)
    o = jnp.dot(jnp.concatenate(es, axis=0), v_ones, preferred_element_type=F32)
    d = v.shape[1]
    return o[:, :d] / o[:, d:d + 1]


def _nsa_kernel(q_ref, ks_ref, kw_ref, vs_ref, vw_ref, kvc_ref, gate_ref, cos_ref, sin_ref, m_ref, o_ref,
                *, tq, seq, key_step):
    qi = pl.program_id(1)
    hg = HEADS_PER_GROUP
    scale = HEAD_DIM ** -0.5
    n_cmp = (seq - CMP_LEN) // CMP_STRIDE + 1
    n_slc = seq // SLC_LEN
    band = WIN + tq

    t_row = qi * tq + lax.broadcasted_iota(jnp.int32, (tq, 1), 0)
    cos = jnp.concatenate([cos_ref[...]] * hg, axis=0)
    sin = jnp.concatenate([sin_ref[...]] * hg, axis=0)
    gates = jax.nn.sigmoid(gate_ref[0])

    n_lane = lax.broadcasted_iota(jnp.int32, (1, LANES), 1)
    cmask = (n_lane * CMP_STRIDE + (CMP_LEN - 1) <= t_row) & (n_lane < n_cmp)
    cbias = jnp.where(cmask, 0.0, NEG)
    j_blk = lax.broadcasted_iota(jnp.int32, (n_slc, 1), 0)
    slc_shift = SLC_LEN.bit_length() - 1
    cur = jnp.right_shift(qi * tq + lax.broadcasted_iota(jnp.int32, (1, tq), 1), slc_shift)
    forced = (j_blk == 0) | (j_blk == cur) | (j_blk == cur - 1)
    w_start = pl.multiple_of(jnp.maximum(qi * tq - WIN, 0), tq)
    w_pos = w_start + lax.broadcasted_iota(jnp.int32, (1, band), 1)
    wbias = jnp.where((w_pos <= t_row) & (w_pos > t_row - WIN), 0.0, NEG)

    def group_queries(g):
        qg = q_ref[0, :, g * hg * HEAD_DIM:(g + 1) * hg * HEAD_DIM].astype(F32)
        qp = jnp.concatenate([qg[:, h * HEAD_DIM:(h + 1) * HEAD_DIM] for h in range(hg)], axis=0)
        return qp.astype(BF), _rope_lanes(qp, cos, sin).astype(BF)

    def compressed_branch(g, qp):
        kc = kvc_ref[g, 0].astype(BF)
        vc = kvc_ref[NSA_GROUPS + g, 0].astype(BF)
        s = _dot_t(qp, kc)
        ps = []
        for h in range(hg):
            u = s[h * tq:(h + 1) * tq] * scale + cbias
            e, l = _softmax_rows(u)
            ps.append(jnp.where(cmask, e / l, 0.0))
        o_cmp = jnp.dot(jnp.concatenate(ps, axis=0).astype(BF), vc, preferred_element_type=F32)

        p_sum = ps[0]
        for h in range(1, hg):
            p_sum = p_sum + ps[h]
        imp = lax.dot_general(m_ref[...], p_sum, (((1,), (1,)), ((), ())), preferred_element_type=F32,
                              precision=lax.Precision.HIGHEST)
        imp = jnp.where(j_blk > cur, -jnp.inf, jnp.where(forced, jnp.inf, imp))
        rank = jnp.zeros((n_slc, tq), jnp.int32)
        for i in range(n_slc):
            ci = imp[i:i + 1, :]
            beats = (ci > imp) | ((ci == imp) & (j_blk > i))
            rank = rank + beats.astype(jnp.int32)
        return o_cmp, jnp.where(rank < SLC_TOP, 0.0, NEG).astype(BF)

    qs = [group_queries(g) for g in range(NSA_GROUPS)]
    cmps = [compressed_branch(g, qs[g][0]) for g in range(NSA_GROUPS)]

    def attend(width):
        blocks = width // SLC_LEN
        expand = jnp.where(jnp.right_shift(lax.broadcasted_iota(jnp.int32, (blocks, width), 1), slc_shift)
                           == lax.broadcasted_iota(jnp.int32, (blocks, width), 0), 1.0, 0.0).astype(BF)
        s_pos = lax.broadcasted_iota(jnp.int32, (1, width), 1)
        lanes = [slice(g * HEAD_DIM, (g + 1) * HEAD_DIM) for g in range(NSA_GROUPS)]
        s_win = [_dot_t(qs[g][1], kw_ref[0, pl.ds(w_start, band), lanes[g]]) for g in range(NSA_GROUPS)]
        s_slc = [_dot_t(qs[g][1], ks_ref[0, :width, lanes[g]]) for g in range(NSA_GROUPS)]
        for g in range(NSA_GROUPS):
            o_win = _stacked_softmax_pv(s_win[g], wbias, vw_ref[0, pl.ds(w_start, band), lanes[g]], scale, hg, tq)
            o_cmp, sel_neg = cmps[g]
            bias = lax.dot_general(sel_neg[:blocks], expand, (((0,), (0,)), ((), ())),
                                   preferred_element_type=F32)
            bias = jnp.where(s_pos <= t_row, bias, NEG)
            o_slc = _stacked_softmax_pv(s_slc[g], bias, vs_ref[0, :width, lanes[g]], scale, hg, tq)
            for hh in range(hg):
                h = g * hg + hh
                rows = slice(hh * tq, (hh + 1) * tq)
                o = (o_cmp[rows] * gates[:, 3 * h:3 * h + 1]
                     + o_slc[rows] * gates[:, 3 * h + 1:3 * h + 2]
                     + o_win[rows] * gates[:, 3 * h + 2:3 * h + 3])
                o_ref[0, :, h * HEAD_DIM:(h + 1) * HEAD_DIM] = o.astype(o_ref.dtype)

    n_widths = seq // key_step
    super_row = qi // (key_step // tq)
    for k in range(n_widths):
        pl.when(super_row == k)(functools.partial(attend, (k + 1) * key_step))


def _nsa(main, kvc, planes, gate_plane, cos, sin, m_cs, *, batch, seq, tq=QUERY_TILE, key_step=NSA_KEY_STEP):
    nq = seq // tq
    kv_w = NSA_GROUPS * HEAD_DIM
    kv0 = NSA_HEADS * HEAD_DIM // kv_w
    return pl.pallas_call(
        functools.partial(_nsa_kernel, tq=tq, seq=seq, key_step=key_step),
        grid=(batch, nq),
        in_specs=[
            pl.BlockSpec((1, tq, NSA_HEADS * HEAD_DIM), lambda b, i: (b, i, 0)),
            pl.BlockSpec((1, seq, kv_w), lambda b, i: (b, 0, kv0)),
            pl.BlockSpec((1, seq, kv_w), lambda b, i: (b, 0, kv0 + 2)),
            pl.BlockSpec((1, seq, kv_w), lambda b, i: (b, 0, kv0 + 1)),
            pl.BlockSpec((1, seq, kv_w), lambda b, i: (b, 0, kv0 + 3)),
            pl.BlockSpec((2 * NSA_GROUPS, 1, LANES, HEAD_DIM), lambda b, i: (0, b, 0, 0)),
            pl.BlockSpec((1, tq, LANES), lambda b, i: (gate_plane, b * nq + i, 0)),
            pl.BlockSpec((tq, LANES), lambda b, i: (i, 0)),
            pl.BlockSpec((tq, LANES), lambda b, i: (i, 0)),
            pl.BlockSpec(m_cs.shape, lambda b, i: (0, 0)),
        ],
        out_specs=pl.BlockSpec((1, tq, NSA_HEADS * HEAD_DIM), lambda b, i: (b, i, 0)),
        out_shape=jax.ShapeDtypeStruct((batch, seq, NSA_HEADS * HEAD_DIM), BF),
        compiler_params=_params(("parallel", "arbitrary")),
        name="nsa_attention",
    )(main, main, main, main, main, kvc, planes, cos, sin, m_cs)


def _cumdecay_kernel(x_ref, b_ref, o_ref):
    z = x_ref[0] + b_ref[...]
    c = jnp.minimum(z, 0.0) - jnp.log1p(jnp.exp(-jnp.abs(z)))
    rows = c.shape[0]
    r = lax.broadcasted_iota(jnp.int32, (rows, 1), 0)
    k = 1
    while k < rows:
        c = c + jnp.where(r >= k, pltpu.roll(c, k, axis=0), 0.0)
        k *= 2
    o_ref[0] = c.T


def _cumdecay(planes, plane, bias, *, batch, seq):
    return pl.pallas_call(
        _cumdecay_kernel,
        grid=(batch,),
        in_specs=[pl.BlockSpec((1, seq, LANES), lambda b: (plane, b, 0)),
                  pl.BlockSpec((1, LANES), lambda b: (0, 0))],
        out_specs=pl.BlockSpec((1, LANES, seq), lambda b: (b, 0, 0)),
        out_shape=jax.ShapeDtypeStruct((batch, LANES, seq), F32),
        compiler_params=_params(("arbitrary",)),
        name="fox_cumdecay",
    )(planes, bias)


def _attn_kernel(*refs, tq, seq, scale, n_qk, decay):
    q_ref = refs[0]
    k_refs = refs[1:1 + n_qk]
    v_ref = refs[1 + n_qk]
    k = 2 + n_qk
    if decay:
        ck_ref = refs[k]
        k += 1
    o_ref = refs[k]

    row = lax.broadcasted_iota(jnp.int32, (tq, 1), 0)
    col = lax.broadcasted_iota(jnp.int32, (1, tq), 1)
    tri = col <= row
    log2_e = float(np.log2(np.e))
    exp2_scale = scale * log2_e

    k_all = k_refs[0][0] if n_qk == 1 else jnp.concatenate([k_ref[0] for k_ref in k_refs], axis=1)

    v_ones = jnp.concatenate([v_ref[0], jnp.ones((seq, HEAD_DIM), BF)], axis=1)
    if decay:
        head_row = lax.broadcasted_iota(jnp.int32, (ck_ref.shape[1], 1), 0) == pl.program_id(1)
        ck_head = jnp.sum(jnp.where(head_row, ck_ref[0], 0.0), axis=0, keepdims=True)

    n_tiles = seq // tq
    scores = lambda i: _dot_t(q_ref[0, i * tq:(i + 1) * tq, :], k_all[:(i + 1) * tq])
    s_next = scores(n_tiles - 1)
    for i in reversed(range(n_tiles)):
        lo, hi = i * tq, (i + 1) * tq
        s = s_next
        if i > 0:
            s_next = scores(i - 1)
        if decay:
            ck = ck_head[:, :hi] * log2_e
            cq = jnp.sum(jnp.where(col == row, ck[:, lo:hi], 0.0), axis=1, keepdims=True)
            s = s * exp2_scale + (cq - ck)
            prob = lambda u, m: jnp.exp2(u - m)
        else:
            prob = lambda u, m: jnp.exp2((u - m) * exp2_scale)
        s_diag = jnp.where(tri, s[:, lo:hi], NEG)
        m = jnp.max(s_diag, axis=-1, keepdims=True)
        if i > 0:
            m = jnp.maximum(m, jnp.max(s[:, :lo], axis=-1, keepdims=True))
        e = prob(s_diag, m).astype(BF)
        if i > 0:
            e = jnp.concatenate([prob(s[:, :lo], m).astype(BF), e], axis=1)
        o = jnp.dot(e, v_ones[:hi], preferred_element_type=F32)
        o_ref[0, lo:hi, :] = (o[:, :HEAD_DIM] / o[:, HEAD_DIM:HEAD_DIM + 1]).astype(o_ref.dtype)


def _attention(q_arr, q_blk0, k_arrs, k_blk0s, k_strides, v_arr, v_blk0, v_stride=1, *, batch, seq, heads, scale,
               decay=None, tq=QUERY_TILE, name="attention"):
    n_qk = len(k_arrs)
    in_specs = [pl.BlockSpec((1, seq, n_qk * HEAD_DIM), lambda b, h: (b, 0, q_blk0 + h))]
    args = [q_arr]
    for arr, blk0, stride in zip(k_arrs, k_blk0s, k_strides):
        in_specs.append(pl.BlockSpec((1, seq, HEAD_DIM),
                                     lambda b, h, blk0=blk0, stride=stride: (b, 0, blk0 + stride * h)))
        args.append(arr)
    in_specs.append(pl.BlockSpec((1, seq, HEAD_DIM), lambda b, h: (b, 0, v_blk0 + v_stride * h)))
    args.append(v_arr)
    if decay is not None:
        cum, row0 = decay
        assert row0 % heads == 0 and heads % 8 == 0
        in_specs.append(pl.BlockSpec((1, heads, seq), lambda b, h: (b, row0 // heads, 0)))
        args.append(cum)
    return pl.pallas_call(
        functools.partial(_attn_kernel, tq=tq, seq=seq, scale=scale, n_qk=n_qk, decay=decay is not None),
        grid=(batch, heads),
        in_specs=in_specs,
        out_specs=pl.BlockSpec((1, seq, HEAD_DIM), lambda b, h: (b, 0, h)),
        out_shape=jax.ShapeDtypeStruct((batch, seq, heads * HEAD_DIM), BF),
        compiler_params=_params(("parallel", "arbitrary")),
        name=name,
    )(*args)


def _mla_latent_kernel(xn_ref, wt_ref, wkr_ref, gq_ref, gkv_ref, cos_ref, sin_ref, cq_ref, ckv_ref, kr_ref):
    h = xn_ref[...]
    c = _dot_t(h, wt_ref[...].astype(BF))
    cq_ref[...] = _rms_normed(c[:, :MLA_Q_RANK], gq_ref[...]).astype(BF)
    ckv_ref[...] = _rms_normed(c[:, MLA_Q_RANK:], gkv_ref[...]).astype(BF)
    kr_ref[...] = _rope_lanes(_dot_t(h, wkr_ref[...].astype(BF)), cos_ref[...], sin_ref[...]).astype(BF)


def _mla_latent(xn, wt, w_kr, gq, gkv, cos, sin, *, seq, tm=ROWS_RESIDENT):
    n = xn.shape[0]
    n_seq = seq // tm
    return pl.pallas_call(
        _mla_latent_kernel,
        grid=(n // tm,),
        in_specs=[
            pl.BlockSpec((tm, xn.shape[1]), lambda i: (i, 0)),
            pl.BlockSpec((MLA_Q_RANK + MLA_KV_RANK, wt.shape[1]), lambda i: (0, 0)),
            pl.BlockSpec(w_kr.shape, lambda i: (0, 0)),
            pl.BlockSpec((1, MLA_Q_RANK), lambda i: (0, 0)),
            pl.BlockSpec((1, MLA_KV_RANK), lambda i: (0, 0)),
            pl.BlockSpec((tm, LANES), lambda i: (i % n_seq, 0)),
            pl.BlockSpec((tm, LANES), lambda i: (i % n_seq, 0)),
        ],
        out_specs=[
            pl.BlockSpec((tm, MLA_Q_RANK), lambda i: (i, 0)),
            pl.BlockSpec((tm, MLA_KV_RANK), lambda i: (i, 0)),
            pl.BlockSpec((tm, LANES), lambda i: (i, 0)),
        ],
        out_shape=[
            jax.ShapeDtypeStruct((n, MLA_Q_RANK), BF),
            jax.ShapeDtypeStruct((n, MLA_KV_RANK), BF),
            jax.ShapeDtypeStruct((n, LANES), BF),
        ],
        compiler_params=_params(("parallel",)),
        name="mla_latent",
    )(xn, wt, w_kr, gq.reshape(1, -1), gkv.reshape(1, -1), cos, sin)


def _rope_tables(seq, dim):
    half = dim // 2
    inv = 1.0 / (ROPE_THETA ** (np.arange(0, dim, 2, dtype=np.float32) / dim))
    ang = np.arange(seq, dtype=np.float32)[:, None] * inv.astype(np.float32)[None, :]
    cos = np.zeros((seq, LANES), np.float32)
    sin = np.zeros((seq, LANES), np.float32)
    cos[:, :half] = np.cos(ang)
    cos[:, LANES // 2:LANES // 2 + half] = np.cos(ang)
    sin[:, :half] = -np.sin(ang)
    sin[:, LANES // 2:LANES // 2 + half] = np.sin(ang)
    return jnp.asarray(cos), jnp.asarray(sin)


def _cmp_to_slc(seq):
    n_cmp = (seq - CMP_LEN) // CMP_STRIDE + 1
    n_slc = seq // SLC_LEN
    c0 = np.arange(n_cmp) * CMP_STRIDE
    c1 = c0 + CMP_LEN
    s0 = np.arange(n_slc) * SLC_LEN
    s1 = s0 + SLC_LEN
    ov = np.clip(np.minimum(c1[:, None], s1[None, :]) - np.maximum(c0[:, None], s0[None, :]), 0, None)
    m = np.zeros((n_slc, LANES), np.float32)
    m[:, :n_cmp] = (ov / CMP_LEN).T
    return jnp.asarray(m)


def _spread_rope_cols(w):
    half = MLA_ROPE // 2
    z = jnp.zeros(w.shape[:-1] + (LANES // 2 - half,), w.dtype)
    return jnp.concatenate([w[..., :half], z, w[..., half:], z], axis=-1)


def _hybrid_mixer(x, xn, g_next, w_in, w_out, layer, cmp_pe, cmp_w1, cmp_w2, f_bias, *, batch, seq):
    n = batch * seq
    nq_w = NSA_HEADS * HEAD_DIM
    kv_w = NSA_GROUPS * HEAD_DIM
    fx_w = FOX_HEADS * HEAD_DIM
    c_cmp, c_slc, c_gate = nq_w, nq_w + 2 * kv_w, nq_w + 6 * kv_w
    c_fox = c_gate + 3 * NSA_HEADS
    c_f = c_fox + 3 * fx_w
    wt = jnp.swapaxes(w_in[layer], 0, 1)

    cos, sin = _rope_tables(seq, HEAD_DIM)
    tn = 4 * kv_w
    main_starts = tuple(range(0, c_cmp, tn)) + tuple(range(c_slc, c_gate, tn)) + tuple(range(c_fox, c_f, tn))
    f_lane0 = LANES - FOX_HEADS
    side_windows = ((c_cmp, c_slc - c_cmp), (c_gate, LANES), (c_f - f_lane0, LANES))
    gate_plane, f_plane = (c_slc - c_cmp) // LANES, (c_slc - c_cmp) // LANES + 1
    main, planes = _hybrid_proj(xn, wt, main_starts, side_windows,
                                (cos, sin, (nq_w // tn, nq_w // tn + 1), (True, True, False, False) * 2), seq=seq,
                                tn=tn)

    kvc = _compress(planes, cmp_pe, cmp_w1, cmp_w2, batch=batch, seq=seq)

    main3 = main.reshape(batch, seq, -1)
    o_nsa = _nsa(main3, kvc, planes, gate_plane, cos, sin, _cmp_to_slc(seq), batch=batch, seq=seq)

    bias = jnp.zeros((1, LANES), F32).at[0, f_lane0:].set(f_bias)
    cum = _cumdecay(planes, f_plane, bias, batch=batch, seq=seq)
    blk = lambda col: col // HEAD_DIM
    c_qf = nq_w + 4 * kv_w
    o_fox = _attention(main3, blk(c_qf), [main3], [blk(c_qf + fx_w)], [1], main3, blk(c_qf + 2 * fx_w),
                       batch=batch, seq=seq, heads=FOX_HEADS, scale=HEAD_DIM ** -0.5,
                       decay=(cum, f_lane0), name="fox_attention")

    return _mm([o_nsa.reshape(n, nq_w), o_fox.reshape(n, fx_w)],
               [_W(w_out, layer=layer, rows=nq_w, row_blk=0), _W(w_out, layer=layer, rows=fx_w, row_blk=1)],
               D_MODEL, out_dtype=F32, tm=ROWS_RESIDENT, tn=D_MODEL, res=x, norm_out=(g_next, BF),
               name="proj_out_hybrid")


def _mla_mixer(x, xn, g_next, w_in, q_norm, kv_norm, w_uq, w_ukv, w_out, layer, *, batch, seq):
    n = batch * seq
    qk = MLA_NOPE + MLA_ROPE
    wt_lat = jnp.swapaxes(w_in, 0, 1)
    w_kr = jnp.swapaxes(_spread_rope_cols(w_in[:, MLA_Q_RANK + MLA_KV_RANK:]), 0, 1)
    wq = w_uq.reshape(MLA_Q_RANK, MLA_HEADS, qk)
    wq = jnp.concatenate([wq[..., :MLA_NOPE], _spread_rope_cols(wq[..., MLA_NOPE:])], axis=-1)
    wq = wq.reshape(MLA_Q_RANK, MLA_HEADS * 2 * HEAD_DIM)

    cos, sin = _rope_tables(seq, MLA_ROPE)
    cq, ckv, kr = _mla_latent(xn, wt_lat, w_kr, q_norm, kv_norm, cos, sin, seq=seq)
    tn = wq.shape[1]
    q = _mm([cq], [_W(wq)], wq.shape[1], out_dtype=BF, tm=ROWS_RESIDENT, tn=tn,
            rope=(cos, sin, (0, 1), (False, True) * (tn // (2 * LANES))), seq=seq, name="proj_mla_q")
    kv = _mm([ckv], [_W(w_ukv)], w_ukv.shape[1], out_dtype=BF, tm=ROWS_RESIDENT, tn=w_ukv.shape[1],
             name="proj_mla_kv")

    q3 = q.reshape(batch, seq, -1)
    kv3 = kv.reshape(batch, seq, -1)
    kr3 = kr.reshape(batch, seq, LANES)
    o = _attention(q3, 0, [kv3, kr3], [0, 0], [2, 0], kv3, 1, 2, batch=batch, seq=seq,
                   heads=MLA_HEADS, scale=qk ** -0.5, name="mla_attention")
    return _mm([o.reshape(n, MLA_HEADS * MLA_V)], [_W(w_out, layer=layer)], D_MODEL, out_dtype=F32, tm=ROWS_RESIDENT,
               tn=D_MODEL, res=x, norm_out=(g_next, BF), name="proj_out_mla")


def _ffn(x, xn, norms, w_in, w_out, layer, norm_out, keep_raw=True):
    depth, d = norms.shape
    a, w_out_bf = _ffn_up(x if xn is None else xn, norms.reshape(depth, 1, d), w_in, w_out, layer)
    return _mm([a], [_W(w_out_bf)], d, out_dtype=F32, tm=ROWS_RESIDENT, tn=d, res=x, norm_out=norm_out,
               keep_raw=keep_raw, name="ffn_down")


def kernel(x, ffn1_norm, ffn1_w_in, ffn1_w_out, mix_norm, ffn2_norm, ffn2_w_in, ffn2_w_out, hyb_w_in, hyb_w_out, nsa_cmp_pe, nsa_cmp_w1, nsa_cmp_w2, fox_f_bias, mla_w_in, mla_q_norm, mla_kv_norm, mla_w_uq, mla_w_ukv, mla_w_out, final_norm):
    batch, seq, d = x.shape
    depth = ffn1_norm.shape[0]
    x = x.reshape(batch * seq, d)
    xn = None
    for i in range(depth):
        x, xn = _ffn(x, xn, ffn1_norm, ffn1_w_in, ffn1_w_out, i, (mix_norm[i], BF))
        if i % 2 == 0:
            e = i // 2
            x, xn = _hybrid_mixer(x, xn, ffn2_norm[i], hyb_w_in, hyb_w_out, e, nsa_cmp_pe[e], nsa_cmp_w1[e],
                                  nsa_cmp_w2[e], fox_f_bias[e], batch=batch, seq=seq)
        else:
            o = i // 2
            x, xn = _mla_mixer(x, xn, ffn2_norm[i], mla_w_in[o], mla_q_norm[o], mla_kv_norm[o], mla_w_uq[o],
                               mla_w_ukv[o], mla_w_out, o, batch=batch, seq=seq)
        if i + 1 < depth:
            x, xn = _ffn(x, xn, ffn2_norm, ffn2_w_in, ffn2_w_out, i, (ffn1_norm[i + 1], BF))
        else:
            out = _ffn(x, xn, ffn2_norm, ffn2_w_in, ffn2_w_out, i, (final_norm, F32), keep_raw=False)
    return out.reshape(batch, seq, d)
```
